```python
import jax
import jax.numpy as jnp
from jax import lax
import numpy as np

D_MODEL = 2048
BATCH = 1
SEQ = 16384
DEPTH = 2

RET_HEADS = 8
RET_DK = 128
RET_DV = 128
RET_W = RET_HEADS * RET_DV
RET_CHUNK = 128
ROPE_BASE = 10000.0
LRU_W = 1024
LRU_BLOCKS = 8
LRU_BS = LRU_W // LRU_BLOCKS
LRU_C = 8.0
CONV_K = 4
GDN_HEADS = 8
GDN_DK = 128
GDN_DV = 128
GDN_W = GDN_HEADS * GDN_DV
GDN_CHUNK = 64
N_BRANCH = 3
N_GROUPS = 4
EXPERTS_PER_GROUP = 8
N_EXPERTS = N_GROUPS * EXPERTS_PER_GROUP
TOP_K = 2
D_EXPERT = 512
MOE_BLOCK = 128
EPS = 1e-6

IN_SPLITS = [RET_HEADS * RET_DK, RET_HEADS * RET_DK, RET_W, RET_W,
             LRU_W, LRU_W,
             GDN_HEADS * GDN_DK, GDN_HEADS * GDN_DK, GDN_W, GDN_W,
             GDN_HEADS, GDN_HEADS,
             N_BRANCH * D_MODEL]
IN_COLS = 4 * RET_W + 2 * LRU_W + 4 * GDN_W + 2 * GDN_HEADS + N_BRANCH * D_MODEL

kernel_name = 'hybrid_retention_rglru_gdeltanet_hmoe'

F32 = jnp.float32


def rms_norm(x, g):
    x32 = x.astype(F32)
    y = x32 * lax.rsqrt(jnp.mean(x32 * x32, axis=-1, keepdims=True) + EPS)
    return (y * g.astype(F32)).astype(x.dtype)


def head_rms(x):
    x32 = x.astype(F32)
    return (x32 * lax.rsqrt(jnp.mean(x32 * x32, axis=-1, keepdims=True) + EPS)).astype(x.dtype)


def l2_normalize(x):
    x32 = x.astype(F32)
    return x32 * lax.rsqrt(jnp.sum(x32 * x32, axis=-1, keepdims=True) + EPS)


def rope(x, positions):
    d = x.shape[-1]
    half = d // 2
    inv_freq = ROPE_BASE ** (-jnp.arange(half, dtype=F32) * (2.0 / d))
    ang = positions.astype(F32)[..., None] * inv_freq
    cos = jnp.cos(ang)[:, :, None, :]
    sin = jnp.sin(ang)[:, :, None, :]
    x32 = x.astype(F32)
    x1, x2 = x32[..., :half], x32[..., half:]
    return jnp.concatenate([x1 * cos - x2 * sin, x2 * cos + x1 * sin], axis=-1).astype(x.dtype)


def causal_dwconv(x, w):
    K = w.shape[0]
    S = x.shape[1]
    xp = jnp.pad(x, ((0, 0), (K - 1, 0), (0, 0)))
    y = xp[:, 0:S] * w[0]
    for k in range(1, K):
        y = y + xp[:, k:k + S] * w[k]
    return y


def to_chunks(t, C):
    B, S = t.shape[0], t.shape[1]
    t = t.astype(F32).reshape((B, S // C, C) + t.shape[2:])
    return jnp.moveaxis(t, 3, 1)


def from_chunks(t):
    B, H, N, C, d = t.shape
    return jnp.moveaxis(t, 1, 3).reshape(B, N * C, H, d)


def retention_chunkwise(q, k, v):
    out_dtype = v.dtype
    B, S, H, dk = q.shape
    dv = v.shape[-1]
    C = RET_CHUNK
    log_g = jnp.log1p(-jnp.exp2(-5.0 - jnp.arange(H, dtype=F32)))
    q = to_chunks(q, C)
    k = to_chunks(k, C) * (dk ** -0.5)
    v = to_chunks(v, C)
    pos = jnp.arange(C, dtype=F32)
    rel = pos[:, None] - pos[None, :]
    intra = jnp.where(rel >= 0, jnp.exp(log_g[:, None, None] * jnp.maximum(rel, 0.0)), 0.0)
    scores = jnp.einsum('bhncd,bhnmd->bhncm', q, k) * intra[:, None]
    o_intra = jnp.einsum('bhncm,bhnme->bhnce', scores, v)
    k_dec = k * jnp.exp(log_g[:, None] * (C - 1.0 - pos))[:, None, :, None]
    q_dec = q * jnp.exp(log_g[:, None] * (pos + 1.0))[:, None, :, None]
    kv = jnp.einsum('bhncd,bhnce->nbhde', k_dec, v)
    chunk_decay = jnp.exp(log_g * C)[:, None, None]

    def step(R, kv_n):
        return R * chunk_decay + kv_n, R

    _, R_prev = lax.scan(step, jnp.zeros((B, H, dk, dv), F32), kv)
    o_inter = jnp.einsum('bhncd,nbhde->bhnce', q_dec, R_prev)
    return from_chunks(o_intra + o_inter).astype(out_dtype)


def block_diag_linear(x, w, b):
    B, S, _ = x.shape
    nb, bs, _ = w.shape
    y = jnp.einsum('bsni,nij->bsnj', x.reshape(B, S, nb, bs), w)
    return y.reshape(B, S, nb * bs) + b


def rg_lru(x, wa, ba, wx, bx, lam):
    dtype = x.dtype
    r = jax.nn.sigmoid(block_diag_linear(x, wa, ba).astype(F32))
    i = jax.nn.sigmoid(block_diag_linear(x, wx, bx).astype(F32))
    log_a = -LRU_C * r * jax.nn.softplus(-lam.astype(F32))
    a = jnp.exp(log_a)
    b = jnp.sqrt(-jnp.expm1(2.0 * log_a)) * (i * x.astype(F32))

    def combine(e, l):
        return (e[0] * l[0], l[0] * e[1] + l[1])

    _, h = lax.associative_scan(combine, (a, b), axis=1)
    return h.astype(dtype)


def gated_delta_chunked(q, k, v, g, beta):
    out_dtype = v.dtype
    B, S, H, dk = q.shape
    dv = v.shape[-1]
    C = GDN_CHUNK
    q = to_chunks(q, C) * (dk ** -0.5)
    k = to_chunks(k, C)
    v = to_chunks(v, C)
    g = to_chunks(g, C)
    beta = to_chunks(beta, C)
    gc = jnp.cumsum(g, axis=-1)
    incl = jnp.tril(jnp.ones((C, C), bool))
    strict = jnp.tril(jnp.ones((C, C), bool), -1)
    decay = jnp.exp(jnp.where(incl, gc[..., :, None] - gc[..., None, :], -jnp.inf))
    kb = k * beta[..., None]
    vb = v * beta[..., None]
    m = jnp.where(strict, jnp.einsum('bhncd,bhnmd->bhncm', kb, k) * decay, 0.0)
    a = m + jnp.eye(C, dtype=F32)
    rhs = jnp.concatenate([vb, kb * jnp.exp(gc)[..., None]], axis=-1)
    sol = lax.linalg.triangular_solve(a, rhs, left_side=True, lower=True, unit_diagonal=True)
    u, w = sol[..., :dv], sol[..., dv:]
    attn = jnp.einsum('bhncd,bhnmd->bhncm', q, k) * decay
    q_dec = q * jnp.exp(gc)[..., None]
    k_dec = k * jnp.exp(gc[..., -1:] - gc)[..., None]
    g_last = jnp.exp(gc[..., -1])
    xs = (jnp.moveaxis(u, 2, 0), jnp.moveaxis(w, 2, 0), jnp.moveaxis(attn, 2, 0),
          jnp.moveaxis(q_dec, 2, 0), jnp.moveaxis(k_dec, 2, 0), jnp.moveaxis(g_last, 2, 0))

    def step(state, inp):
        u_n, w_n, attn_n, qd_n, kd_n, gl_n = inp
        v_new = u_n - jnp.einsum('bhck,bhkv->bhcv', w_n, state)
        o = jnp.einsum('bhck,bhkv->bhcv', qd_n, state) + jnp.einsum('bhcm,bhmv->bhcv', attn_n, v_new)
        state = state * gl_n[..., None, None] + jnp.einsum('bhck,bhcv->bhkv', kd_n, v_new)
        return state, o

    _, o = lax.scan(step, jnp.zeros((B, H, dk, dv), F32), xs)
    return from_chunks(jnp.moveaxis(o, 0, 2)).astype(out_dtype)


def hybrid_mixer(h, positions, w_in, lru_conv_w, lru_conv_b, lru_wa, lru_ba, lru_wx, lru_bx,
                 lru_lambda, gdn_conv_w, gdn_a_log, gdn_dt_bias, gdn_norm_g, branch_proj, w_out):
    B, S, _ = h.shape
    z = h @ w_in
    idx = [int(i) for i in np.cumsum(IN_SPLITS)[:-1]]
    rq, rk, rv, rg, lx, lg, gq, gk, gv, gz, ga, gb, br = jnp.split(z, idx, axis=-1)

    rq = rope(rq.reshape(B, S, RET_HEADS, RET_DK), positions)
    rk = rope(rk.reshape(B, S, RET_HEADS, RET_DK), positions)
    ro = retention_chunkwise(rq, rk, rv.reshape(B, S, RET_HEADS, RET_DV))
    y_ret = (head_rms(ro) * jax.nn.silu(rg.reshape(B, S, RET_HEADS, RET_DV))).reshape(B, S, RET_W)

    lx = causal_dwconv(lx, lru_conv_w) + lru_conv_b
    y_lru = rg_lru(lx, lru_wa, lru_ba, lru_wx, lru_bx, lru_lambda) * jax.nn.gelu(lg)

    qkv = jax.nn.silu(causal_dwconv(jnp.concatenate([gq, gk, gv], axis=-1), gdn_conv_w))
    gq, gk, gv = jnp.split(qkv, [GDN_HEADS * GDN_DK, 2 * GDN_HEADS * GDN_DK], axis=-1)
    gq = l2_normalize(gq.reshape(B, S, GDN_HEADS, GDN_DK))
    gk = l2_normalize(gk.reshape(B, S, GDN_HEADS, GDN_DK))
    beta = jax.nn.sigmoid(gb.astype(F32))
    g = -jnp.exp(gdn_a_log.astype(F32)) * jax.nn.softplus(ga.astype(F32) + gdn_dt_bias.astype(F32))
    go = gated_delta_chunked(gq, gk, gv.reshape(B, S, GDN_HEADS, GDN_DV), g, beta)
    y_gdn = (head_rms(go) * gdn_norm_g * jax.nn.silu(gz.reshape(B, S, GDN_HEADS, GDN_DV))).reshape(B, S, GDN_W)

    gates = jax.nn.sigmoid(br).reshape(B, S, N_BRANCH, D_MODEL)
    merged = (gates[:, :, 0] * (y_ret @ branch_proj[0])
              + gates[:, :, 1] * (y_lru @ branch_proj[1])
              + gates[:, :, 2] * (y_gdn @ branch_proj[2]))
    return merged @ w_out


def hier_moe(h, wg, bg, we, be, w1, w3, w2):
    B, S, D = h.shape
    T = B * S
    xf = h.reshape(T, D)
    logits_g = (xf @ wg + bg).astype(F32)
    p_g = jax.nn.softmax(logits_g, axis=-1)
    pg_top, g_sel = lax.top_k(p_g, 1)
    logits_e = (xf @ we + be).astype(F32).reshape(T, N_GROUPS, EXPERTS_PER_GROUP)
    logits_in = jnp.take_along_axis(logits_e, g_sel[:, :, None], axis=1)[:, 0]
    top_v, top_i = lax.top_k(logits_in, TOP_K)
    weights = jax.nn.softmax(top_v, axis=-1) * pg_top
    expert = g_sel * EXPERTS_PER_GROUP + top_i

    A = T * TOP_K
    e_flat = expert.reshape(A)
    tok_flat = jnp.repeat(jnp.arange(T, dtype=jnp.int32), TOP_K)
    w_flat = weights.reshape(A)
    order = jnp.argsort(e_flat)
    e_s, tok_s, w_s = e_flat[order], tok_flat[order], w_flat[order]
    counts = jnp.bincount(e_flat, length=N_EXPERTS)
    start = jnp.cumsum(counts) - counts
    padded = ((counts + MOE_BLOCK - 1) // MOE_BLOCK) * MOE_BLOCK
    pend = jnp.cumsum(padded)
    pstart = pend - padded
    dest = pstart[e_s] + (jnp.arange(A) - start[e_s])
    n_blocks = -(-A // MOE_BLOCK) + N_EXPERTS
    P = n_blocks * MOE_BLOCK
    buf_tok = jnp.full((P,), T, jnp.int32).at[dest].set(tok_s)
    buf_w = jnp.zeros((P,), F32).at[dest].set(w_s)
    block_e = jnp.clip(jnp.searchsorted(pend, jnp.arange(n_blocks) * MOE_BLOCK, side='right'),
                       0, N_EXPERTS - 1)
    x_pad = jnp.concatenate([xf, jnp.zeros((1, D), xf.dtype)], axis=0)
    xb = x_pad[buf_tok].reshape(n_blocks, MOE_BLOCK, D)

    def expert_block(args):
        xblk, e = args
        return (jax.nn.silu(xblk @ w1[e]) * (xblk @ w3[e])) @ w2[e]

    yb = lax.map(expert_block, (xb, block_e)).reshape(P, D)
    out = jnp.zeros((T + 1, D), xf.dtype).at[buf_tok].add(yb * buf_w[:, None].astype(xf.dtype))
    return out[:T].reshape(B, S, D)


def setup_inputs(seed: int = 0) -> dict:
    key = jax.random.key(seed)
    keys = jax.random.split(key, 40)
    ctr = [0]

    def nk():
        ctr[0] += 1
        return keys[ctr[0] - 1]

    def nrm(shape, scale):
        return jax.random.normal(nk(), shape, F32) * scale

    L, D = DEPTH, D_MODEL
    x = nrm((BATCH, SEQ, D), 1.0)
    c = nrm((BATCH, D), 1.0)
    positions = (jax.random.randint(nk(), (BATCH, 1), 0, 1024, jnp.int32)
                 + jnp.arange(SEQ, dtype=jnp.int32)[None, :])
    ada_w = nrm((L, D, 6 * D), 0.5 * D ** -0.5)
    ada_b = nrm((L, 6 * D), 0.01)
    norm1_g = 1.0 + nrm((L, D), 0.01)
    norm2_g = 1.0 + nrm((L, D), 0.01)
    w_in = nrm((L, D, IN_COLS), D ** -0.5)
    lru_conv_w = nrm((L, CONV_K, LRU_W), CONV_K ** -0.5)
    lru_conv_b = nrm((L, LRU_W), 0.01)
    lru_wa = nrm((L, LRU_BLOCKS, LRU_BS, LRU_BS), LRU_BS ** -0.5)
    lru_ba = nrm((L, LRU_W), 0.1)
    lru_wx = nrm((L, LRU_BLOCKS, LRU_BS, LRU_BS), LRU_BS ** -0.5)
    lru_bx = nrm((L, LRU_W), 0.1)
    a0 = jax.random.uniform(nk(), (L, LRU_W), F32, 0.9, 0.999)
    s = a0 ** (1.0 / LRU_C)
    lru_lambda = jnp.log(s) - jnp.log1p(-s)
    gdn_conv_w = nrm((L, CONV_K, 2 * GDN_HEADS * GDN_DK + GDN_W), CONV_K ** -0.5)
    gdn_a_log = jnp.log(jax.random.uniform(nk(), (L, GDN_HEADS), F32, 1.0, 16.0))
    dt = jnp.exp(jax.random.uniform(nk(), (L, GDN_HEADS), F32, float(np.log(1e-3)), float(np.log(1e-1))))
    gdn_dt_bias = dt + jnp.log(-jnp.expm1(-dt))
    gdn_norm_g = 1.0 + nrm((L, GDN_DV), 0.01)
    branch_proj = nrm((L, N_BRANCH, RET_W, D), RET_W ** -0.5)
    w_out = nrm((L, D, D), D ** -0.5)
    router_grp_w = nrm((L, D, N_GROUPS), D ** -0.5)
    router_grp_b = nrm((L, N_GROUPS), 0.01)
    router_exp_w = nrm((L, D, N_EXPERTS), D ** -0.5)
    router_exp_b = nrm((L, N_EXPERTS), 0.01)
    moe_w1 = nrm((L, N_EXPERTS, D, D_EXPERT), D ** -0.5)
    moe_w3 = nrm((L, N_EXPERTS, D, D_EXPERT), D ** -0.5)
    moe_w2 = nrm((L, N_EXPERTS, D_EXPERT, D), D_EXPERT ** -0.5)
    final_g = 1.0 + nrm((D,), 0.01)
    return {'x': x, 'c': c, 'positions': positions, 'ada_w': ada_w, 'ada_b': ada_b,
            'norm1_g': norm1_g, 'norm2_g': norm2_g, 'w_in': w_in,
            'lru_conv_w': lru_conv_w, 'lru_conv_b': lru_conv_b, 'lru_wa': lru_wa, 'lru_ba': lru_ba,
            'lru_wx': lru_wx, 'lru_bx': lru_bx, 'lru_lambda': lru_lambda,
            'gdn_conv_w': gdn_conv_w, 'gdn_a_log': gdn_a_log, 'gdn_dt_bias': gdn_dt_bias,
            'gdn_norm_g': gdn_norm_g, 'branch_proj': branch_proj, 'w_out': w_out,
            'router_grp_w': router_grp_w, 'router_grp_b': router_grp_b,
            'router_exp_w': router_exp_w, 'router_exp_b': router_exp_b,
            'moe_w1': moe_w1, 'moe_w3': moe_w3, 'moe_w2': moe_w2, 'final_g': final_g}


def reference(x, c, positions, ada_w, ada_b, norm1_g, norm2_g, w_in, lru_conv_w, lru_conv_b,
              lru_wa, lru_ba, lru_wx, lru_bx, lru_lambda, gdn_conv_w, gdn_a_log, gdn_dt_bias,
              gdn_norm_g, branch_proj, w_out, router_grp_w, router_grp_b, router_exp_w,
              router_exp_b, moe_w1, moe_w3, moe_w2, final_g):
    cond = jax.nn.silu(c)
    for l in range(DEPTH):
        mod = (cond @ ada_w[l] + ada_b[l])[:, None, :]
        sh1, sc1, gt1, sh2, sc2, gt2 = jnp.split(mod, 6, axis=-1)
        h = rms_norm(x, norm1_g[l]) * (1 + sc1) + sh1
        x = x + gt1 * hybrid_mixer(h, positions, w_in[l], lru_conv_w[l], lru_conv_b[l], lru_wa[l],
                                   lru_ba[l], lru_wx[l], lru_bx[l], lru_lambda[l], gdn_conv_w[l],
                                   gdn_a_log[l], gdn_dt_bias[l], gdn_norm_g[l], branch_proj[l], w_out[l])
        h = rms_norm(x, norm2_g[l]) * (1 + sc2) + sh2
        x = x + gt2 * hier_moe(h, router_grp_w[l], router_grp_b[l], router_exp_w[l], router_exp_b[l],
                               moe_w1[l], moe_w3[l], moe_w2[l])
    return rms_norm(x, final_g)
```

```python
import functools
import math

import numpy as np
import jax
import jax.numpy as jnp
from jax import lax
from jax.experimental import pallas as pl
from jax.experimental.pallas import tpu as pltpu

F32 = jnp.float32
BF16 = jnp.bfloat16
HIGHEST = lax.Precision.HIGHEST

D_MODEL = 2048
DEPTH = 2
RET_HEADS = 8
RET_DK = 128
RET_CHUNK = 128
ROPE_BASE = 10000.0
LRU_W = 1024
LRU_BLOCKS = 8
LRU_BS = LRU_W // LRU_BLOCKS
LRU_C = 8.0
CONV_K = 4
GDN_HEADS = 8
GDN_DK = 128
GDN_CHUNK = 64
N_BRANCH = 3
N_GROUPS = 4
EXPERTS_PER_GROUP = 8
N_EXPERTS = N_GROUPS * EXPERTS_PER_GROUP
D_EXPERT = 512
EPS = 1e-6

HEAD_W = 1024
MAIN_COLS = 10 * HEAD_W
GATE_COLS = N_BRANCH * D_MODEL
AB_COL0 = MAIN_COLS + GATE_COLS
PROJ_TN = 1280
PROJ_COLS = 13 * PROJ_TN
LANES = 128
SUBLANES = 8
EXPERT_ROWS = 256
VMEM_LIMIT = 56 * 1024 * 1024

NT_DIMS = (((1,), (1,)), ((), ()))
TN_DIMS = (((0,), (0,)), ((), ()))


def _params(*sem):
    return pltpu.CompilerParams(dimension_semantics=sem, vmem_limit_bytes=VMEM_LIMIT)


def _dot32(a, b):
    return jnp.dot(a, b, preferred_element_type=F32, precision=HIGHEST)


def _dotg32(a, b, dims):
    return lax.dot_general(a, b, dims, preferred_element_type=F32, precision=HIGHEST)


def _silu(x):
    return x * jax.nn.sigmoid(x)


def _softplus(x):
    return jnp.maximum(x, 0.0) + jnp.log1p(jnp.exp(-jnp.abs(x)))


def _rms(x):
    return x * lax.rsqrt(jnp.mean(x * x, axis=-1, keepdims=True) + EPS)


def _shift_rows(x, s, fill):
    n = x.shape[0]
    if s % SUBLANES == 0:
        return jnp.concatenate([jnp.full((s, x.shape[1]), fill, x.dtype), x[:n - s]], axis=0)
    row = lax.broadcasted_iota(jnp.int32, x.shape, 0)
    return jnp.where(row < s, fill, pltpu.roll(x, s, 0))


def _causal_conv(x, tail, w):
    n = x.shape[0]
    row = lax.broadcasted_iota(jnp.int32, (SUBLANES, x.shape[1]), 0)
    y = x * w[CONV_K - 1:CONV_K]
    for s in range(1, CONV_K):
        xs = pltpu.roll(x, s, 0)
        head = jnp.where(row < s, pltpu.roll(tail, s, 0), xs[:SUBLANES])
        xs = jnp.concatenate([head, xs[SUBLANES:]], axis=0) if n > SUBLANES else head
        y = y + xs * w[CONV_K - 1 - s:CONV_K - s]
    return y


def _mod_kernel(c_ref, w_ref, b_ref, o_ref):
    c = c_ref[...]
    o_ref[...] = _dot32(_silu(c), w_ref[...]) + b_ref[...]


def ada_mod(c, ada_w, ada_b):
    L, D, N = ada_w.shape
    tn = 1536
    c8 = jnp.broadcast_to(c.astype(F32), (SUBLANES, D))
    out = pl.pallas_call(
        _mod_kernel,
        grid=(L, N // tn),
        in_specs=[pl.BlockSpec((SUBLANES, D), lambda l, j: (0, 0)),
                  pl.BlockSpec((None, D, tn), lambda l, j: (l, 0, j)),
                  pl.BlockSpec((None, 1, tn), lambda l, j: (l, 0, j))],
        out_specs=pl.BlockSpec((None, SUBLANES, tn), lambda l, j: (l, 0, j)),
        out_shape=jax.ShapeDtypeStruct((L, SUBLANES, N), F32),
        compiler_params=_params("parallel", "parallel"),
        name="ada_mod",
    )(c8, ada_w, ada_b.reshape(L, 1, N))
    return out[:, 0, :]


def _norm_proj_kernel(x_ref, g_ref, sc_ref, sh_ref, w_ref, o_ref, h_ref):
    @pl.when(pl.program_id(1) == 0)
    def _():
        h = (_rms(x_ref[...]) * g_ref[...]) * (1.0 + sc_ref[...]) + sh_ref[...]
        h_ref[...] = h.astype(BF16)

    o_ref[...] = jnp.dot(h_ref[...], w_ref[...], preferred_element_type=F32)


def norm_proj(x, g, sc, sh, w):
    S, D = x.shape
    N = w.shape[1]
    tm = min(1024, S)
    vec = pl.BlockSpec((1, D), lambda i, j: (0, 0))
    return pl.pallas_call(
        _norm_proj_kernel,
        grid=(S // tm, N // PROJ_TN),
        in_specs=[pl.BlockSpec((tm, D), lambda i, j: (i, 0)), vec, vec, vec,
                  pl.BlockSpec((D, PROJ_TN), lambda i, j: (0, j))],
        out_specs=pl.BlockSpec((tm, PROJ_TN), lambda i, j: (i, j)),
        out_shape=jax.ShapeDtypeStruct((S, N), F32),
        scratch_shapes=[pltpu.VMEM((tm, D), BF16)],
        compiler_params=_params("parallel", "arbitrary"),
        name="norm_proj",
    )(x, g, sc, sh, w)


def _ret_tables():
    H, C = RET_HEADS, RET_CHUNK
    log_g = np.log1p(-np.exp2(-5.0 - np.arange(H, dtype=np.float64)))
    pos = np.arange(C, dtype=np.float64)
    rel = pos[:, None] - pos[None, :]
    intra = np.where(rel >= 0, np.exp(log_g[:, None, None] * np.maximum(rel, 0.0)), 0.0)
    qd = np.exp(log_g[:, None] * (pos + 1.0))
    kd = np.exp(log_g[:, None] * (C - 1.0 - pos))
    ones = np.ones((1, 1, LANES))
    cd = [float(np.exp(lg * C)) for lg in log_g]
    return (jnp.asarray(intra, F32), jnp.asarray(qd[:, :, None] * ones, F32),
            jnp.asarray(kd[:, :, None] * ones, F32), cd)


def _ret_kernel(cd, q_ref, k_ref, v_ref, g_ref, cos_ref, sin_ref, intra_ref, qd_ref, kd_ref,
                o_ref, state_ref):
    @pl.when(pl.program_id(0) == 0)
    def _():
        state_ref[...] = jnp.zeros_like(state_ref)

    cos2 = cos_ref[...]
    sin2 = sin_ref[...]
    half = RET_DK // 2
    for h in range(RET_HEADS):
        sl = slice(h * RET_DK, (h + 1) * RET_DK)
        q = q_ref[:, sl]
        k = k_ref[:, sl]
        v = v_ref[:, sl]
        q = q * cos2 + pltpu.roll(q, half, 1) * sin2
        k = (k * cos2 + pltpu.roll(k, half, 1) * sin2) * (RET_DK ** -0.5)
        scores = _dotg32(q, k, NT_DIMS) * intra_ref[h]
        state = state_ref[h]
        o = _dot32(scores, v) + _dot32(q * qd_ref[h], state)
        state_ref[h] = state * cd[h] + _dotg32(k * kd_ref[h], v, TN_DIMS)
        o_ref[:, sl] = (_rms(o) * _silu(g_ref[:, sl])).astype(o_ref.dtype)


def retention(z, cos2, sin2):
    S = z.shape[0]
    C = RET_CHUNK
    intra, qd, kd, cd = _ret_tables()
    col = lambda b: pl.BlockSpec((C, HEAD_W), lambda n, b=b: (n, b))
    tab = lambda a: pl.BlockSpec(a.shape, lambda n: (0, 0, 0))
    rot = pl.BlockSpec((C, RET_DK), lambda n: (n, 0))
    return pl.pallas_call(
        functools.partial(_ret_kernel, cd),
        grid=(S // C,),
        in_specs=[col(0), col(1), col(2), col(3), rot, rot, tab(intra), tab(qd), tab(kd)],
        out_specs=pl.BlockSpec((C, HEAD_W), lambda n: (n, 0)),
        out_shape=jax.ShapeDtypeStruct((S, HEAD_W), BF16),
        scratch_shapes=[pltpu.VMEM((RET_HEADS, RET_DK, RET_DK), F32)],
        compiler_params=_params("arbitrary"),
        name="retention",
    )(z, z, z, z, cos2, sin2, intra, qd, kd)


def _lru_kernel(x_ref, gate_ref, cw_ref, cb_ref, wa_ref, ba_ref, wx_ref, bx_ref, lam_ref,
                o_ref, tail_ref, carry_ref):
    @pl.when(pl.program_id(0) == 0)
    def _():
        tail_ref[...] = jnp.zeros_like(tail_ref)
        carry_ref[...] = jnp.zeros_like(carry_ref)

    tm = x_ref.shape[0]
    for j in range(LRU_BLOCKS):
        sl = slice(j * LRU_BS, (j + 1) * LRU_BS)
        x = x_ref[:, sl]
        xc = _causal_conv(x, tail_ref[:, sl], cw_ref[:, sl]) + cb_ref[:, sl]
        tail_ref[:, sl] = x[tm - SUBLANES:]
        r = jax.nn.sigmoid(_dot32(xc, wa_ref[j]) + ba_ref[:, sl])
        i = jax.nn.sigmoid(_dot32(xc, wx_ref[j]) + bx_ref[:, sl])
        log_a = (-LRU_C * r) * _softplus(-lam_ref[:, sl])
        a = jnp.exp(log_a)
        t = jnp.tanh(log_a)
        b = jnp.sqrt(-2.0 * t / (1.0 - t)) * (i * xc)
        s = 1
        while s < tm:
            b = b + a * _shift_rows(b, s, 0.0)
            a = a * _shift_rows(a, s, 1.0)
            s *= 2
        h = b + a * carry_ref[0:1, sl]
        carry_ref[0:1, sl] = h[tm - 1:tm]
        g = gate_ref[:, sl]
        gelu = g * (0.5 * (1.0 + jnp.tanh(math.sqrt(2.0 / math.pi) * (g + 0.044715 * (g * g * g)))))
        o_ref[:, sl] = (h * gelu).astype(o_ref.dtype)


def rglru(z, conv_w, conv_b, wa, ba, wx, bx, lam):
    S = z.shape[0]
    tm = min(256, S)
    row = lambda a: a.reshape(1, LRU_W)
    vec = pl.BlockSpec((1, LRU_W), lambda n: (0, 0))
    wspec = pl.BlockSpec((LRU_BLOCKS, LRU_BS, LRU_BS), lambda n: (0, 0, 0))
    return pl.pallas_call(
        _lru_kernel,
        grid=(S // tm,),
        in_specs=[pl.BlockSpec((tm, HEAD_W), lambda n: (n, 4)),
                  pl.BlockSpec((tm, HEAD_W), lambda n: (n, 5)),
                  pl.BlockSpec((CONV_K, LRU_W), lambda n: (0, 0)), vec, wspec, vec, wspec, vec, vec],
        out_specs=pl.BlockSpec((tm, HEAD_W), lambda n: (n, 0)),
        out_shape=jax.ShapeDtypeStruct((S, HEAD_W), BF16),
        scratch_shapes=[pltpu.VMEM((SUBLANES, LRU_W), F32), pltpu.VMEM((SUBLANES, LRU_W), F32)],
        compiler_params=_params("arbitrary"),
        name="rglru",
    )(z, z, conv_w, row(conv_b), wa, row(ba), wx, row(bx), row(lam))


def _gdn_tables():
    C = GDN_CHUNK
    i = np.arange(C)[:, None]
    j = np.arange(C)[None, :]
    levels = []
    s = 1
    while s < C:
        levels.append((i // (2 * s) == j // (2 * s)) & (i % (2 * s) >= s) & (j % (2 * s) < s))
        s *= 2
    return jnp.asarray(np.stack(levels), F32), jnp.asarray(i >= j, F32), jnp.asarray(i <= j, F32)


def _gdn_kernel(q_ref, k_ref, v_ref, z_ref, ab_ref, cw_ref, alog_ref, dtb_ref, ng_ref, lvl_ref, tri_ref,
                trit_ref, o_ref, tail_ref, state_ref):
    @pl.when(pl.program_id(0) == 0)
    def _():
        tail_ref[...] = jnp.zeros_like(tail_ref)
        state_ref[...] = jnp.zeros_like(state_ref)

    C = GDN_CHUNK
    H = GDN_HEADS
    incl = tri_ref[...]
    rowi = lax.broadcasted_iota(jnp.int32, (C, C), 0)
    coli = lax.broadcasted_iota(jnp.int32, (C, C), 1)
    eye = (rowi == coli).astype(F32)

    ab = ab_ref[...]
    g_all = -jnp.exp(alog_ref[...]) * _softplus(ab + dtb_ref[...])
    beta_all = jax.nn.sigmoid(ab)
    gc_all = _dot32(incl, g_all)
    gc_rows = _dotg32(g_all, trit_ref[...], TN_DIMS)

    conv = []
    for p, ref in enumerate((q_ref, k_ref, v_ref)):
        x = ref[...]
        conv.append(_silu(_causal_conv(x, tail_ref[p], cw_ref[p])))
        tail_ref[p] = x[C - SUBLANES:]
    qc, kc, vc = conv

    for h in range(H):
        sl = slice(h * GDN_DK, (h + 1) * GDN_DK)
        q = qc[:, sl]
        k = kc[:, sl]
        v = vc[:, sl]
        q = q * lax.rsqrt(jnp.sum(q * q, axis=-1, keepdims=True) + EPS) * (GDN_DK ** -0.5)
        k = k * lax.rsqrt(jnp.sum(k * k, axis=-1, keepdims=True) + EPS)
        beta = beta_all[:, H + h:H + h + 1]
        gc = gc_all[:, h:h + 1]
        gcr = gc_rows[h:h + 1, :]
        decay = incl * jnp.exp(jnp.where(incl > 0, gc - gcr, 0.0))
        kb = k * beta
        vb = v * beta
        egc = jnp.exp(gc)
        m = _dotg32(kb, k, NT_DIMS) * (decay - eye)
        inv = eye - m * lvl_ref[0]
        for lv in range(1, lvl_ref.shape[0]):
            inv = inv - _dot32(inv, _dot32(m * lvl_ref[lv], inv))
        u = _dot32(inv, vb)
        w = _dot32(inv, kb * egc)
        attn = _dotg32(q, k, NT_DIMS) * decay
        gl = gc[C - 1:C]
        state = state_ref[h]
        v_new = u - _dot32(w, state)
        o = _dot32(q * egc, state) + _dot32(attn, v_new)
        state_ref[h] = state * jnp.exp(gl) + _dotg32(k * jnp.exp(gl - gc), v_new, TN_DIMS)
        o_ref[:, sl] = (_rms(o) * ng_ref[...] * _silu(z_ref[:, sl])).astype(o_ref.dtype)


def gdn(z, conv_w, a_log, dt_bias, norm_g):
    S = z.shape[0]
    C = GDN_CHUNK
    H = GDN_HEADS
    lvl, tri, trit = _gdn_tables()
    alog = jnp.zeros((1, LANES), F32).at[0, :H].set(a_log)
    dtb = jnp.zeros((1, LANES), F32).at[0, :H].set(dt_bias)
    cw = conv_w.reshape(CONV_K, 3, HEAD_W).transpose(1, 0, 2)
    col = lambda b: pl.BlockSpec((C, HEAD_W), lambda n, b=b: (n, b))
    lane = pl.BlockSpec((1, LANES), lambda n: (0, 0))
    return pl.pallas_call(
        _gdn_kernel,
        grid=(S // C,),
        in_specs=[col(6), col(7), col(8), col(9),
                  pl.BlockSpec((C, LANES), lambda n: (n, AB_COL0 // LANES)),
                  pl.BlockSpec((3, CONV_K, HEAD_W), lambda n: (0, 0, 0)),
                  lane, lane, lane,
                  pl.BlockSpec(lvl.shape, lambda n: (0, 0, 0)),
                  pl.BlockSpec(tri.shape, lambda n: (0, 0)),
                  pl.BlockSpec(trit.shape, lambda n: (0, 0))],
        out_specs=pl.BlockSpec((C, HEAD_W), lambda n: (n, 0)),
        out_shape=jax.ShapeDtypeStruct((S, HEAD_W), BF16),
        scratch_shapes=[pltpu.VMEM((3, SUBLANES, HEAD_W), F32),
                        pltpu.VMEM((H, GDN_DK, GDN_DK), F32)],
        compiler_params=_params("arbitrary"),
        name="gdn",
    )(z, z, z, z, z, cw, alog, dtb, norm_g.reshape(1, GDN_DK), lvl, tri, trit)


ROUTE_LANES = LANES


def _merge_kernel(yr_ref, yl_ref, yg_ref, g0_ref, g1_ref, g2_ref, x_ref, gt_ref, bp_ref, wo_ref,
                  ng_ref, sc_ref, sh_ref, wr_ref, br_ref, xo_ref, h_ref, lg_ref):
    merged = jax.nn.sigmoid(g0_ref[...]) * jnp.dot(yr_ref[...], bp_ref[0], preferred_element_type=F32)
    merged += jax.nn.sigmoid(g1_ref[...]) * jnp.dot(yl_ref[...], bp_ref[1], preferred_element_type=F32)
    merged += jax.nn.sigmoid(g2_ref[...]) * jnp.dot(yg_ref[...], bp_ref[2], preferred_element_type=F32)
    out = jnp.dot(merged.astype(BF16), wo_ref[...], preferred_element_type=F32)
    x = x_ref[...] + gt_ref[...] * out
    xo_ref[...] = x
    h = (_rms(x) * ng_ref[...]) * (1.0 + sc_ref[...]) + sh_ref[...]
    h_ref[...] = h
    lg_ref[...] = _dot32(h, wr_ref[...]) + br_ref[...]


def merge(y_ret, y_lru, y_gdn, z, x, gt1, bp, w_out, norm2_g, sc2, sh2, w_route, b_route):
    S, D = x.shape
    tm = min(256, S)
    ycol = pl.BlockSpec((tm, HEAD_W), lambda i: (i, 0))
    gate = lambda b: pl.BlockSpec((tm, D), lambda i, b=b: (i, MAIN_COLS // D + b))
    rows = pl.BlockSpec((tm, D), lambda i: (i, 0))
    vec = pl.BlockSpec((1, D), lambda i: (0, 0))
    once = dict(pipeline_mode=pl.Buffered(1))
    return pl.pallas_call(
        _merge_kernel,
        grid=(S // tm,),
        in_specs=[ycol, ycol, ycol, gate(0), gate(1), gate(2), rows, vec,
                  pl.BlockSpec((N_BRANCH, HEAD_W, D), lambda i: (0, 0, 0), **once),
                  pl.BlockSpec((D, D), lambda i: (0, 0), **once),
                  vec, vec, vec,
                  pl.BlockSpec((D, ROUTE_LANES), lambda i: (0, 0)),
                  pl.BlockSpec((1, ROUTE_LANES), lambda i: (0, 0))],
        out_specs=[rows, rows, pl.BlockSpec((tm, ROUTE_LANES), lambda i: (i, 0))],
        out_shape=[jax.ShapeDtypeStruct((S, D), F32), jax.ShapeDtypeStruct((S, D), F32),
                   jax.ShapeDtypeStruct((S, ROUTE_LANES), F32)],
        compiler_params=_params("parallel"),
        name="merge",
    )(y_ret, y_lru, y_gdn, z, z, z, x, gt1, bp, w_out, norm2_g, sc2, sh2, w_route, b_route)


R_E0, R_E1, R_W0, R_W1, R_RANK0, R_RANK1 = range(6)


def _first_max(vals, lane):
    m = jnp.max(vals, axis=-1, keepdims=True)
    idx = jnp.min(jnp.where(vals == m, lane, ROUTE_LANES), axis=-1, keepdims=True)
    return m, idx


def _route_kernel(lg_ref, o_ref, cnt_ref, carry_ref):
    @pl.when(pl.program_id(0) == 0)
    def _():
        carry_ref[...] = jnp.zeros_like(carry_ref)

    lg = lg_ref[...]
    tm = lg.shape[0]
    lane = lax.broadcasted_iota(jnp.int32, lg.shape, 1)
    neg = jnp.float32(-jnp.inf)
    grp = jnp.where(lane < N_GROUPS, lg, neg)
    gmax, gsel = _first_max(grp, lane)
    pg_top = 1.0 / jnp.sum(jnp.exp(grp - gmax), axis=-1, keepdims=True)
    lo = N_GROUPS + gsel * EXPERTS_PER_GROUP
    inside = jnp.where((lane >= lo) & (lane < lo + EXPERTS_PER_GROUP), lg, neg)
    v0, i0 = _first_max(inside, lane)
    v1, i1 = _first_max(jnp.where(lane == i0, neg, inside), lane)
    e1w = jnp.exp(v1 - v0)
    w0 = pg_top / (1.0 + e1w)
    w1 = pg_top * e1w / (1.0 + e1w)
    hit0 = lane == i0
    hit1 = lane == i1
    hits = (hit0 | hit1).astype(BF16)
    row = lax.broadcasted_iota(jnp.int32, (tm, tm), 0)
    col = lax.broadcasted_iota(jnp.int32, (tm, tm), 1)
    before = jnp.dot((row > col).astype(BF16), hits, preferred_element_type=F32) + carry_ref[0:1, :]
    rank0 = jnp.sum(jnp.where(hit0, before, 0.0), axis=-1, keepdims=True)
    rank1 = jnp.sum(jnp.where(hit1, before, 0.0), axis=-1, keepdims=True)
    total = carry_ref[0:1, :] + jnp.sum(hits.astype(F32), axis=0, keepdims=True)
    carry_ref[0:1, :] = total
    cnt_ref[...] = jnp.broadcast_to(total, cnt_ref.shape)
    rec = jnp.zeros(lg.shape, F32)
    for slot, val in ((R_E0, (i0 - N_GROUPS).astype(F32)), (R_E1, (i1 - N_GROUPS).astype(F32)),
                      (R_W0, w0), (R_W1, w1), (R_RANK0, rank0), (R_RANK1, rank1)):
        rec = jnp.where(lane == slot, val, rec)
    o_ref[...] = rec


def route(logits):
    S = logits.shape[0]
    tm = min(256, S)
    return pl.pallas_call(
        _route_kernel,
        grid=(S // tm,),
        in_specs=[pl.BlockSpec((tm, ROUTE_LANES), lambda i: (i, 0))],
        out_specs=[pl.BlockSpec((tm, ROUTE_LANES), lambda i: (i, 0)),
                   pl.BlockSpec((SUBLANES, ROUTE_LANES), lambda i: (0, 0))],
        out_shape=[jax.ShapeDtypeStruct((S, ROUTE_LANES), F32),
                   jax.ShapeDtypeStruct((SUBLANES, ROUTE_LANES), F32)],
        scratch_shapes=[pltpu.VMEM((SUBLANES, ROUTE_LANES), F32)],
        compiler_params=_params("arbitrary"),
        name="route",
    )(logits)


def _expert_kernel(be_ref, nused_ref, nvalid_ref, src_ref, nsrc_ref, dst_ref, h_hbm, w1_ref, w3_ref, w2_ref,
                   y_hbm, xbuf, ybuf, gsem, ssem):
    i = pl.program_id(0)
    nused = nused_ref[0]
    B = EXPERT_ROWS
    slot = i % 2

    def gather(idx_ref, s):
        def body(r, c):
            pltpu.make_async_copy(h_hbm.at[pl.ds(idx_ref[0, r], 1)], xbuf.at[s, pl.ds(r, 1)],
                                  gsem.at[s]).start()
            return c
        lax.fori_loop(0, B, body, 0)

    def scatter_wait(s, n):
        n8 = pl.multiple_of((n // SUBLANES) * SUBLANES, SUBLANES)

        @pl.when(n8 > 0)
        def _():
            pltpu.make_async_copy(ybuf.at[s, pl.ds(0, n8)], y_hbm.at[pl.ds(0, n8)], ssem.at[s]).wait()

        def body(r, c):
            pltpu.make_async_copy(ybuf.at[s, pl.ds(0, 1)], y_hbm.at[pl.ds(0, 1)], ssem.at[s]).wait()
            return c
        lax.fori_loop(0, n - n8, body, 0)

    @pl.when(i == 0)
    def _():
        gather(src_ref, 0)

    @pl.when(i < nused)
    def _():
        @pl.when(i + 1 < nused)
        def _():
            gather(nsrc_ref, 1 - slot)

        pltpu.make_async_copy(h_hbm.at[pl.ds(0, B)], xbuf.at[slot], gsem.at[slot]).wait()
        x = xbuf[slot].astype(BF16)
        a1 = jnp.dot(x, w1_ref[...], preferred_element_type=F32)
        a3 = jnp.dot(x, w3_ref[...], preferred_element_type=F32)
        mid = (_silu(a1) * a3).astype(BF16)
        ybuf[slot] = jnp.dot(mid, w2_ref[...], preferred_element_type=F32)

        @pl.when(i >= 1)
        def _():
            scatter_wait(1 - slot, nvalid_ref[jnp.maximum(i - 1, 0)])

        def body(r, c):
            pltpu.make_async_copy(ybuf.at[slot, pl.ds(r, 1)], y_hbm.at[pl.ds(dst_ref[0, r], 1)],
                                  ssem.at[slot]).start()
            return c
        lax.fori_loop(0, nvalid_ref[i], body, 0)

        @pl.when(i == nused - 1)
        def _():
            scatter_wait(slot, nvalid_ref[i])


def experts(h, block_e, nused, nvalid, src_row, dst_row, w1, w3, w2):
    T, D = h.shape
    nb = block_e.shape[0]
    B = EXPERT_ROWS
    idx = lambda f: pl.BlockSpec((None, 1, B), f, memory_space=pltpu.SMEM)
    wspec = lambda shp: pl.BlockSpec((None,) + shp, lambda i, be, nu, nv: (be[i], 0, 0))
    grid_spec = pltpu.PrefetchScalarGridSpec(
        num_scalar_prefetch=3,
        grid=(nb,),
        in_specs=[idx(lambda i, be, nu, nv: (i, 0, 0)),
                  idx(lambda i, be, nu, nv: (jnp.minimum(i + 1, nb - 1), 0, 0)),
                  idx(lambda i, be, nu, nv: (i, 0, 0)),
                  pl.BlockSpec(memory_space=pl.ANY),
                  wspec((D, D_EXPERT)), wspec((D, D_EXPERT)), wspec((D_EXPERT, D))],
        out_specs=pl.BlockSpec(memory_space=pl.ANY),
        scratch_shapes=[pltpu.VMEM((2, B, D), F32), pltpu.VMEM((2, B, D), F32),
                        pltpu.SemaphoreType.DMA((2,)), pltpu.SemaphoreType.DMA((2,))])
    src3 = src_row.reshape(nb, 1, B)
    return pl.pallas_call(
        _expert_kernel,
        grid_spec=grid_spec,
        out_shape=jax.ShapeDtypeStruct((2 * T, D), F32),
        compiler_params=_params("arbitrary"),
        name="experts",
    )(block_e, nused, nvalid, src3, src3, dst_row.reshape(nb, 1, B), h, w1, w3, w2)


def dispatch_plan(rec, counts, T):
    B = EXPERT_ROWS
    A = 2 * T
    nb = A // B + N_EXPERTS
    cnt = counts[0, N_GROUPS:N_GROUPS + N_EXPERTS].astype(jnp.int32)
    padded = ((cnt + B - 1) // B) * B
    pend = jnp.cumsum(padded)
    pstart = pend - padded
    e = rec[:, R_E0:R_E1 + 1].astype(jnp.int32)
    rank = rec[:, R_RANK0:R_RANK1 + 1].astype(jnp.int32)
    dest = (pstart[e] + rank).reshape(-1)
    tok = jnp.broadcast_to(jnp.arange(T, dtype=jnp.int32)[:, None], (T, 2))
    plane = jnp.broadcast_to(jnp.arange(2, dtype=jnp.int32)[None, :] * T, (T, 2))
    src_row = jnp.zeros((nb * B,), jnp.int32).at[dest].set(tok.reshape(-1))
    dst_row = jnp.zeros((nb * B,), jnp.int32).at[dest].set((tok + plane).reshape(-1))
    first = jnp.arange(nb, dtype=jnp.int32) * B
    block_e = jnp.clip(jnp.searchsorted(pend, first, side='right'), 0, N_EXPERTS - 1).astype(jnp.int32)
    nvalid = jnp.clip(cnt[block_e] - (first - pstart[block_e]), 0, B).astype(jnp.int32)
    nused = (pend[-1] // B).astype(jnp.int32).reshape(1)
    return block_e, nused, nvalid, src_row, dst_row


def _combine_kernel(final, x_ref, y0_ref, y1_ref, rec_ref, gt_ref, fg_ref, o_ref):
    rec = rec_ref[...]
    moe = rec[:, R_W0:R_W0 + 1] * y0_ref[...] + rec[:, R_W1:R_W1 + 1] * y1_ref[...]
    x = x_ref[...] + gt_ref[...] * moe
    o_ref[...] = _rms(x) * fg_ref[...] if final else x


def combine(x, y, rec, gt2, final_g, final):
    S, D = x.shape
    tm = min(512, S)
    y3 = y.reshape(2, y.shape[0] // 2, D)
    rows = pl.BlockSpec((tm, D), lambda i: (i, 0))
    vec = pl.BlockSpec((1, D), lambda i: (0, 0))
    plane = lambda k: pl.BlockSpec((None, tm, D), lambda i, k=k: (k, i, 0))
    return pl.pallas_call(
        functools.partial(_combine_kernel, final),
        grid=(S // tm,),
        in_specs=[rows, plane(0), plane(1), pl.BlockSpec((tm, ROUTE_LANES), lambda i: (i, 0)), vec, vec],
        out_specs=rows,
        out_shape=jax.ShapeDtypeStruct((S, D), F32),
        compiler_params=_params("parallel"),
        name="combine",
    )(x, y3, y3, rec, gt2, final_g)


def _rope_tables(positions):
    half = RET_DK // 2
    inv_freq = ROPE_BASE ** (-jnp.arange(half, dtype=F32) * (2.0 / RET_DK))
    ang = positions.astype(F32)[:, None] * inv_freq
    cos = jnp.cos(ang)
    sin = jnp.sin(ang)
    return jnp.concatenate([cos, cos], axis=-1), jnp.concatenate([-sin, sin], axis=-1)


def _proj_weight(w_in):
    D = w_in.shape[0]
    n_ab = 2 * GDN_HEADS
    main = w_in[:, :MAIN_COLS]
    ab = w_in[:, MAIN_COLS:MAIN_COLS + n_ab]
    gates = w_in[:, MAIN_COLS + n_ab:]
    pad = jnp.zeros((D, PROJ_COLS - AB_COL0 - n_ab), w_in.dtype)
    return jnp.concatenate([main, gates, ab, pad], axis=1).astype(BF16)


def kernel(x, c, positions, ada_w, ada_b, norm1_g, norm2_g, w_in, lru_conv_w, lru_conv_b, lru_wa, lru_ba,
           lru_wx, lru_bx, lru_lambda, gdn_conv_w, gdn_a_log, gdn_dt_bias, gdn_norm_g, branch_proj, w_out,
           router_grp_w, router_grp_b, router_exp_w, router_exp_b, moe_w1, moe_w3, moe_w2, final_g):
    B, S, D = x.shape
    assert B == 1
    xs = x.reshape(S, D)
    mod = ada_mod(c, ada_w, ada_b)
    cos2, sin2 = _rope_tables(positions.reshape(S))
    row = lambda v: v.reshape(1, -1).astype(F32)
    for l in range(DEPTH):
        sh1, sc1, gt1, sh2, sc2, gt2 = (row(m) for m in jnp.split(mod[l], 6))
        z = norm_proj(xs, row(norm1_g[l]), sc1, sh1, _proj_weight(w_in[l]))
        y_ret = retention(z, cos2, sin2)
        y_lru = rglru(z, lru_conv_w[l], lru_conv_b[l], lru_wa[l], lru_ba[l], lru_wx[l], lru_bx[l],
                      lru_lambda[l])
        y_gdn = gdn(z, gdn_conv_w[l], gdn_a_log[l], gdn_dt_bias[l], gdn_norm_g[l])
        n_r = N_GROUPS + N_EXPERTS
        w_route = jnp.concatenate([router_grp_w[l], router_exp_w[l],
                                   jnp.zeros((D, ROUTE_LANES - n_r), F32)], axis=1)
        b_route = jnp.concatenate([router_grp_b[l], router_exp_b[l],
                                   jnp.zeros((ROUTE_LANES - n_r,), F32)]).reshape(1, ROUTE_LANES)
        xs, h2, logits = merge(y_ret, y_lru, y_gdn, z, xs, gt1, branch_proj[l].astype(BF16),
                               w_out[l].astype(BF16), row(norm2_g[l]), sc2, sh2, w_route, b_route)
        rec, counts = route(logits)
        block_e, nused, nvalid, src_row, dst_row = dispatch_plan(rec, counts, S)
        y = experts(h2, block_e, nused, nvalid, src_row, dst_row, moe_w1[l].astype(BF16),
                    moe_w3[l].astype(BF16), moe_w2[l].astype(BF16))
        xs = combine(xs, y, rec, gt2, row(final_g), final=(l == DEPTH - 1))
    return xs.reshape(B, S, D)
```

```python
import functools
import math

import numpy as np
import jax
import jax.numpy as jnp
from jax import lax
from jax.experimental import pallas as pl
from jax.experimental.pallas import tpu as pltpu

F32 = jnp.float32
BF16 = jnp.bfloat16
HIGHEST = lax.Precision.HIGHEST

D_MODEL = 2048
DEPTH = 2
RET_HEADS = 8
RET_DK = 128
RET_CHUNK = 128
ROPE_BASE = 10000.0
LRU_W = 1024
LRU_BLOCKS = 8
LRU_BS = LRU_W // LRU_BLOCKS
LRU_C = 8.0
CONV_K = 4
GDN_HEADS = 8
GDN_DK = 128
GDN_CHUNK = 64
N_BRANCH = 3
N_GROUPS = 4
EXPERTS_PER_GROUP = 8
N_EXPERTS = N_GROUPS * EXPERTS_PER_GROUP
D_EXPERT = 512
EPS = 1e-6

HEAD_W = 1024
MAIN_COLS = 10 * HEAD_W
GATE_COLS = N_BRANCH * D_MODEL
AB_COL0 = MAIN_COLS + GATE_COLS
PROJ_TN = 1280
PROJ_COLS = 13 * PROJ_TN
LANES = 128
SUBLANES = 8
EXPERT_ROWS = 256
GDN_ROWS = 2 * GDN_CHUNK
RET_ROWS = 2 * RET_CHUNK
VMEM_LIMIT = 56 * 1024 * 1024

NT_DIMS = (((1,), (1,)), ((), ()))
TN_DIMS = (((0,), (0,)), ((), ()))


def _params(*sem):
    return pltpu.CompilerParams(dimension_semantics=sem, vmem_limit_bytes=VMEM_LIMIT)


def _dot32(a, b):
    return jnp.dot(a, b, preferred_element_type=F32, precision=HIGHEST)


def _dotg32(a, b, dims):
    return lax.dot_general(a, b, dims, preferred_element_type=F32, precision=HIGHEST)


def _dot(a, b):
    return jnp.dot(a.astype(BF16), b.astype(BF16), preferred_element_type=F32)


def _dotg(a, b, dims):
    return lax.dot_general(a.astype(BF16), b.astype(BF16), dims, preferred_element_type=F32)


def _dot3(a, b):
    ah = a.astype(BF16)
    bh = b.astype(BF16)
    al = (a - ah.astype(F32)).astype(BF16)
    bl = (b - bh.astype(F32)).astype(BF16)
    d = functools.partial(jnp.dot, preferred_element_type=F32)
    return d(ah, bh) + (d(ah, bl) + d(al, bh))


def _silu(x):
    return x * jax.nn.sigmoid(x)


def _softplus(x):
    return jnp.maximum(x, 0.0) + jnp.log1p(jnp.exp(-jnp.abs(x)))


def _rms(x):
    return x * lax.rsqrt(jnp.mean(x * x, axis=-1, keepdims=True) + EPS)


def _shift_rows(x, s, fill):
    n = x.shape[0]
    if s % SUBLANES == 0:
        return jnp.concatenate([jnp.full((s, x.shape[1]), fill, x.dtype), x[:n - s]], axis=0)
    row = lax.broadcasted_iota(jnp.int32, x.shape, 0)
    return jnp.where(row < s, fill, pltpu.roll(x, s, 0))


def _causal_conv(x, tail, w):
    n = x.shape[0]
    row = lax.broadcasted_iota(jnp.int32, (SUBLANES, x.shape[1]), 0)
    y = x * w[CONV_K - 1:CONV_K]
    for s in range(1, CONV_K):
        xs = pltpu.roll(x, s, 0)
        head = jnp.where(row < s, pltpu.roll(tail, s, 0), xs[:SUBLANES])
        xs = jnp.concatenate([head, xs[SUBLANES:]], axis=0) if n > SUBLANES else head
        y = y + xs * w[CONV_K - 1 - s:CONV_K - s]
    return y


def _mod_kernel(c_ref, w_ref, b_ref, o_ref):
    c = c_ref[...]
    o_ref[...] = _dot32(_silu(c), w_ref[...]) + b_ref[...]


def ada_mod(c, ada_w, ada_b):
    L, D, N = ada_w.shape
    tn = 1536
    c8 = jnp.broadcast_to(c.astype(F32), (SUBLANES, D))
    out = pl.pallas_call(
        _mod_kernel,
        grid=(L, N // tn),
        in_specs=[pl.BlockSpec((SUBLANES, D), lambda l, j: (0, 0)),
                  pl.BlockSpec((None, D, tn), lambda l, j: (l, 0, j)),
                  pl.BlockSpec((None, 1, tn), lambda l, j: (l, 0, j))],
        out_specs=pl.BlockSpec((None, SUBLANES, tn), lambda l, j: (l, 0, j)),
        out_shape=jax.ShapeDtypeStruct((L, SUBLANES, N), F32),
        compiler_params=_params("parallel", "parallel"),
        name="ada_mod",
    )(c8, ada_w, ada_b.reshape(L, 1, N))
    return out[:, 0, :]


def _norm_proj_kernel(x_ref, g_ref, sc_ref, sh_ref, w_ref, o_ref, h_ref):
    @pl.when(pl.program_id(1) == 0)
    def _():
        h = (_rms(x_ref[...]) * g_ref[...]) * (1.0 + sc_ref[...]) + sh_ref[...]
        h_ref[...] = h.astype(BF16)

    o_ref[...] = jnp.dot(h_ref[...], w_ref[...], preferred_element_type=F32)


def norm_proj(x, g, sc, sh, w):
    S, D = x.shape
    N = w.shape[1]
    tm = min(1024, S)
    vec = pl.BlockSpec((1, D), lambda i, j: (0, 0))
    return pl.pallas_call(
        _norm_proj_kernel,
        grid=(S // tm, N // PROJ_TN),
        in_specs=[pl.BlockSpec((tm, D), lambda i, j: (i, 0)), vec, vec, vec,
                  pl.BlockSpec((D, PROJ_TN), lambda i, j: (0, j))],
        out_specs=pl.BlockSpec((tm, PROJ_TN), lambda i, j: (i, j)),
        out_shape=jax.ShapeDtypeStruct((S, N), F32),
        scratch_shapes=[pltpu.VMEM((tm, D), BF16)],
        compiler_params=_params("parallel", "arbitrary"),
        name="norm_proj",
    )(x, g, sc, sh, w)


def _ret_tables():
    H, C = RET_HEADS, RET_CHUNK
    log_g = np.log1p(-np.exp2(-5.0 - np.arange(H, dtype=np.float64)))
    pos = np.arange(C, dtype=np.float64)
    rel = pos[:, None] - pos[None, :]
    intra = np.where(rel >= 0, np.exp(log_g[:, None, None] * np.maximum(rel, 0.0)), 0.0)
    qd = np.exp(log_g[:, None] * (pos + 1.0))
    kd = np.exp(log_g[:, None] * (C - 1.0 - pos))
    ones = np.ones((1, 1, LANES))
    cd = [float(np.exp(lg * C)) for lg in log_g]
    return (jnp.asarray(intra, F32), jnp.asarray(qd[:, :, None] * ones, F32),
            jnp.asarray(kd[:, :, None] * ones, F32), cd)


def _ret_kernel(cd, q_ref, k_ref, v_ref, g_ref, cos_ref, sin_ref, intra_ref, qd_ref, kd_ref,
                o_ref, state_ref):
    @pl.when(pl.program_id(0) == 0)
    def _():
        state_ref[...] = jnp.zeros_like(state_ref)

    C = RET_CHUNK
    half = RET_DK // 2
    for c in range(q_ref.shape[0] // C):
        rs = slice(c * C, (c + 1) * C)
        cos2 = cos_ref[rs]
        sin2 = sin_ref[rs]
        for h in range(RET_HEADS):
            sl = slice(h * RET_DK, (h + 1) * RET_DK)
            q = q_ref[rs, sl]
            k = k_ref[rs, sl]
            v = v_ref[rs, sl]
            q = q * cos2 + pltpu.roll(q, half, 1) * sin2
            k = (k * cos2 + pltpu.roll(k, half, 1) * sin2) * (RET_DK ** -0.5)
            scores = _dotg(q, k, NT_DIMS) * intra_ref[h]
            state = state_ref[h]
            o = _dot(scores, v) + _dot(q * qd_ref[h], state)
            state_ref[h] = state * cd[h] + _dotg(k * kd_ref[h], v, TN_DIMS)
            o_ref[rs, sl] = (_rms(o) * _silu(g_ref[rs, sl])).astype(o_ref.dtype)


def retention(z, cos2, sin2):
    S = z.shape[0]
    C = min(RET_ROWS, S)
    intra, qd, kd, cd = _ret_tables()
    col = lambda b: pl.BlockSpec((C, HEAD_W), lambda n, b=b: (n, b))
    tab = lambda a: pl.BlockSpec(a.shape, lambda n: (0, 0, 0))
    rot = pl.BlockSpec((C, RET_DK), lambda n: (n, 0))
    return pl.pallas_call(
        functools.partial(_ret_kernel, cd),
        grid=(S // C,),
        in_specs=[col(0), col(1), col(2), col(3), rot, rot, tab(intra), tab(qd), tab(kd)],
        out_specs=pl.BlockSpec((C, HEAD_W), lambda n: (n, 0)),
        out_shape=jax.ShapeDtypeStruct((S, HEAD_W), BF16),
        scratch_shapes=[pltpu.VMEM((RET_HEADS, RET_DK, RET_DK), F32)],
        compiler_params=_params("arbitrary"),
        name="retention",
    )(z, z, z, z, cos2, sin2, intra, qd, kd)


def _lru_kernel(x_ref, gate_ref, cw_ref, cb_ref, wa_ref, ba_ref, wx_ref, bx_ref, lam_ref,
                o_ref, tail_ref, carry_ref):
    @pl.when(pl.program_id(0) == 0)
    def _():
        tail_ref[...] = jnp.zeros_like(tail_ref)
        carry_ref[...] = jnp.zeros_like(carry_ref)

    tm = x_ref.shape[0]
    for j in range(LRU_BLOCKS):
        sl = slice(j * LRU_BS, (j + 1) * LRU_BS)
        x = x_ref[:, sl]
        xc = _causal_conv(x, tail_ref[:, sl], cw_ref[:, sl]) + cb_ref[:, sl]
        tail_ref[:, sl] = x[tm - SUBLANES:]
        r = jax.nn.sigmoid(_dot(xc, wa_ref[j]) + ba_ref[:, sl])
        i = jax.nn.sigmoid(_dot(xc, wx_ref[j]) + bx_ref[:, sl])
        log_a = (-LRU_C * r) * _softplus(-lam_ref[:, sl])
        a = jnp.exp(log_a)
        t = jnp.tanh(log_a)
        b = jnp.sqrt(-2.0 * t / (1.0 - t)) * (i * xc)
        s = 1
        while s < tm:
            b = b + a * _shift_rows(b, s, 0.0)
            a = a * _shift_rows(a, s, 1.0)
            s *= 2
        h = b + a * carry_ref[0:1, sl]
        carry_ref[0:1, sl] = h[tm - 1:tm]
        g = gate_ref[:, sl]
        gelu = g * (0.5 * (1.0 + jnp.tanh(math.sqrt(2.0 / math.pi) * (g + 0.044715 * (g * g * g)))))
        o_ref[:, sl] = (h * gelu).astype(o_ref.dtype)


def rglru(z, conv_w, conv_b, wa, ba, wx, bx, lam):
    S = z.shape[0]
    tm = min(256, S)
    row = lambda a: a.reshape(1, LRU_W)
    vec = pl.BlockSpec((1, LRU_W), lambda n: (0, 0))
    wspec = pl.BlockSpec((LRU_BLOCKS, LRU_BS, LRU_BS), lambda n: (0, 0, 0))
    return pl.pallas_call(
        _lru_kernel,
        grid=(S // tm,),
        in_specs=[pl.BlockSpec((tm, HEAD_W), lambda n: (n, 4)),
                  pl.BlockSpec((tm, HEAD_W), lambda n: (n, 5)),
                  pl.BlockSpec((CONV_K, LRU_W), lambda n: (0, 0)), vec, wspec, vec, wspec, vec, vec],
        out_specs=pl.BlockSpec((tm, HEAD_W), lambda n: (n, 0)),
        out_shape=jax.ShapeDtypeStruct((S, HEAD_W), BF16),
        scratch_shapes=[pltpu.VMEM((SUBLANES, LRU_W), F32), pltpu.VMEM((SUBLANES, LRU_W), F32)],
        compiler_params=_params("arbitrary"),
        name="rglru",
    )(z, z, conv_w, row(conv_b), wa, row(ba), wx, row(bx), row(lam))


def _gdn_tables():
    C = GDN_CHUNK
    i = np.arange(C)[:, None]
    j = np.arange(C)[None, :]
    levels = []
    s = 1
    while s < C:
        levels.append((i // (2 * s) == j // (2 * s)) & (i % (2 * s) >= s) & (j % (2 * s) < s))
        s *= 2
    return jnp.asarray(np.stack(levels), F32), jnp.asarray(i >= j, F32), jnp.asarray(i <= j, F32)


def _gdn_kernel(q_ref, k_ref, v_ref, z_ref, ab_ref, cw_ref, alog_ref, dtb_ref, ng_ref, lvl_ref, tri_ref,
                trit_ref, o_ref, tail_ref, state_ref):
    @pl.when(pl.program_id(0) == 0)
    def _():
        tail_ref[...] = jnp.zeros_like(tail_ref)
        state_ref[...] = jnp.zeros_like(state_ref)

    C = GDN_CHUNK
    H = GDN_HEADS
    DK = GDN_DK
    rows = q_ref.shape[0]
    incl = tri_ref[...]
    rowi = lax.broadcasted_iota(jnp.int32, (C, C), 0)
    coli = lax.broadcasted_iota(jnp.int32, (C, C), 1)
    eye = (rowi == coli).astype(F32)

    ab = ab_ref[...]
    g_all = -jnp.exp(alog_ref[...]) * _softplus(ab + dtb_ref[...])
    beta_all = jax.nn.sigmoid(ab)

    conv = []
    for p, ref in enumerate((q_ref, k_ref, v_ref)):
        x = ref[...]
        conv.append(_silu(_causal_conv(x, tail_ref[p], cw_ref[p])))
        tail_ref[p] = x[rows - SUBLANES:]
    qc, kc, vc = conv

    pairs = [(c, h) for c in range(rows // C) for h in range(H)]
    st = {}
    for c in range(rows // C):
        g_c = g_all[c * C:(c + 1) * C]
        st[c] = (_dot32(incl, g_c),
                 _dotg32(g_c, trit_ref[...], TN_DIMS))
    for c, h in pairs:
        rs = slice(c * C, (c + 1) * C)
        sl = slice(h * DK, (h + 1) * DK)
        q = qc[rs, sl]
        k = kc[rs, sl]
        q = q * lax.rsqrt(jnp.sum(q * q, axis=-1, keepdims=True) + EPS) * (DK ** -0.5)
        k = k * lax.rsqrt(jnp.sum(k * k, axis=-1, keepdims=True) + EPS)
        beta = beta_all[rs, H + h:H + h + 1]
        gc = st[c][0][:, h:h + 1]
        gcr = st[c][1][h:h + 1, :]
        decay = incl * jnp.exp(jnp.where(incl > 0, gc - gcr, 0.0))
        kb = k * beta
        egc = jnp.exp(gc)
        gl = gc[C - 1:C]
        qk = _dotg(jnp.concatenate([q, kb], axis=0), k, NT_DIMS)
        st[c, h] = dict(
            attn=qk[:C] * decay,
            m=qk[C:] * (decay - eye),
            rhs=jnp.concatenate([vc[rs, sl] * beta, kb * egc], axis=1),
            qdec=q * egc,
            kdt=(k * jnp.exp(gl - gc)).T,
            egl=jnp.exp(gl))
    for p in pairs:
        st[p]['inv'] = eye - st[p]['m'] * lvl_ref[0]
    for lv in range(1, lvl_ref.shape[0]):
        for p in pairs:
            st[p]['t'] = _dot3(st[p]['m'] * lvl_ref[lv], st[p]['inv'])
        for p in pairs:
            st[p]['inv'] = st[p]['inv'] - _dot3(st[p]['inv'], st[p]['t'])
    for p in pairs:
        st[p]['uw'] = _dot(st[p]['inv'], st[p]['rhs'])
    for c, h in pairs:
        d = st[c, h]
        state = state_ref[h]
        ws = _dot(jnp.concatenate([d['uw'][:, DK:], d['qdec']], axis=0), state)
        v_new = d['uw'][:, :DK] - ws[:C]
        o = ws[C:] + _dot(d['attn'], v_new)
        state_ref[h] = state * d['egl'] + _dot(d['kdt'], v_new)
        rs = slice(c * C, (c + 1) * C)
        sl = slice(h * DK, (h + 1) * DK)
        o_ref[rs, sl] = (_rms(o) * ng_ref[...] * _silu(z_ref[rs, sl])).astype(o_ref.dtype)


def gdn(z, conv_w, a_log, dt_bias, norm_g):
    S = z.shape[0]
    C = GDN_CHUNK
    H = GDN_HEADS
    lvl, tri, trit = _gdn_tables()
    alog = jnp.zeros((1, LANES), F32).at[0, :H].set(a_log)
    dtb = jnp.zeros((1, LANES), F32).at[0, :H].set(dt_bias)
    cw = conv_w.reshape(CONV_K, 3, HEAD_W).transpose(1, 0, 2)
    tm = min(GDN_ROWS, S)
    col = lambda b: pl.BlockSpec((tm, HEAD_W), lambda n, b=b: (n, b))
    lane = pl.BlockSpec((1, LANES), lambda n: (0, 0))
    return pl.pallas_call(
        _gdn_kernel,
        grid=(S // tm,),
        in_specs=[col(6), col(7), col(8), col(9),
                  pl.BlockSpec((tm, LANES), lambda n: (n, AB_COL0 // LANES)),
                  pl.BlockSpec((3, CONV_K, HEAD_W), lambda n: (0, 0, 0)),
                  lane, lane, lane,
                  pl.BlockSpec(lvl.shape, lambda n: (0, 0, 0)),
                  pl.BlockSpec(tri.shape, lambda n: (0, 0)),
                  pl.BlockSpec(trit.shape, lambda n: (0, 0))],
        out_specs=pl.BlockSpec((tm, HEAD_W), lambda n: (n, 0)),
        out_shape=jax.ShapeDtypeStruct((S, HEAD_W), BF16),
        scratch_shapes=[pltpu.VMEM((3, SUBLANES, HEAD_W), F32),
                        pltpu.VMEM((H, GDN_DK, GDN_DK), F32)],
        compiler_params=_params("arbitrary"),
        name="gdn",
    )(z, z, z, z, z, cw, alog, dtb, norm_g.reshape(1, GDN_DK), lvl, tri, trit)


ROUTE_LANES = LANES


def _merge_kernel(yr_ref, yl_ref, yg_ref, g0_ref, g1_ref, g2_ref, x_ref, gt_ref, bp_ref, wo_ref,
                  ng_ref, sc_ref, sh_ref, wr_ref, br_ref, xo_ref, h_ref, lg_ref):
    merged = jax.nn.sigmoid(g0_ref[...]) * jnp.dot(yr_ref[...], bp_ref[0], preferred_element_type=F32)
    merged += jax.nn.sigmoid(g1_ref[...]) * jnp.dot(yl_ref[...], bp_ref[1], preferred_element_type=F32)
    merged += jax.nn.sigmoid(g2_ref[...]) * jnp.dot(yg_ref[...], bp_ref[2], preferred_element_type=F32)
    out = jnp.dot(merged.astype(BF16), wo_ref[...], preferred_element_type=F32)
    x = x_ref[...] + gt_ref[...] * out
    xo_ref[...] = x
    h = (_rms(x) * ng_ref[...]) * (1.0 + sc_ref[...]) + sh_ref[...]
    h_ref[...] = h
    lg_ref[...] = _dot32(h, wr_ref[...]) + br_ref[...]


def merge(y_ret, y_lru, y_gdn, z, x, gt1, bp, w_out, norm2_g, sc2, sh2, w_route, b_route):
    S, D = x.shape
    tm = min(256, S)
    ycol = pl.BlockSpec((tm, HEAD_W), lambda i: (i, 0))
    gate = lambda b: pl.BlockSpec((tm, D), lambda i, b=b: (i, MAIN_COLS // D + b))
    rows = pl.BlockSpec((tm, D), lambda i: (i, 0))
    vec = pl.BlockSpec((1, D), lambda i: (0, 0))
    once = dict(pipeline_mode=pl.Buffered(1))
    return pl.pallas_call(
        _merge_kernel,
        grid=(S // tm,),
        in_specs=[ycol, ycol, ycol, gate(0), gate(1), gate(2), rows, vec,
                  pl.BlockSpec((N_BRANCH, HEAD_W, D), lambda i: (0, 0, 0), **once),
                  pl.BlockSpec((D, D), lambda i: (0, 0), **once),
                  vec, vec, vec,
                  pl.BlockSpec((D, ROUTE_LANES), lambda i: (0, 0)),
                  pl.BlockSpec((1, ROUTE_LANES), lambda i: (0, 0))],
        out_specs=[rows, rows, pl.BlockSpec((tm, ROUTE_LANES), lambda i: (i, 0))],
        out_shape=[jax.ShapeDtypeStruct((S, D), F32), jax.ShapeDtypeStruct((S, D), F32),
                   jax.ShapeDtypeStruct((S, ROUTE_LANES), F32)],
        compiler_params=_params("parallel"),
        name="merge",
    )(y_ret, y_lru, y_gdn, z, z, z, x, gt1, bp, w_out, norm2_g, sc2, sh2, w_route, b_route)


R_E0, R_E1, R_W0, R_W1, R_RANK0, R_RANK1 = range(6)


def _first_max(vals, lane):
    m = jnp.max(vals, axis=-1, keepdims=True)
    idx = jnp.min(jnp.where(vals == m, lane, ROUTE_LANES), axis=-1, keepdims=True)
    return m, idx


def _route_kernel(lg_ref, o_ref, cnt_ref, carry_ref):
    @pl.when(pl.program_id(0) == 0)
    def _():
        carry_ref[...] = jnp.zeros_like(carry_ref)

    lg = lg_ref[...]
    tm = lg.shape[0]
    lane = lax.broadcasted_iota(jnp.int32, lg.shape, 1)
    neg = jnp.float32(-jnp.inf)
    grp = jnp.where(lane < N_GROUPS, lg, neg)
    gmax, gsel = _first_max(grp, lane)
    pg_top = 1.0 / jnp.sum(jnp.exp(grp - gmax), axis=-1, keepdims=True)
    lo = N_GROUPS + gsel * EXPERTS_PER_GROUP
    inside = jnp.where((lane >= lo) & (lane < lo + EXPERTS_PER_GROUP), lg, neg)
    v0, i0 = _first_max(inside, lane)
    v1, i1 = _first_max(jnp.where(lane == i0, neg, inside), lane)
    e1w = jnp.exp(v1 - v0)
    w0 = pg_top / (1.0 + e1w)
    w1 = pg_top * e1w / (1.0 + e1w)
    hit0 = lane == i0
    hit1 = lane == i1
    hits = (hit0 | hit1).astype(BF16)
    row = lax.broadcasted_iota(jnp.int32, (tm, tm), 0)
    col = lax.broadcasted_iota(jnp.int32, (tm, tm), 1)
    before = jnp.dot((row > col).astype(BF16), hits, preferred_element_type=F32) + carry_ref[0:1, :]
    rank0 = jnp.sum(jnp.where(hit0, before, 0.0), axis=-1, keepdims=True)
    rank1 = jnp.sum(jnp.where(hit1, before, 0.0), axis=-1, keepdims=True)
    total = carry_ref[0:1, :] + jnp.sum(hits.astype(F32), axis=0, keepdims=True)
    carry_ref[0:1, :] = total
    cnt_ref[...] = jnp.broadcast_to(total, cnt_ref.shape)
    rec = jnp.zeros(lg.shape, F32)
    for slot, val in ((R_E0, (i0 - N_GROUPS).astype(F32)), (R_E1, (i1 - N_GROUPS).astype(F32)),
                      (R_W0, w0), (R_W1, w1), (R_RANK0, rank0), (R_RANK1, rank1)):
        rec = jnp.where(lane == slot, val, rec)
    o_ref[...] = rec


def route(logits):
    S = logits.shape[0]
    tm = min(256, S)
    return pl.pallas_call(
        _route_kernel,
        grid=(S // tm,),
        in_specs=[pl.BlockSpec((tm, ROUTE_LANES), lambda i: (i, 0))],
        out_specs=[pl.BlockSpec((tm, ROUTE_LANES), lambda i: (i, 0)),
                   pl.BlockSpec((SUBLANES, ROUTE_LANES), lambda i: (0, 0))],
        out_shape=[jax.ShapeDtypeStruct((S, ROUTE_LANES), F32),
                   jax.ShapeDtypeStruct((SUBLANES, ROUTE_LANES), F32)],
        scratch_shapes=[pltpu.VMEM((SUBLANES, ROUTE_LANES), F32)],
        compiler_params=_params("arbitrary"),
        name="route",
    )(logits)


DMA_UNROLL = SUBLANES


def _expert_kernel(be_ref, nused_ref, nvalid_ref, src_ref, nsrc_ref, dst_ref, h_hbm, w1_ref, w3_ref, w2_ref,
                   y_hbm, xbuf, ybuf, wb1, wb3, wb2, gsem, ssem):
    i = pl.program_id(0)
    nused = nused_ref[0]
    B = EXPERT_ROWS
    slot = i % 2

    def gather(idx_ref, s):
        def body(g, c):
            for u in range(DMA_UNROLL):
                r = g * DMA_UNROLL + u
                pltpu.make_async_copy(h_hbm.at[pl.ds(idx_ref[0, r], 1)], xbuf.at[s, pl.ds(r, 1)],
                                      gsem.at[s]).start()
            return c
        lax.fori_loop(0, B // DMA_UNROLL, body, 0)

    def scatter_wait(s, n):
        n8 = pl.multiple_of((n // SUBLANES) * SUBLANES, SUBLANES)

        @pl.when(n8 > 0)
        def _():
            pltpu.make_async_copy(ybuf.at[s, pl.ds(0, n8)], y_hbm.at[pl.ds(0, n8)], ssem.at[s]).wait()

        def body(r, c):
            pltpu.make_async_copy(ybuf.at[s, pl.ds(0, 1)], y_hbm.at[pl.ds(0, 1)], ssem.at[s]).wait()
            return c
        lax.fori_loop(0, n - n8, body, 0)

    @pl.when(i == 0)
    def _():
        gather(src_ref, 0)

    @pl.when(i < nused)
    def _():
        @pl.when(i + 1 < nused)
        def _():
            gather(nsrc_ref, 1 - slot)

        @pl.when((i == 0) | (be_ref[i] != be_ref[jnp.maximum(i - 1, 0)]))
        def _():
            wb1[...] = w1_ref[...].astype(BF16)
            wb3[...] = w3_ref[...].astype(BF16)
            wb2[...] = w2_ref[...].astype(BF16)

        pltpu.make_async_copy(h_hbm.at[pl.ds(0, B)], xbuf.at[slot], gsem.at[slot]).wait()
        x = xbuf[slot].astype(BF16)
        a1 = jnp.dot(x, wb1[...], preferred_element_type=F32)
        a3 = jnp.dot(x, wb3[...], preferred_element_type=F32)
        mid = (_silu(a1) * a3).astype(BF16)
        ybuf[slot] = jnp.dot(mid, wb2[...], preferred_element_type=F32)

        @pl.when(i >= 1)
        def _():
            scatter_wait(1 - slot, nvalid_ref[jnp.maximum(i - 1, 0)])

        def put(r):
            pltpu.make_async_copy(ybuf.at[slot, pl.ds(r, 1)], y_hbm.at[pl.ds(dst_ref[0, r], 1)],
                                  ssem.at[slot]).start()

        def body_group(g, c):
            for u in range(DMA_UNROLL):
                put(g * DMA_UNROLL + u)
            return c

        def body_row(r, c):
            put(r)
            return c
        n = nvalid_ref[i]
        groups = n // DMA_UNROLL
        lax.fori_loop(0, groups, body_group, 0)
        lax.fori_loop(groups * DMA_UNROLL, n, body_row, 0)

        @pl.when(i == nused - 1)
        def _():
            scatter_wait(slot, nvalid_ref[i])


def experts(h, block_e, nused, nvalid, src_row, dst_row, w1, w3, w2, layer):
    T, D = h.shape
    nb = block_e.shape[0]
    B = EXPERT_ROWS
    idx = lambda f: pl.BlockSpec((None, 1, B), f, memory_space=pltpu.SMEM)
    wspec = lambda shp: pl.BlockSpec((None, None) + shp, lambda i, be, nu, nv: (layer, be[i], 0, 0))
    grid_spec = pltpu.PrefetchScalarGridSpec(
        num_scalar_prefetch=3,
        grid=(nb,),
        in_specs=[idx(lambda i, be, nu, nv: (i, 0, 0)),
                  idx(lambda i, be, nu, nv: (jnp.minimum(i + 1, nb - 1), 0, 0)),
                  idx(lambda i, be, nu, nv: (i, 0, 0)),
                  pl.BlockSpec(memory_space=pl.ANY),
                  wspec((D, D_EXPERT)), wspec((D, D_EXPERT)), wspec((D_EXPERT, D))],
        out_specs=pl.BlockSpec(memory_space=pl.ANY),
        scratch_shapes=[pltpu.VMEM((2, B, D), F32), pltpu.VMEM((2, B, D), F32),
                        pltpu.VMEM((D, D_EXPERT), BF16), pltpu.VMEM((D, D_EXPERT), BF16),
                        pltpu.VMEM((D_EXPERT, D), BF16),
                        pltpu.SemaphoreType.DMA((2,)), pltpu.SemaphoreType.DMA((2,))])
    src3 = src_row.reshape(nb, 1, B)
    return pl.pallas_call(
        _expert_kernel,
        grid_spec=grid_spec,
        out_shape=jax.ShapeDtypeStruct((2 * T, D), F32),
        compiler_params=_params("arbitrary"),
        name="experts",
    )(block_e, nused, nvalid, src3, src3, dst_row.reshape(nb, 1, B), h, w1, w3, w2)


def dispatch_plan(rec, counts, T):
    B = EXPERT_ROWS
    A = 2 * T
    nb = A // B + N_EXPERTS
    cnt = counts[0, N_GROUPS:N_GROUPS + N_EXPERTS].astype(jnp.int32)
    padded = ((cnt + B - 1) // B) * B
    pend = jnp.cumsum(padded)
    pstart = pend - padded
    e = rec[:, R_E0:R_E1 + 1].astype(jnp.int32)
    rank = rec[:, R_RANK0:R_RANK1 + 1].astype(jnp.int32)
    dest = (pstart[e] + rank).reshape(-1)
    slot_a = jnp.zeros((nb * B,), jnp.int32).at[dest].set(jnp.arange(A, dtype=jnp.int32))
    src_row = slot_a >> 1
    dst_row = (slot_a & 1) * T + src_row
    first = jnp.arange(nb, dtype=jnp.int32) * B
    block_e = jnp.minimum(jnp.sum((first[:, None] >= pend[None, :]).astype(jnp.int32), axis=1),
                          N_EXPERTS - 1)
    nvalid = jnp.clip(cnt[block_e] - (first - pstart[block_e]), 0, B).astype(jnp.int32)
    nused = (pend[-1] // B).astype(jnp.int32).reshape(1)
    return block_e, nused, nvalid, src_row, dst_row


def _combine_kernel(final, x_ref, y0_ref, y1_ref, rec_ref, gt_ref, fg_ref, o_ref):
    rec = rec_ref[...]
    moe = rec[:, R_W0:R_W0 + 1] * y0_ref[...] + rec[:, R_W1:R_W1 + 1] * y1_ref[...]
    x = x_ref[...] + gt_ref[...] * moe
    o_ref[...] = _rms(x) * fg_ref[...] if final else x


def combine(x, y, rec, gt2, final_g, final):
    S, D = x.shape
    tm = min(512, S)
    y3 = y.reshape(2, y.shape[0] // 2, D)
    rows = pl.BlockSpec((tm, D), lambda i: (i, 0))
    vec = pl.BlockSpec((1, D), lambda i: (0, 0))
    plane = lambda k: pl.BlockSpec((None, tm, D), lambda i, k=k: (k, i, 0))
    return pl.pallas_call(
        functools.partial(_combine_kernel, final),
        grid=(S // tm,),
        in_specs=[rows, plane(0), plane(1), pl.BlockSpec((tm, ROUTE_LANES), lambda i: (i, 0)), vec, vec],
        out_specs=rows,
        out_shape=jax.ShapeDtypeStruct((S, D), F32),
        compiler_params=_params("parallel"),
        name="combine",
    )(x, y3, y3, rec, gt2, final_g)


def _rope_tables(positions):
    half = RET_DK // 2
    inv_freq = ROPE_BASE ** (-jnp.arange(half, dtype=F32) * (2.0 / RET_DK))
    ang = positions.astype(F32)[:, None] * inv_freq
    cos = jnp.cos(ang)
    sin = jnp.sin(ang)
    return jnp.concatenate([cos, cos], axis=-1), jnp.concatenate([-sin, sin], axis=-1)


def _proj_weight(w_in):
    D = w_in.shape[0]
    n_ab = 2 * GDN_HEADS
    main = w_in[:, :MAIN_COLS]
    ab = w_in[:, MAIN_COLS:MAIN_COLS + n_ab]
    gates = w_in[:, MAIN_COLS + n_ab:]
    pad = jnp.zeros((D, PROJ_COLS - AB_COL0 - n_ab), w_in.dtype)
    return jnp.concatenate([main, gates, ab, pad], axis=1).astype(BF16)


def kernel(x, c, positions, ada_w, ada_b, norm1_g, norm2_g, w_in, lru_conv_w, lru_conv_b, lru_wa, lru_ba,
           lru_wx, lru_bx, lru_lambda, gdn_conv_w, gdn_a_log, gdn_dt_bias, gdn_norm_g, branch_proj, w_out,
           router_grp_w, router_grp_b, router_exp_w, router_exp_b, moe_w1, moe_w3, moe_w2, final_g):
    B, S, D = x.shape
    assert B == 1
    xs = x.reshape(S, D)
    mod = ada_mod(c, ada_w, ada_b)
    cos2, sin2 = _rope_tables(positions.reshape(S))
    row = lambda v: v.reshape(1, -1).astype(F32)
    for l in range(DEPTH):
        sh1, sc1, gt1, sh2, sc2, gt2 = (row(m) for m in jnp.split(mod[l], 6))
        z = norm_proj(xs, row(norm1_g[l]), sc1, sh1, _proj_weight(w_in[l]))
        y_ret = retention(z, cos2, sin2)
        y_lru = rglru(z, lru_conv_w[l], lru_conv_b[l], lru_wa[l], lru_ba[l], lru_wx[l], lru_bx[l],
                      lru_lambda[l])
        y_gdn = gdn(z, gdn_conv_w[l], gdn_a_log[l], gdn_dt_bias[l], gdn_norm_g[l])
        n_r = N_GROUPS + N_EXPERTS
        w_route = jnp.concatenate([router_grp_w[l], router_exp_w[l],
                                   jnp.zeros((D, ROUTE_LANES - n_r), F32)], axis=1)
        b_route = jnp.concatenate([router_grp_b[l], router_exp_b[l],
                                   jnp.zeros((ROUTE_LANES - n_r,), F32)]).reshape(1, ROUTE_LANES)
        xs, h2, logits = merge(y_ret, y_lru, y_gdn, z, xs, gt1, branch_proj[l].astype(BF16),
                               w_out[l].astype(BF16), row(norm2_g[l]), sc2, sh2, w_route, b_route)
        rec, counts = route(logits)
        block_e, nused, nvalid, src_row, dst_row = dispatch_plan(rec, counts, S)
        y = experts(h2, block_e, nused, nvalid, src_row, dst_row, moe_w1, moe_w3, moe_w2, l)
        xs = combine(xs, y, rec, gt2, row(final_g), final=(l == DEPTH - 1))
    return xs.reshape(B, S, D)
```

```python
import functools
import math

import numpy as np
import jax
import jax.numpy as jnp
from jax import lax
from jax.experimental import pallas as pl
from jax.experimental.pallas import tpu as pltpu

F32 = jnp.float32
BF16 = jnp.bfloat16
HIGHEST = lax.Precision.HIGHEST

D_MODEL = 2048
DEPTH = 2
RET_HEADS = 8
RET_DK = 128
RET_CHUNK = 128
ROPE_BASE = 10000.0
LRU_W = 1024
LRU_BLOCKS = 8
LRU_BS = LRU_W // LRU_BLOCKS
LRU_C = 8.0
CONV_K = 4
GDN_HEADS = 8
GDN_DK = 128
GDN_CHUNK = 64
N_BRANCH = 3
N_GROUPS = 4
EXPERTS_PER_GROUP = 8
N_EXPERTS = N_GROUPS * EXPERTS_PER_GROUP
D_EXPERT = 512
EPS = 1e-6

HEAD_W = 1024
MAIN_COLS = 10 * HEAD_W
N_AB = 2 * GDN_HEADS
GATE_COLS = N_BRANCH * D_MODEL
PROJ_TN = 1280
LANES = 128
SUBLANES = 8
EXPERT_ROWS = 256
GDN_ROWS = 2 * GDN_CHUNK
RET_ROWS = 2 * RET_CHUNK
VMEM_LIMIT = 56 * 1024 * 1024

NT_DIMS = (((1,), (1,)), ((), ()))
TN_DIMS = (((0,), (0,)), ((), ()))


def _params(*sem):
    return pltpu.CompilerParams(dimension_semantics=sem, vmem_limit_bytes=VMEM_LIMIT)


def _dot32(a, b):
    return jnp.dot(a, b, preferred_element_type=F32, precision=HIGHEST)


def _dotg32(a, b, dims):
    return lax.dot_general(a, b, dims, preferred_element_type=F32, precision=HIGHEST)


def _dot(a, b):
    return jnp.dot(a.astype(BF16), b.astype(BF16), preferred_element_type=F32)


def _dotg(a, b, dims):
    return lax.dot_general(a.astype(BF16), b.astype(BF16), dims, preferred_element_type=F32)


def _dot3(a, b):
    ah = a.astype(BF16)
    bh = b.astype(BF16)
    al = (a - ah.astype(F32)).astype(BF16)
    bl = (b - bh.astype(F32)).astype(BF16)
    d = functools.partial(jnp.dot, preferred_element_type=F32)
    return d(ah, bh) + (d(ah, bl) + d(al, bh))


def _silu(x):
    return x * jax.nn.sigmoid(x)


def _softplus(x):
    return jnp.maximum(x, 0.0) + jnp.log1p(jnp.exp(-jnp.abs(x)))


def _rms(x):
    return x * lax.rsqrt(jnp.mean(x * x, axis=-1, keepdims=True) + EPS)


def _shift_rows(x, s, fill):
    n = x.shape[0]
    if s % SUBLANES == 0:
        return jnp.concatenate([jnp.full((s, x.shape[1]), fill, x.dtype), x[:n - s]], axis=0)
    row = lax.broadcasted_iota(jnp.int32, x.shape, 0)
    return jnp.where(row < s, fill, pltpu.roll(x, s, 0))


def _causal_conv(x, tail, w):
    n = x.shape[0]
    row = lax.broadcasted_iota(jnp.int32, (SUBLANES, x.shape[1]), 0)
    y = x * w[CONV_K - 1:CONV_K]
    for s in range(1, CONV_K):
        xs = pltpu.roll(x, s, 0)
        head = jnp.where(row < s, pltpu.roll(tail, s, 0), xs[:SUBLANES])
        xs = jnp.concatenate([head, xs[SUBLANES:]], axis=0) if n > SUBLANES else head
        y = y + xs * w[CONV_K - 1 - s:CONV_K - s]
    return y


def _mod_kernel(c_ref, w_ref, b_ref, o_ref):
    c = c_ref[...]
    o_ref[...] = _dot32(_silu(c), w_ref[...]) + b_ref[...]


def ada_mod(c, ada_w, ada_b):
    L, D, N = ada_w.shape
    tn = 1536
    c8 = jnp.broadcast_to(c.astype(F32), (SUBLANES, D))
    out = pl.pallas_call(
        _mod_kernel,
        grid=(L, N // tn),
        in_specs=[pl.BlockSpec((SUBLANES, D), lambda l, j: (0, 0)),
                  pl.BlockSpec((None, D, tn), lambda l, j: (l, 0, j)),
                  pl.BlockSpec((None, 1, tn), lambda l, j: (l, 0, j))],
        out_specs=pl.BlockSpec((None, SUBLANES, tn), lambda l, j: (l, 0, j)),
        out_shape=jax.ShapeDtypeStruct((L, SUBLANES, N), F32),
        compiler_params=_params("parallel", "parallel"),
        name="ada_mod",
    )(c8, ada_w, ada_b.reshape(L, 1, N))
    return out[:, 0, :]


def _norm_proj_kernel(x_ref, g_ref, sc_ref, sh_ref, w_ref, wab_ref, o_ref, ab_ref, h_ref):
    @pl.when(pl.program_id(1) == 0)
    def _():
        h = (_rms(x_ref[...]) * g_ref[...]) * (1.0 + sc_ref[...]) + sh_ref[...]
        h_ref[...] = h.astype(BF16)
        ab_ref[...] = jnp.dot(h_ref[...], wab_ref[...], preferred_element_type=F32)

    o_ref[...] = jnp.dot(h_ref[...], w_ref[...], preferred_element_type=F32)


def norm_proj(x, g, sc, sh, w, w_ab, layer):
    S, D = x.shape
    tm = min(1024, S)
    vec = pl.BlockSpec((1, D), lambda i, j: (0, 0))
    return pl.pallas_call(
        _norm_proj_kernel,
        grid=(S // tm, MAIN_COLS // PROJ_TN),
        in_specs=[pl.BlockSpec((tm, D), lambda i, j: (i, 0)), vec, vec, vec,
                  pl.BlockSpec((None, D, PROJ_TN), lambda i, j: (layer, 0, j)),
                  pl.BlockSpec((None, D, LANES), lambda i, j: (layer, 0, 0))],
        out_specs=[pl.BlockSpec((tm, PROJ_TN), lambda i, j: (i, j)),
                   pl.BlockSpec((tm, LANES), lambda i, j: (i, 0))],
        out_shape=[jax.ShapeDtypeStruct((S, MAIN_COLS), F32), jax.ShapeDtypeStruct((S, LANES), F32)],
        scratch_shapes=[pltpu.VMEM((tm, D), BF16)],
        compiler_params=_params("parallel", "arbitrary"),
        name="norm_proj",
    )(x, g, sc, sh, w, w_ab)


def _gate_proj_kernel(x_ref, g_ref, sc_ref, sh_ref, w_ref, o_ref, h_ref):
    @pl.when(pl.program_id(1) == 0)
    def _():
        h = (_rms(x_ref[...]) * g_ref[...]) * (1.0 + sc_ref[...]) + sh_ref[...]
        h_ref[...] = h.astype(BF16)

    o_ref[...] = jax.nn.sigmoid(jnp.dot(h_ref[...], w_ref[...], preferred_element_type=F32)).astype(BF16)


def gate_proj(x, g, sc, sh, w_gates, layer):
    S, D = x.shape
    tm = min(512, S)
    vec = pl.BlockSpec((1, D), lambda i, j: (0, 0))
    return pl.pallas_call(
        _gate_proj_kernel,
        grid=(S // tm, N_BRANCH),
        in_specs=[pl.BlockSpec((tm, D), lambda i, j: (i, 0)), vec, vec, vec,
                  pl.BlockSpec((None, D, D), lambda i, j: (layer, 0, j))],
        out_specs=pl.BlockSpec((tm, D), lambda i, j: (i, j)),
        out_shape=jax.ShapeDtypeStruct((S, GATE_COLS), BF16),
        scratch_shapes=[pltpu.VMEM((tm, D), BF16)],
        compiler_params=_params("parallel", "arbitrary"),
        name="gate_proj",
    )(x, g, sc, sh, w_gates)


def _ret_tables():
    H, C = RET_HEADS, RET_CHUNK
    log_g = np.log1p(-np.exp2(-5.0 - np.arange(H, dtype=np.float64)))
    pos = np.arange(C, dtype=np.float64)
    rel = pos[:, None] - pos[None, :]
    intra = np.where(rel >= 0, np.exp(log_g[:, None, None] * np.maximum(rel, 0.0)), 0.0)
    qd = np.exp(log_g[:, None] * (pos + 1.0))
    kd = np.exp(log_g[:, None] * (C - 1.0 - pos))
    ones = np.ones((1, 1, LANES))
    cd = [float(np.exp(lg * C)) for lg in log_g]
    return (jnp.asarray(intra, F32), jnp.asarray(qd[:, :, None] * ones, F32),
            jnp.asarray(kd[:, :, None] * ones, F32), cd)


def _ret_kernel(cd, q_ref, k_ref, v_ref, g_ref, cos_ref, sin_ref, intra_ref, qd_ref, kd_ref,
                o_ref, state_ref):
    @pl.when(pl.program_id(0) == 0)
    def _():
        state_ref[...] = jnp.zeros_like(state_ref)

    C = RET_CHUNK
    half = RET_DK // 2
    for c in range(q_ref.shape[0] // C):
        rs = slice(c * C, (c + 1) * C)
        cos2 = cos_ref[rs]
        sin2 = sin_ref[rs]
        for h in range(RET_HEADS):
            sl = slice(h * RET_DK, (h + 1) * RET_DK)
            q = q_ref[rs, sl]
            k = k_ref[rs, sl]
            v = v_ref[rs, sl]
            q = q * cos2 + pltpu.roll(q, half, 1) * sin2
            k = (k * cos2 + pltpu.roll(k, half, 1) * sin2) * (RET_DK ** -0.5)
            scores = _dotg(q, k, NT_DIMS) * intra_ref[h]
            state = state_ref[h]
            o = _dot(scores, v) + _dot(q * qd_ref[h], state)
            state_ref[h] = state * cd[h] + _dotg(k * kd_ref[h], v, TN_DIMS)
            o_ref[rs, sl] = (_rms(o) * _silu(g_ref[rs, sl])).astype(o_ref.dtype)


def retention(z, cos2, sin2):
    S = z.shape[0]
    C = min(RET_ROWS, S)
    intra, qd, kd, cd = _ret_tables()
    col = lambda b: pl.BlockSpec((C, HEAD_W), lambda n, b=b: (n, b))
    tab = lambda a: pl.BlockSpec(a.shape, lambda n: (0, 0, 0))
    rot = pl.BlockSpec((C, RET_DK), lambda n: (n, 0))
    return pl.pallas_call(
        functools.partial(_ret_kernel, cd),
        grid=(S // C,),
        in_specs=[col(0), col(1), col(2), col(3), rot, rot, tab(intra), tab(qd), tab(kd)],
        out_specs=pl.BlockSpec((C, HEAD_W), lambda n: (n, 0)),
        out_shape=jax.ShapeDtypeStruct((S, HEAD_W), BF16),
        scratch_shapes=[pltpu.VMEM((RET_HEADS, RET_DK, RET_DK), F32)],
        compiler_params=_params("arbitrary"),
        name="retention",
    )(z, z, z, z, cos2, sin2, intra, qd, kd)


def _lru_kernel(x_ref, gate_ref, cw_ref, cb_ref, wa_ref, ba_ref, wx_ref, bx_ref, lam_ref,
                o_ref, tail_ref, carry_ref):
    @pl.when(pl.program_id(0) == 0)
    def _():
        tail_ref[...] = jnp.zeros_like(tail_ref)
        carry_ref[...] = jnp.zeros_like(carry_ref)

    tm = x_ref.shape[0]
    for j in range(LRU_BLOCKS):
        sl = slice(j * LRU_BS, (j + 1) * LRU_BS)
        x = x_ref[:, sl]
        xc = _causal_conv(x, tail_ref[:, sl], cw_ref[:, sl]) + cb_ref[:, sl]
        tail_ref[:, sl] = x[tm - SUBLANES:]
        r = jax.nn.sigmoid(_dot(xc, wa_ref[j]) + ba_ref[:, sl])
        i = jax.nn.sigmoid(_dot(xc, wx_ref[j]) + bx_ref[:, sl])
        log_a = (-LRU_C * r) * _softplus(-lam_ref[:, sl])
        a = jnp.exp(log_a)
        t = jnp.tanh(log_a)
        b = jnp.sqrt(-2.0 * t / (1.0 - t)) * (i * xc)
        s = 1
        while s < tm:
            b = b + a * _shift_rows(b, s, 0.0)
            a = a * _shift_rows(a, s, 1.0)
            s *= 2
        h = b + a * carry_ref[0:1, sl]
        carry_ref[0:1, sl] = h[tm - 1:tm]
        g = gate_ref[:, sl]
        gelu = g * (0.5 * (1.0 + jnp.tanh(math.sqrt(2.0 / math.pi) * (g + 0.044715 * (g * g * g)))))
        o_ref[:, sl] = (h * gelu).astype(o_ref.dtype)


def rglru(z, conv_w, conv_b, wa, ba, wx, bx, lam):
    S = z.shape[0]
    tm = min(256, S)
    row = lambda a: a.reshape(1, LRU_W)
    vec = pl.BlockSpec((1, LRU_W), lambda n: (0, 0))
    wspec = pl.BlockSpec((LRU_BLOCKS, LRU_BS, LRU_BS), lambda n: (0, 0, 0))
    return pl.pallas_call(
        _lru_kernel,
        grid=(S // tm,),
        in_specs=[pl.BlockSpec((tm, HEAD_W), lambda n: (n, 4)),
                  pl.BlockSpec((tm, HEAD_W), lambda n: (n, 5)),
                  pl.BlockSpec((CONV_K, LRU_W), lambda n: (0, 0)), vec, wspec, vec, wspec, vec, vec],
        out_specs=pl.BlockSpec((tm, HEAD_W), lambda n: (n, 0)),
        out_shape=jax.ShapeDtypeStruct((S, HEAD_W), BF16),
        scratch_shapes=[pltpu.VMEM((SUBLANES, LRU_W), F32), pltpu.VMEM((SUBLANES, LRU_W), F32)],
        compiler_params=_params("arbitrary"),
        name="rglru",
    )(z, z, conv_w, row(conv_b), wa, row(ba), wx, row(bx), row(lam))


def _gdn_tables():
    C = GDN_CHUNK
    i = np.arange(C)[:, None]
    j = np.arange(C)[None, :]
    levels = []
    s = 1
    while s < C:
        levels.append((i // (2 * s) == j // (2 * s)) & (i % (2 * s) >= s) & (j % (2 * s) < s))
        s *= 2
    return jnp.asarray(np.stack(levels), F32), jnp.asarray(i >= j, F32), jnp.asarray(i <= j, F32)


def _gdn_kernel(q_ref, k_ref, v_ref, z_ref, ab_ref, cw_ref, alog_ref, dtb_ref, ng_ref, lvl_ref, tri_ref,
                trit_ref, o_ref, tail_ref, state_ref):
    @pl.when(pl.program_id(0) == 0)
    def _():
        tail_ref[...] = jnp.zeros_like(tail_ref)
        state_ref[...] = jnp.zeros_like(state_ref)

    C = GDN_CHUNK
    H = GDN_HEADS
    DK = GDN_DK
    rows = q_ref.shape[0]
    incl = tri_ref[...]
    rowi = lax.broadcasted_iota(jnp.int32, (C, C), 0)
    coli = lax.broadcasted_iota(jnp.int32, (C, C), 1)
    eye = (rowi == coli).astype(F32)

    ab = ab_ref[...]
    g_all = -jnp.exp(alog_ref[...]) * _softplus(ab + dtb_ref[...])
    beta_all = jax.nn.sigmoid(ab)

    conv = []
    for p, ref in enumerate((q_ref, k_ref, v_ref)):
        x = ref[...]
        conv.append(_silu(_causal_conv(x, tail_ref[p], cw_ref[p])))
        tail_ref[p] = x[rows - SUBLANES:]
    qc, kc, vc = conv

    pairs = [(c, h) for c in range(rows // C) for h in range(H)]
    st = {}
    for c in range(rows // C):
        g_c = g_all[c * C:(c + 1) * C]
        st[c] = (_dot32(incl, g_c),
                 _dotg32(g_c, trit_ref[...], TN_DIMS))
    for c, h in pairs:
        rs = slice(c * C, (c + 1) * C)
        sl = slice(h * DK, (h + 1) * DK)
        q = qc[rs, sl]
        k = kc[rs, sl]
        q = q * lax.rsqrt(jnp.sum(q * q, axis=-1, keepdims=True) + EPS) * (DK ** -0.5)
        k = k * lax.rsqrt(jnp.sum(k * k, axis=-1, keepdims=True) + EPS)
        beta = beta_all[rs, H + h:H + h + 1]
        gc = st[c][0][:, h:h + 1]
        gcr = st[c][1][h:h + 1, :]
        decay = incl * jnp.exp(jnp.where(incl > 0, gc - gcr, 0.0))
        kb = k * beta
        egc = jnp.exp(gc)
        gl = gc[C - 1:C]
        qk = _dotg(jnp.concatenate([q, kb], axis=0), k, NT_DIMS)
        st[c, h] = dict(
            attn=qk[:C] * decay,
            m=qk[C:] * (decay - eye),
            rhs=jnp.concatenate([vc[rs, sl] * beta, kb * egc], axis=1),
            qdec=q * egc,
            kdt=(k * jnp.exp(gl - gc)).T,
            egl=jnp.exp(gl))
    for p in pairs:
        st[p]['inv'] = eye - st[p]['m'] * lvl_ref[0]
    for lv in range(1, lvl_ref.shape[0]):
        for p in pairs:
            st[p]['t'] = _dot3(st[p]['m'] * lvl_ref[lv], st[p]['inv'])
        for p in pairs:
            st[p]['inv'] = st[p]['inv'] - _dot3(st[p]['inv'], st[p]['t'])
    for p in pairs:
        st[p]['uw'] = _dot(st[p]['inv'], st[p]['rhs'])
    for c, h in pairs:
        d = st[c, h]
        state = state_ref[h]
        ws = _dot(jnp.concatenate([d['uw'][:, DK:], d['qdec']], axis=0), state)
        v_new = d['uw'][:, :DK] - ws[:C]
        o = ws[C:] + _dot(d['attn'], v_new)
        state_ref[h] = state * d['egl'] + _dot(d['kdt'], v_new)
        rs = slice(c * C, (c + 1) * C)
        sl = slice(h * DK, (h + 1) * DK)
        o_ref[rs, sl] = (_rms(o) * ng_ref[...] * _silu(z_ref[rs, sl])).astype(o_ref.dtype)


def gdn(z, z_ab, conv_w, a_log, dt_bias, norm_g):
    S = z.shape[0]
    C = GDN_CHUNK
    H = GDN_HEADS
    lvl, tri, trit = _gdn_tables()
    alog = jnp.zeros((1, LANES), F32).at[0, :H].set(a_log)
    dtb = jnp.zeros((1, LANES), F32).at[0, :H].set(dt_bias)
    cw = conv_w.reshape(CONV_K, 3, HEAD_W).transpose(1, 0, 2)
    tm = min(GDN_ROWS, S)
    col = lambda b: pl.BlockSpec((tm, HEAD_W), lambda n, b=b: (n, b))
    lane = pl.BlockSpec((1, LANES), lambda n: (0, 0))
    return pl.pallas_call(
        _gdn_kernel,
        grid=(S // tm,),
        in_specs=[col(6), col(7), col(8), col(9),
                  pl.BlockSpec((tm, LANES), lambda n: (n, 0)),
                  pl.BlockSpec((3, CONV_K, HEAD_W), lambda n: (0, 0, 0)),
                  lane, lane, lane,
                  pl.BlockSpec(lvl.shape, lambda n: (0, 0, 0)),
                  pl.BlockSpec(tri.shape, lambda n: (0, 0)),
                  pl.BlockSpec(trit.shape, lambda n: (0, 0))],
        out_specs=pl.BlockSpec((tm, HEAD_W), lambda n: (n, 0)),
        out_shape=jax.ShapeDtypeStruct((S, HEAD_W), BF16),
        scratch_shapes=[pltpu.VMEM((3, SUBLANES, HEAD_W), F32),
                        pltpu.VMEM((H, GDN_DK, GDN_DK), F32)],
        compiler_params=_params("arbitrary"),
        name="gdn",
    )(z, z, z, z, z_ab, cw, alog, dtb, norm_g.reshape(1, GDN_DK), lvl, tri, trit)


ROUTE_LANES = LANES


def _merge_kernel(yr_ref, yl_ref, yg_ref, g0_ref, g1_ref, g2_ref, x_ref, gt_ref, bp_ref, wo_ref,
                  ng_ref, sc_ref, sh_ref, wr_ref, br_ref, xo_ref, h_ref, lg_ref):
    merged = g0_ref[...] * jnp.dot(yr_ref[...], bp_ref[0], preferred_element_type=F32)
    merged += g1_ref[...] * jnp.dot(yl_ref[...], bp_ref[1], preferred_element_type=F32)
    merged += g2_ref[...] * jnp.dot(yg_ref[...], bp_ref[2], preferred_element_type=F32)
    out = jnp.dot(merged.astype(BF16), wo_ref[...], preferred_element_type=F32)
    x = x_ref[...] + gt_ref[...] * out
    xo_ref[...] = x
    h = (_rms(x) * ng_ref[...]) * (1.0 + sc_ref[...]) + sh_ref[...]
    h_ref[...] = h
    lg_ref[...] = _dot32(h, wr_ref[...]) + br_ref[...]


def merge(y_ret, y_lru, y_gdn, gates, x, gt1, bp, w_out, norm2_g, sc2, sh2, w_route, b_route):
    S, D = x.shape
    tm = min(256, S)
    ycol = pl.BlockSpec((tm, HEAD_W), lambda i: (i, 0))
    gate = lambda b: pl.BlockSpec((tm, D), lambda i, b=b: (i, b))
    rows = pl.BlockSpec((tm, D), lambda i: (i, 0))
    vec = pl.BlockSpec((1, D), lambda i: (0, 0))
    once = dict(pipeline_mode=pl.Buffered(1))
    return pl.pallas_call(
        _merge_kernel,
        grid=(S // tm,),
        in_specs=[ycol, ycol, ycol, gate(0), gate(1), gate(2), rows, vec,
                  pl.BlockSpec((N_BRANCH, HEAD_W, D), lambda i: (0, 0, 0), **once),
                  pl.BlockSpec((D, D), lambda i: (0, 0), **once),
                  vec, vec, vec,
                  pl.BlockSpec((D, ROUTE_LANES), lambda i: (0, 0)),
                  pl.BlockSpec((1, ROUTE_LANES), lambda i: (0, 0))],
        out_specs=[rows, rows, pl.BlockSpec((tm, ROUTE_LANES), lambda i: (i, 0))],
        out_shape=[jax.ShapeDtypeStruct((S, D), F32), jax.ShapeDtypeStruct((S, D), F32),
                   jax.ShapeDtypeStruct((S, ROUTE_LANES), F32)],
        compiler_params=_params("parallel"),
        name="merge",
    )(y_ret, y_lru, y_gdn, gates, gates, gates, x, gt1, bp, w_out, norm2_g, sc2, sh2, w_route, b_route)


R_E0, R_E1, R_W0, R_W1, R_RANK0, R_RANK1 = range(6)


def _first_max(vals, lane):
    m = jnp.max(vals, axis=-1, keepdims=True)
    idx = jnp.min(jnp.where(vals == m, lane, ROUTE_LANES), axis=-1, keepdims=True)
    return m, idx


def _route_kernel(lg_ref, o_ref, cnt_ref, carry_ref):
    @pl.when(pl.program_id(0) == 0)
    def _():
        carry_ref[...] = jnp.zeros_like(carry_ref)

    lg = lg_ref[...]
    tm = lg.shape[0]
    lane = lax.broadcasted_iota(jnp.int32, lg.shape, 1)
    neg = jnp.float32(-jnp.inf)
    grp = jnp.where(lane < N_GROUPS, lg, neg)
    gmax, gsel = _first_max(grp, lane)
    pg_top = 1.0 / jnp.sum(jnp.exp(grp - gmax), axis=-1, keepdims=True)
    lo = N_GROUPS + gsel * EXPERTS_PER_GROUP
    inside = jnp.where((lane >= lo) & (lane < lo + EXPERTS_PER_GROUP), lg, neg)
    v0, i0 = _first_max(inside, lane)
    v1, i1 = _first_max(jnp.where(lane == i0, neg, inside), lane)
    e1w = jnp.exp(v1 - v0)
    w0 = pg_top / (1.0 + e1w)
    w1 = pg_top * e1w / (1.0 + e1w)
    hit0 = lane == i0
    hit1 = lane == i1
    hits = (hit0 | hit1).astype(BF16)
    row = lax.broadcasted_iota(jnp.int32, (tm, tm), 0)
    col = lax.broadcasted_iota(jnp.int32, (tm, tm), 1)
    before = jnp.dot((row > col).astype(BF16), hits, preferred_element_type=F32) + carry_ref[0:1, :]
    rank0 = jnp.sum(jnp.where(hit0, before, 0.0), axis=-1, keepdims=True)
    rank1 = jnp.sum(jnp.where(hit1, before, 0.0), axis=-1, keepdims=True)
    total = carry_ref[0:1, :] + jnp.sum(hits.astype(F32), axis=0, keepdims=True)
    carry_ref[0:1, :] = total
    cnt_ref[...] = jnp.broadcast_to(total, cnt_ref.shape)
    rec = jnp.zeros(lg.shape, F32)
    for slot, val in ((R_E0, (i0 - N_GROUPS).astype(F32)), (R_E1, (i1 - N_GROUPS).astype(F32)),
                      (R_W0, w0), (R_W1, w1), (R_RANK0, rank0), (R_RANK1, rank1)):
        rec = jnp.where(lane == slot, val, rec)
    o_ref[...] = rec


def route(logits):
    S = logits.shape[0]
    tm = min(256, S)
    return pl.pallas_call(
        _route_kernel,
        grid=(S // tm,),
        in_specs=[pl.BlockSpec((tm, ROUTE_LANES), lambda i: (i, 0))],
        out_specs=[pl.BlockSpec((tm, ROUTE_LANES), lambda i: (i, 0)),
                   pl.BlockSpec((SUBLANES, ROUTE_LANES), lambda i: (0, 0))],
        out_shape=[jax.ShapeDtypeStruct((S, ROUTE_LANES), F32),
                   jax.ShapeDtypeStruct((SUBLANES, ROUTE_LANES), F32)],
        scratch_shapes=[pltpu.VMEM((SUBLANES, ROUTE_LANES), F32)],
        compiler_params=_params("arbitrary"),
        name="route",
    )(logits)


def _expert_kernel(be_ref, nused_ref, src_ref, nsrc_ref, dst_ref, h_hbm, w1_ref, w3_ref, w2_ref,
                   y_hbm, xbuf, ybuf, wb1, wb3, wb2, gsem, ssem):
    i = pl.program_id(0)
    nused = nused_ref[0]
    B = EXPERT_ROWS
    T = h_hbm.shape[0]
    slot = i % 2

    def gather(idx_ref, s):
        for r in range(B):
            pltpu.make_async_copy(h_hbm.at[pl.ds(idx_ref[0, r], 1)], xbuf.at[s, pl.ds(r, 1)],
                                  gsem.at[s]).start()

    def scatter_wait(s):
        pltpu.make_async_copy(ybuf.at[s], y_hbm.at[pl.ds(0, B)], ssem.at[s]).wait()

    @pl.when(i == 0)
    def _():
        gather(src_ref, 0)
        ybuf[1] = jnp.zeros(ybuf.shape[1:], ybuf.dtype)
        for plane in range(2):
            fill = pltpu.make_async_copy(ybuf.at[1], y_hbm.at[pl.ds(plane * (T + B) + T, B)], ssem.at[1])
            fill.start()
            fill.wait()

    @pl.when(i < nused)
    def _():
        @pl.when(i + 1 < nused)
        def _():
            gather(nsrc_ref, 1 - slot)

        @pl.when((i == 0) | (be_ref[i] != be_ref[jnp.maximum(i - 1, 0)]))
        def _():
            wb1[...] = w1_ref[...].astype(BF16)
            wb3[...] = w3_ref[...].astype(BF16)
            wb2[...] = w2_ref[...].astype(BF16)

        pltpu.make_async_copy(h_hbm.at[pl.ds(0, B)], xbuf.at[slot], gsem.at[slot]).wait()
        x = xbuf[slot].astype(BF16)
        a1 = jnp.dot(x, wb1[...], preferred_element_type=F32)
        a3 = jnp.dot(x, wb3[...], preferred_element_type=F32)
        mid = (_silu(a1) * a3).astype(BF16)
        ybuf[slot] = jnp.dot(mid, wb2[...], preferred_element_type=F32)

        @pl.when(i >= 1)
        def _():
            scatter_wait(1 - slot)

        for r in range(B):
            pltpu.make_async_copy(ybuf.at[slot, pl.ds(r, 1)], y_hbm.at[pl.ds(dst_ref[0, r], 1)],
                                  ssem.at[slot]).start()

        @pl.when(i == nused - 1)
        def _():
            scatter_wait(slot)


def experts(h, block_e, nused, src_row, dst_row, w1, w3, w2, layer):
    T, D = h.shape
    nb = block_e.shape[0]
    B = EXPERT_ROWS
    idx = lambda f: pl.BlockSpec((None, 1, B), f, memory_space=pltpu.SMEM)
    wspec = lambda shp: pl.BlockSpec((None, None) + shp, lambda i, be, nu: (layer, be[i], 0, 0))
    grid_spec = pltpu.PrefetchScalarGridSpec(
        num_scalar_prefetch=2,
        grid=(nb,),
        in_specs=[idx(lambda i, be, nu: (i, 0, 0)),
                  idx(lambda i, be, nu: (jnp.minimum(i + 1, nb - 1), 0, 0)),
                  idx(lambda i, be, nu: (i, 0, 0)),
                  pl.BlockSpec(memory_space=pl.ANY),
                  wspec((D, D_EXPERT)), wspec((D, D_EXPERT)), wspec((D_EXPERT, D))],
        out_specs=pl.BlockSpec(memory_space=pl.ANY),
        scratch_shapes=[pltpu.VMEM((2, B, D), F32), pltpu.VMEM((2, B, D), F32),
                        pltpu.VMEM((D, D_EXPERT), BF16), pltpu.VMEM((D, D_EXPERT), BF16),
                        pltpu.VMEM((D_EXPERT, D), BF16),
                        pltpu.SemaphoreType.DMA((2,)), pltpu.SemaphoreType.DMA((2,))])
    src3 = src_row.reshape(nb, 1, B)
    return pl.pallas_call(
        _expert_kernel,
        grid_spec=grid_spec,
        out_shape=jax.ShapeDtypeStruct((2 * (T + B), D), F32),
        compiler_params=_params("arbitrary"),
        name="experts",
    )(block_e, nused, src3, src3, dst_row.reshape(nb, 1, B), h, w1, w3, w2)


def dispatch_plan(rec, counts, T):
    B = EXPERT_ROWS
    A = 2 * T
    nb = A // B + N_EXPERTS
    cnt = counts[0, N_GROUPS:N_GROUPS + N_EXPERTS].astype(jnp.int32)
    padded = ((cnt + B - 1) // B) * B
    pend = jnp.cumsum(padded)
    pstart = pend - padded
    e = rec[:, R_E0:R_E1 + 1].astype(jnp.int32)
    rank = rec[:, R_RANK0:R_RANK1 + 1].astype(jnp.int32)
    dest = (pstart[e] + rank).reshape(-1)
    slot_a = jnp.full((nb * B,), -1, jnp.int32).at[dest].set(jnp.arange(A, dtype=jnp.int32))
    slots = jnp.arange(nb * B, dtype=jnp.int32)
    src_row = jnp.maximum(slot_a, 0) >> 1
    dump = (slots // B % 2) * (T + B) + T + slots % B
    dst_row = jnp.where(slot_a >= 0, (slot_a & 1) * (T + B) + src_row, dump)
    first = jnp.arange(nb, dtype=jnp.int32) * B
    block_e = jnp.minimum(jnp.sum((first[:, None] >= pend[None, :]).astype(jnp.int32), axis=1),
                          N_EXPERTS - 1)
    nused = (pend[-1] // B).astype(jnp.int32).reshape(1)
    return block_e, nused, src_row, dst_row


def _combine_kernel(final, x_ref, y0_ref, y1_ref, rec_ref, gt_ref, fg_ref, o_ref):
    rec = rec_ref[...]
    moe = rec[:, R_W0:R_W0 + 1] * y0_ref[...] + rec[:, R_W1:R_W1 + 1] * y1_ref[...]
    x = x_ref[...] + gt_ref[...] * moe
    o_ref[...] = _rms(x) * fg_ref[...] if final else x


def combine(x, y, rec, gt2, final_g, final):
    S, D = x.shape
    tm = min(512, S)
    y3 = y.reshape(2, y.shape[0] // 2, D)
    rows = pl.BlockSpec((tm, D), lambda i: (i, 0))
    vec = pl.BlockSpec((1, D), lambda i: (0, 0))
    plane = lambda k: pl.BlockSpec((None, tm, D), lambda i, k=k: (k, i, 0))
    return pl.pallas_call(
        functools.partial(_combine_kernel, final),
        grid=(S // tm,),
        in_specs=[rows, plane(0), plane(1), pl.BlockSpec((tm, ROUTE_LANES), lambda i: (i, 0)), vec, vec],
        out_specs=rows,
        out_shape=jax.ShapeDtypeStruct((S, D), F32),
        compiler_params=_params("parallel"),
        name="combine",
    )(x, y3, y3, rec, gt2, final_g)


def _rope_tables(positions):
    half = RET_DK // 2
    inv_freq = ROPE_BASE ** (-jnp.arange(half, dtype=F32) * (2.0 / RET_DK))
    ang = positions.astype(F32)[:, None] * inv_freq
    cos = jnp.cos(ang)
    sin = jnp.sin(ang)
    return jnp.concatenate([cos, cos], axis=-1), jnp.concatenate([-sin, sin], axis=-1)


def _proj_weights(w_in):
    L, D, _ = w_in.shape
    w_all = w_in.astype(BF16)
    ab = w_in[:, :, MAIN_COLS:MAIN_COLS + N_AB]
    w_ab = jnp.concatenate([ab, jnp.zeros((L, D, LANES - N_AB), w_in.dtype)], axis=2).astype(BF16)
    w_gates = w_in[:, :, MAIN_COLS + N_AB:].astype(BF16)
    return w_all, w_ab, w_gates


def kernel(x, c, positions, ada_w, ada_b, norm1_g, norm2_g, w_in, lru_conv_w, lru_conv_b, lru_wa, lru_ba,
           lru_wx, lru_bx, lru_lambda, gdn_conv_w, gdn_a_log, gdn_dt_bias, gdn_norm_g, branch_proj, w_out,
           router_grp_w, router_grp_b, router_exp_w, router_exp_b, moe_w1, moe_w3, moe_w2, final_g):
    B, S, D = x.shape
    assert B == 1
    xs = x.reshape(S, D)
    mod = ada_mod(c, ada_w, ada_b)
    cos2, sin2 = _rope_tables(positions.reshape(S))
    row = lambda v: v.reshape(1, -1).astype(F32)
    w_all, w_ab, w_gates = _proj_weights(w_in)
    for l in range(DEPTH):
        sh1, sc1, gt1, sh2, sc2, gt2 = (row(m) for m in jnp.split(mod[l], 6))
        z, z_ab = norm_proj(xs, row(norm1_g[l]), sc1, sh1, w_all, w_ab, l)
        gates = gate_proj(xs, row(norm1_g[l]), sc1, sh1, w_gates, l)
        y_ret = retention(z, cos2, sin2)
        y_lru = rglru(z, lru_conv_w[l], lru_conv_b[l], lru_wa[l], lru_ba[l], lru_wx[l], lru_bx[l],
                      lru_lambda[l])
        y_gdn = gdn(z, z_ab, gdn_conv_w[l], gdn_a_log[l], gdn_dt_bias[l], gdn_norm_g[l])
        n_r = N_GROUPS + N_EXPERTS
        w_route = jnp.concatenate([router_grp_w[l], router_exp_w[l],
                                   jnp.zeros((D, ROUTE_LANES - n_r), F32)], axis=1)
        b_route = jnp.concatenate([router_grp_b[l], router_exp_b[l],
                                   jnp.zeros((ROUTE_LANES - n_r,), F32)]).reshape(1, ROUTE_LANES)
        xs, h2, logits = merge(y_ret, y_lru, y_gdn, gates, xs, gt1, branch_proj[l].astype(BF16),
                               w_out[l].astype(BF16), row(norm2_g[l]), sc2, sh2, w_route, b_route)
        rec, counts = route(logits)
        block_e, nused, src_row, dst_row = dispatch_plan(rec, counts, S)
        y = experts(h2, block_e, nused, src_row, dst_row, moe_w1, moe_w3, moe_w2, l)
        xs = combine(xs, y, rec, gt2, row(final_g), final=(l == DEPTH - 1))
    return xs.reshape(B, S, D)
```

```python
import functools
import math

import numpy as np
import jax
import jax.numpy as jnp
from jax import lax
from jax.experimental import pallas as pl
from jax.experimental.pallas import tpu as pltpu

F32 = jnp.float32
BF16 = jnp.bfloat16
HIGHEST = lax.Precision.HIGHEST

D_MODEL = 2048
DEPTH = 2
RET_HEADS = 8
RET_DK = 128
RET_CHUNK = 128
ROPE_BASE = 10000.0
LRU_W = 1024
LRU_BLOCKS = 8
LRU_BS = LRU_W // LRU_BLOCKS
LRU_C = 8.0
CONV_K = 4
GDN_HEADS = 8
GDN_DK = 128
GDN_CHUNK = 64
N_BRANCH = 3
N_GROUPS = 4
EXPERTS_PER_GROUP = 8
N_EXPERTS = N_GROUPS * EXPERTS_PER_GROUP
D_EXPERT = 512
EPS = 1e-6

HEAD_W = 1024
MAIN_COLS = 10 * HEAD_W
N_AB = 2 * GDN_HEADS
GATE_COLS = N_BRANCH * D_MODEL
PROJ_TN = 1280
GATE_TN = 1024
LANES = 128
SUBLANES = 8
EXPERT_ROWS = 256
GDN_ROWS = 2 * GDN_CHUNK
RET_ROWS = 2 * RET_CHUNK
VMEM_LIMIT = 56 * 1024 * 1024

NT_DIMS = (((1,), (1,)), ((), ()))
TN_DIMS = (((0,), (0,)), ((), ()))


def _params(*sem):
    return pltpu.CompilerParams(dimension_semantics=sem, vmem_limit_bytes=VMEM_LIMIT)


def _dot32(a, b):
    return jnp.dot(a, b, preferred_element_type=F32, precision=HIGHEST)


def _dotg32(a, b, dims):
    return lax.dot_general(a, b, dims, preferred_element_type=F32, precision=HIGHEST)


def _dot(a, b):
    return jnp.dot(a.astype(BF16), b.astype(BF16), preferred_element_type=F32)


def _dotg(a, b, dims):
    return lax.dot_general(a.astype(BF16), b.astype(BF16), dims, preferred_element_type=F32)


def _dot3(a, b):
    ah = a.astype(BF16)
    bh = b.astype(BF16)
    al = (a - ah.astype(F32)).astype(BF16)
    bl = (b - bh.astype(F32)).astype(BF16)
    d = functools.partial(jnp.dot, preferred_element_type=F32)
    return d(ah, bh) + (d(ah, bl) + d(al, bh))


def _silu(x):
    return x * jax.nn.sigmoid(x)


def _softplus(x):
    return jnp.maximum(x, 0.0) + jnp.log1p(jnp.exp(-jnp.abs(x)))


def _rms(x):
    return x * lax.rsqrt(jnp.mean(x * x, axis=-1, keepdims=True) + EPS)


def _store_row_slabs(ref, x):
    for j in range(x.shape[1] // LANES):
        ref[:, j, :] = x[:, j * LANES:(j + 1) * LANES]


def _load_row_slabs(ref):
    return jnp.concatenate([ref[:, j, :] for j in range(ref.shape[1])], axis=1)


def _shift_rows(x, s, fill):
    n = x.shape[0]
    if s % SUBLANES == 0:
        return jnp.concatenate([jnp.full((s, x.shape[1]), fill, x.dtype), x[:n - s]], axis=0)
    row = lax.broadcasted_iota(jnp.int32, x.shape, 0)
    return jnp.where(row < s, fill, pltpu.roll(x, s, 0))


def _causal_conv(x, tail, w):
    n = x.shape[0]
    row = lax.broadcasted_iota(jnp.int32, (SUBLANES, x.shape[1]), 0)
    y = x * w[CONV_K - 1:CONV_K]
    for s in range(1, CONV_K):
        xs = pltpu.roll(x, s, 0)
        head = jnp.where(row < s, pltpu.roll(tail, s, 0), xs[:SUBLANES])
        xs = jnp.concatenate([head, xs[SUBLANES:]], axis=0) if n > SUBLANES else head
        y = y + xs * w[CONV_K - 1 - s:CONV_K - s]
    return y


def _mod_kernel(c_ref, w_ref, b_ref, o_ref):
    c = c_ref[...]
    o_ref[...] = _dot32(_silu(c), w_ref[...]) + b_ref[...]


def ada_mod(c, ada_w, ada_b):
    L, D, N = ada_w.shape
    tn = 1536
    c8 = jnp.broadcast_to(c.astype(F32), (SUBLANES, D))
    out = pl.pallas_call(
        _mod_kernel,
        grid=(L, N // tn),
        in_specs=[pl.BlockSpec((SUBLANES, D), lambda l, j: (0, 0)),
                  pl.BlockSpec((None, D, tn), lambda l, j: (l, 0, j)),
                  pl.BlockSpec((None, 1, tn), lambda l, j: (l, 0, j))],
        out_specs=pl.BlockSpec((None, SUBLANES, tn), lambda l, j: (l, 0, j)),
        out_shape=jax.ShapeDtypeStruct((L, SUBLANES, N), F32),
        compiler_params=_params("parallel", "parallel"),
        name="ada_mod",
    )(c8, ada_w, ada_b.reshape(L, 1, N))
    return out[:, 0, :]


def _norm_proj_kernel(x_ref, g_ref, sc_ref, sh_ref, w_ref, wab_ref, o_ref, ab_ref, h_ref):
    @pl.when(pl.program_id(1) == 0)
    def _():
        h = (_rms(x_ref[...]) * g_ref[...]) * (1.0 + sc_ref[...]) + sh_ref[...]
        h_ref[...] = h.astype(BF16)
        ab_ref[...] = jnp.dot(h_ref[...], wab_ref[...], preferred_element_type=F32)

    o_ref[...] = jnp.dot(h_ref[...], w_ref[...], preferred_element_type=F32)


def norm_proj(x, g, sc, sh, w, w_ab, layer):
    S, D = x.shape
    tm = min(1024, S)
    vec = pl.BlockSpec((1, D), lambda i, j: (0, 0))
    return pl.pallas_call(
        _norm_proj_kernel,
        grid=(S // tm, MAIN_COLS // PROJ_TN),
        in_specs=[pl.BlockSpec((tm, D), lambda i, j: (i, 0)), vec, vec, vec,
                  pl.BlockSpec((None, D, PROJ_TN), lambda i, j: (layer, 0, j)),
                  pl.BlockSpec((None, D, LANES), lambda i, j: (layer, 0, 0))],
        out_specs=[pl.BlockSpec((tm, PROJ_TN), lambda i, j: (i, j)),
                   pl.BlockSpec((tm, LANES), lambda i, j: (i, 0))],
        out_shape=[jax.ShapeDtypeStruct((S, MAIN_COLS), F32), jax.ShapeDtypeStruct((S, LANES), F32)],
        scratch_shapes=[pltpu.VMEM((tm, D), BF16)],
        compiler_params=_params("parallel", "arbitrary"),
        name="norm_proj",
    )(x, g, sc, sh, w, w_ab)


def _gate_proj_kernel(x_ref, g_ref, sc_ref, sh_ref, w_ref, o_ref, h_ref):
    @pl.when(pl.program_id(1) == 0)
    def _():
        h = (_rms(x_ref[...]) * g_ref[...]) * (1.0 + sc_ref[...]) + sh_ref[...]
        h_ref[...] = h.astype(BF16)

    o_ref[...] = jax.nn.sigmoid(jnp.dot(h_ref[...], w_ref[...], preferred_element_type=F32)).astype(BF16)


def gate_proj(x, g, sc, sh, w_gates, layer):
    S, D = x.shape
    tm = min(1024, S)
    tn = GATE_TN
    vec = pl.BlockSpec((1, D), lambda i, j: (0, 0))
    return pl.pallas_call(
        _gate_proj_kernel,
        grid=(S // tm, GATE_COLS // tn),
        in_specs=[pl.BlockSpec((tm, D), lambda i, j: (i, 0)), vec, vec, vec,
                  pl.BlockSpec((None, D, tn), lambda i, j: (layer, 0, j))],
        out_specs=pl.BlockSpec((tm, tn), lambda i, j: (i, j)),
        out_shape=jax.ShapeDtypeStruct((S, GATE_COLS), BF16),
        scratch_shapes=[pltpu.VMEM((tm, D), BF16)],
        compiler_params=_params("parallel", "arbitrary"),
        name="gate_proj",
    )(x, g, sc, sh, w_gates)


def _ret_tables():
    H, C = RET_HEADS, RET_CHUNK
    log_g = np.log1p(-np.exp2(-5.0 - np.arange(H, dtype=np.float64)))
    pos = np.arange(C, dtype=np.float64)
    rel = pos[:, None] - pos[None, :]
    intra = np.where(rel >= 0, np.exp(log_g[:, None, None] * np.maximum(rel, 0.0)), 0.0)
    qd = np.exp(log_g[:, None] * (pos + 1.0))
    kd = np.exp(log_g[:, None] * (C - 1.0 - pos))
    ones = np.ones((1, 1, LANES))
    cd = [float(np.exp(lg * C)) for lg in log_g]
    return (jnp.asarray(intra, F32), jnp.asarray(qd[:, :, None] * ones, F32),
            jnp.asarray(kd[:, :, None] * ones, F32), cd)


def _ret_kernel(cd, q_ref, k_ref, v_ref, g_ref, cos_ref, sin_ref, intra_ref, qd_ref, kd_ref,
                o_ref, state_ref):
    @pl.when(pl.program_id(0) == 0)
    def _():
        state_ref[...] = jnp.zeros_like(state_ref)

    C = RET_CHUNK
    half = RET_DK // 2
    for c in range(q_ref.shape[0] // C):
        rs = slice(c * C, (c + 1) * C)
        cos2 = cos_ref[rs]
        sin2 = sin_ref[rs]
        for h in range(RET_HEADS):
            sl = slice(h * RET_DK, (h + 1) * RET_DK)
            q = q_ref[rs, sl]
            k = k_ref[rs, sl]
            v = v_ref[rs, sl]
            q = q * cos2 + pltpu.roll(q, half, 1) * sin2
            k = (k * cos2 + pltpu.roll(k, half, 1) * sin2) * (RET_DK ** -0.5)
            scores = _dotg(q, k, NT_DIMS) * intra_ref[h]
            state = state_ref[h]
            o = _dot(scores, v) + _dot(q * qd_ref[h], state)
            state_ref[h] = state * cd[h] + _dotg(k * kd_ref[h], v, TN_DIMS)
            o_ref[rs, sl] = (_rms(o) * _silu(g_ref[rs, sl])).astype(o_ref.dtype)


def retention(z, cos2, sin2):
    S = z.shape[0]
    C = min(RET_ROWS, S)
    intra, qd, kd, cd = _ret_tables()
    col = lambda b: pl.BlockSpec((C, HEAD_W), lambda n, b=b: (n, b))
    tab = lambda a: pl.BlockSpec(a.shape, lambda n: (0, 0, 0))
    rot = pl.BlockSpec((C, RET_DK), lambda n: (n, 0))
    return pl.pallas_call(
        functools.partial(_ret_kernel, cd),
        grid=(S // C,),
        in_specs=[col(0), col(1), col(2), col(3), rot, rot, tab(intra), tab(qd), tab(kd)],
        out_specs=pl.BlockSpec((C, HEAD_W), lambda n: (n, 0)),
        out_shape=jax.ShapeDtypeStruct((S, HEAD_W), BF16),
        scratch_shapes=[pltpu.VMEM((RET_HEADS, RET_DK, RET_DK), F32)],
        compiler_params=_params("arbitrary"),
        name="retention",
    )(z, z, z, z, cos2, sin2, intra, qd, kd)


def _lru_kernel(x_ref, gate_ref, cw_ref, cb_ref, wa_ref, ba_ref, wx_ref, bx_ref, lam_ref,
                o_ref, tail_ref, carry_ref):
    @pl.when(pl.program_id(0) == 0)
    def _():
        tail_ref[...] = jnp.zeros_like(tail_ref)
        carry_ref[...] = jnp.zeros_like(carry_ref)

    tm = x_ref.shape[0]
    for j in range(LRU_BLOCKS):
        sl = slice(j * LRU_BS, (j + 1) * LRU_BS)
        x = x_ref[:, sl]
        xc = _causal_conv(x, tail_ref[:, sl], cw_ref[:, sl]) + cb_ref[:, sl]
        tail_ref[:, sl] = x[tm - SUBLANES:]
        r = jax.nn.sigmoid(_dot(xc, wa_ref[j]) + ba_ref[:, sl])
        i = jax.nn.sigmoid(_dot(xc, wx_ref[j]) + bx_ref[:, sl])
        log_a = (-LRU_C * r) * _softplus(-lam_ref[:, sl])
        a = jnp.exp(log_a)
        t = jnp.tanh(log_a)
        b = jnp.sqrt(-2.0 * t / (1.0 - t)) * (i * xc)
        s = 1
        while s < tm:
            b = b + a * _shift_rows(b, s, 0.0)
            a = a * _shift_rows(a, s, 1.0)
            s *= 2
        h = b + a * carry_ref[0:1, sl]
        carry_ref[0:1, sl] = h[tm - 1:tm]
        g = gate_ref[:, sl]
        gelu = g * (0.5 * (1.0 + jnp.tanh(math.sqrt(2.0 / math.pi) * (g + 0.044715 * (g * g * g)))))
        o_ref[:, sl] = (h * gelu).astype(o_ref.dtype)


def rglru(z, conv_w, conv_b, wa, ba, wx, bx, lam):
    S = z.shape[0]
    tm = min(256, S)
    row = lambda a: a.reshape(1, LRU_W)
    vec = pl.BlockSpec((1, LRU_W), lambda n: (0, 0))
    wspec = pl.BlockSpec((LRU_BLOCKS, LRU_BS, LRU_BS), lambda n: (0, 0, 0))
    return pl.pallas_call(
        _lru_kernel,
        grid=(S // tm,),
        in_specs=[pl.BlockSpec((tm, HEAD_W), lambda n: (n, 4)),
                  pl.BlockSpec((tm, HEAD_W), lambda n: (n, 5)),
                  pl.BlockSpec((CONV_K, LRU_W), lambda n: (0, 0)), vec, wspec, vec, wspec, vec, vec],
        out_specs=pl.BlockSpec((tm, HEAD_W), lambda n: (n, 0)),
        out_shape=jax.ShapeDtypeStruct((S, HEAD_W), BF16),
        scratch_shapes=[pltpu.VMEM((SUBLANES, LRU_W), F32), pltpu.VMEM((SUBLANES, LRU_W), F32)],
        compiler_params=_params("arbitrary"),
        name="rglru",
    )(z, z, conv_w, row(conv_b), wa, row(ba), wx, row(bx), row(lam))


def _gdn_tables():
    C = GDN_CHUNK
    i = np.arange(C)[:, None]
    j = np.arange(C)[None, :]
    levels = []
    s = 1
    while s < C:
        levels.append((i // (2 * s) == j // (2 * s)) & (i % (2 * s) >= s) & (j % (2 * s) < s))
        s *= 2
    return jnp.asarray(np.stack(levels), F32), jnp.asarray(i >= j, F32), jnp.asarray(i <= j, F32)


def _gdn_kernel(q_ref, k_ref, v_ref, z_ref, ab_ref, cw_ref, alog_ref, dtb_ref, ng_ref, lvl_ref, tri_ref,
                trit_ref, o_ref, tail_ref, state_ref):
    @pl.when(pl.program_id(0) == 0)
    def _():
        tail_ref[...] = jnp.zeros_like(tail_ref)
        state_ref[...] = jnp.zeros_like(state_ref)

    C = GDN_CHUNK
    H = GDN_HEADS
    DK = GDN_DK
    rows = q_ref.shape[0]
    incl = tri_ref[...]
    rowi = lax.broadcasted_iota(jnp.int32, (C, C), 0)
    coli = lax.broadcasted_iota(jnp.int32, (C, C), 1)
    eye = (rowi == coli).astype(F32)

    ab = ab_ref[...]
    g_all = -jnp.exp(alog_ref[...]) * _softplus(ab + dtb_ref[...])
    beta_all = jax.nn.sigmoid(ab)

    conv = []
    for p, ref in enumerate((q_ref, k_ref, v_ref)):
        x = ref[...]
        conv.append(_silu(_causal_conv(x, tail_ref[p], cw_ref[p])))
        tail_ref[p] = x[rows - SUBLANES:]
    qc, kc, vc = conv

    pairs = [(c, h) for c in range(rows // C) for h in range(H)]
    st = {}
    for c in range(rows // C):
        g_c = g_all[c * C:(c + 1) * C]
        st[c] = (_dot32(incl, g_c),
                 _dotg32(g_c, trit_ref[...], TN_DIMS))
    for c, h in pairs:
        rs = slice(c * C, (c + 1) * C)
        sl = slice(h * DK, (h + 1) * DK)
        q = qc[rs, sl]
        k = kc[rs, sl]
        q = q * lax.rsqrt(jnp.sum(q * q, axis=-1, keepdims=True) + EPS) * (DK ** -0.5)
        k = k * lax.rsqrt(jnp.sum(k * k, axis=-1, keepdims=True) + EPS)
        beta = beta_all[rs, H + h:H + h + 1]
        gc = st[c][0][:, h:h + 1]
        gcr = st[c][1][h:h + 1, :]
        decay = incl * jnp.exp(jnp.where(incl > 0, gc - gcr, 0.0))
        kb = k * beta
        egc = jnp.exp(gc)
        gl = gc[C - 1:C]
        qk = _dotg(jnp.concatenate([q, kb], axis=0), k, NT_DIMS)
        st[c, h] = dict(
            attn=qk[:C] * decay,
            m=qk[C:] * (decay - eye),
            rhs=jnp.concatenate([vc[rs, sl] * beta, kb * egc], axis=1),
            qdec=q * egc,
            kdt=(k * jnp.exp(gl - gc)).T,
            egl=jnp.exp(gl))
    for p in pairs:
        st[p]['inv'] = eye - st[p]['m'] * lvl_ref[0]
    for lv in range(1, lvl_ref.shape[0]):
        for p in pairs:
            st[p]['t'] = _dot3(st[p]['m'] * lvl_ref[lv], st[p]['inv'])
        for p in pairs:
            st[p]['inv'] = st[p]['inv'] - _dot3(st[p]['inv'], st[p]['t'])
    for p in pairs:
        st[p]['uw'] = _dot(st[p]['inv'], st[p]['rhs'])
    for c, h in pairs:
        d = st[c, h]
        state = state_ref[h]
        ws = _dot(jnp.concatenate([d['uw'][:, DK:], d['qdec']], axis=0), state)
        v_new = d['uw'][:, :DK] - ws[:C]
        o = ws[C:] + _dot(d['attn'], v_new)
        state_ref[h] = state * d['egl'] + _dot(d['kdt'], v_new)
        rs = slice(c * C, (c + 1) * C)
        sl = slice(h * DK, (h + 1) * DK)
        o_ref[rs, sl] = (_rms(o) * ng_ref[...] * _silu(z_ref[rs, sl])).astype(o_ref.dtype)


def gdn(z, z_ab, conv_w, a_log, dt_bias, norm_g):
    S = z.shape[0]
    C = GDN_CHUNK
    H = GDN_HEADS
    lvl, tri, trit = _gdn_tables()
    alog = jnp.zeros((1, LANES), F32).at[0, :H].set(a_log)
    dtb = jnp.zeros((1, LANES), F32).at[0, :H].set(dt_bias)
    cw = conv_w.reshape(CONV_K, 3, HEAD_W).transpose(1, 0, 2)
    tm = min(GDN_ROWS, S)
    col = lambda b: pl.BlockSpec((tm, HEAD_W), lambda n, b=b: (n, b))
    lane = pl.BlockSpec((1, LANES), lambda n: (0, 0))
    return pl.pallas_call(
        _gdn_kernel,
        grid=(S // tm,),
        in_specs=[col(6), col(7), col(8), col(9),
                  pl.BlockSpec((tm, LANES), lambda n: (n, 0)),
                  pl.BlockSpec((3, CONV_K, HEAD_W), lambda n: (0, 0, 0)),
                  lane, lane, lane,
                  pl.BlockSpec(lvl.shape, lambda n: (0, 0, 0)),
                  pl.BlockSpec(tri.shape, lambda n: (0, 0)),
                  pl.BlockSpec(trit.shape, lambda n: (0, 0))],
        out_specs=pl.BlockSpec((tm, HEAD_W), lambda n: (n, 0)),
        out_shape=jax.ShapeDtypeStruct((S, HEAD_W), BF16),
        scratch_shapes=[pltpu.VMEM((3, SUBLANES, HEAD_W), F32),
                        pltpu.VMEM((H, GDN_DK, GDN_DK), F32)],
        compiler_params=_params("arbitrary"),
        name="gdn",
    )(z, z, z, z, z_ab, cw, alog, dtb, norm_g.reshape(1, GDN_DK), lvl, tri, trit)


ROUTE_LANES = LANES


def _merge_kernel(yr_ref, yl_ref, yg_ref, g0_ref, g1_ref, g2_ref, x_ref, gt_ref, bp_ref, wo_ref,
                  ng_ref, sc_ref, sh_ref, wr_ref, br_ref, xo_ref, h_ref, lg_ref):
    merged = g0_ref[...] * jnp.dot(yr_ref[...], bp_ref[0], preferred_element_type=F32)
    merged += g1_ref[...] * jnp.dot(yl_ref[...], bp_ref[1], preferred_element_type=F32)
    merged += g2_ref[...] * jnp.dot(yg_ref[...], bp_ref[2], preferred_element_type=F32)
    out = jnp.dot(merged.astype(BF16), wo_ref[...], preferred_element_type=F32)
    x = x_ref[...] + gt_ref[...] * out
    xo_ref[...] = x
    h = (_rms(x) * ng_ref[...]) * (1.0 + sc_ref[...]) + sh_ref[...]
    _store_row_slabs(h_ref, h)
    lg_ref[...] = _dot3(h, wr_ref[...]) + br_ref[...]


def merge(y_ret, y_lru, y_gdn, gates, x, gt1, bp, w_out, norm2_g, sc2, sh2, w_route, b_route):
    S, D = x.shape
    tm = min(256, S)
    ycol = pl.BlockSpec((tm, HEAD_W), lambda i: (i, 0))
    gate = lambda b: pl.BlockSpec((tm, D), lambda i, b=b: (i, b))
    rows = pl.BlockSpec((tm, D), lambda i: (i, 0))
    vec = pl.BlockSpec((1, D), lambda i: (0, 0))
    once = dict(pipeline_mode=pl.Buffered(1))
    return pl.pallas_call(
        _merge_kernel,
        grid=(S // tm,),
        in_specs=[ycol, ycol, ycol, gate(0), gate(1), gate(2), rows, vec,
                  pl.BlockSpec((N_BRANCH, HEAD_W, D), lambda i: (0, 0, 0), **once),
                  pl.BlockSpec((D, D), lambda i: (0, 0), **once),
                  vec, vec, vec,
                  pl.BlockSpec((D, ROUTE_LANES), lambda i: (0, 0)),
                  pl.BlockSpec((1, ROUTE_LANES), lambda i: (0, 0))],
        out_specs=[rows, pl.BlockSpec((tm, D // LANES, LANES), lambda i: (i, 0, 0)),
                   pl.BlockSpec((tm, ROUTE_LANES), lambda i: (i, 0))],
        out_shape=[jax.ShapeDtypeStruct((S, D), F32), jax.ShapeDtypeStruct((S, D // LANES, LANES), F32),
                   jax.ShapeDtypeStruct((S, ROUTE_LANES), F32)],
        compiler_params=_params("parallel"),
        name="merge",
    )(y_ret, y_lru, y_gdn, gates, gates, gates, x, gt1, bp, w_out, norm2_g, sc2, sh2, w_route, b_route)


R_E0, R_E1, R_W0, R_W1, R_RANK0, R_RANK1 = range(6)


def _first_max(vals, lane):
    m = jnp.max(vals, axis=-1, keepdims=True)
    idx = jnp.min(jnp.where(vals == m, lane, ROUTE_LANES), axis=-1, keepdims=True)
    return m, idx


def _route_kernel(lg_ref, o_ref, cnt_ref, carry_ref):
    @pl.when(pl.program_id(0) == 0)
    def _():
        carry_ref[...] = jnp.zeros_like(carry_ref)

    lg = lg_ref[...]
    tm = lg.shape[0]
    lane = lax.broadcasted_iota(jnp.int32, lg.shape, 1)
    neg = jnp.float32(-jnp.inf)
    grp = jnp.where(lane < N_GROUPS, lg, neg)
    gmax, gsel = _first_max(grp, lane)
    pg_top = 1.0 / jnp.sum(jnp.exp(grp - gmax), axis=-1, keepdims=True)
    lo = N_GROUPS + gsel * EXPERTS_PER_GROUP
    inside = jnp.where((lane >= lo) & (lane < lo + EXPERTS_PER_GROUP), lg, neg)
    v0, i0 = _first_max(inside, lane)
    v1, i1 = _first_max(jnp.where(lane == i0, neg, inside), lane)
    e1w = jnp.exp(v1 - v0)
    w0 = pg_top / (1.0 + e1w)
    w1 = pg_top * e1w / (1.0 + e1w)
    hit0 = lane == i0
    hit1 = lane == i1
    hits = (hit0 | hit1).astype(BF16)
    row = lax.broadcasted_iota(jnp.int32, (tm, tm), 0)
    col = lax.broadcasted_iota(jnp.int32, (tm, tm), 1)
    before = jnp.dot((row > col).astype(BF16), hits, preferred_element_type=F32) + carry_ref[0:1, :]
    rank0 = jnp.sum(jnp.where(hit0, before, 0.0), axis=-1, keepdims=True)
    rank1 = jnp.sum(jnp.where(hit1, before, 0.0), axis=-1, keepdims=True)
    total = carry_ref[0:1, :] + jnp.sum(hits.astype(F32), axis=0, keepdims=True)
    carry_ref[0:1, :] = total
    cnt_ref[...] = jnp.broadcast_to(total, cnt_ref.shape)
    rec = jnp.zeros(lg.shape, F32)
    for slot, val in ((R_E0, (i0 - N_GROUPS).astype(F32)), (R_E1, (i1 - N_GROUPS).astype(F32)),
                      (R_W0, w0), (R_W1, w1), (R_RANK0, rank0), (R_RANK1, rank1)):
        rec = jnp.where(lane == slot, val, rec)
    o_ref[...] = rec


def route(logits):
    S = logits.shape[0]
    tm = min(256, S)
    return pl.pallas_call(
        _route_kernel,
        grid=(S // tm,),
        in_specs=[pl.BlockSpec((tm, ROUTE_LANES), lambda i: (i, 0))],
        out_specs=[pl.BlockSpec((tm, ROUTE_LANES), lambda i: (i, 0)),
                   pl.BlockSpec((SUBLANES, ROUTE_LANES), lambda i: (0, 0))],
        out_shape=[jax.ShapeDtypeStruct((S, ROUTE_LANES), F32),
                   jax.ShapeDtypeStruct((SUBLANES, ROUTE_LANES), F32)],
        scratch_shapes=[pltpu.VMEM((SUBLANES, ROUTE_LANES), F32)],
        compiler_params=_params("arbitrary"),
        name="route",
    )(logits)


def _expert_kernel(be_ref, nused_ref, src_ref, nsrc_ref, dst_ref, h_hbm, w1_ref, w3_ref, w2_ref,
                   y_hbm, xbuf, ybuf, wb1, wb3, wb2, gsem, ssem):
    i = pl.program_id(0)
    nused = nused_ref[0]
    B = EXPERT_ROWS
    T = h_hbm.shape[0]
    slot = i % 2

    def gather(idx_ref, s):
        for r in range(B):
            pltpu.make_async_copy(h_hbm.at[pl.ds(idx_ref[0, r], 1)], xbuf.at[s, pl.ds(r, 1)],
                                  gsem.at[s]).start()

    def scatter_wait(s):
        pltpu.make_async_copy(ybuf.at[s], y_hbm.at[pl.ds(0, B)], ssem.at[s]).wait()

    @pl.when(i == 0)
    def _():
        gather(src_ref, 0)
        ybuf[1] = jnp.zeros(ybuf.shape[1:], ybuf.dtype)
        for plane in range(2):
            fill = pltpu.make_async_copy(ybuf.at[1], y_hbm.at[pl.ds(plane * (T + B) + T, B)], ssem.at[1])
            fill.start()
            fill.wait()

    @pl.when(i < nused)
    def _():
        @pl.when(i + 1 < nused)
        def _():
            gather(nsrc_ref, 1 - slot)

        @pl.when((i == 0) | (be_ref[i] != be_ref[jnp.maximum(i - 1, 0)]))
        def _():
            wb1[...] = w1_ref[...].astype(BF16)
            wb3[...] = w3_ref[...].astype(BF16)
            wb2[...] = w2_ref[...].astype(BF16)

        pltpu.make_async_copy(h_hbm.at[pl.ds(0, B)], xbuf.at[slot], gsem.at[slot]).wait()
        x = _load_row_slabs(xbuf.at[slot]).astype(BF16)
        a1 = jnp.dot(x, wb1[...], preferred_element_type=F32)
        a3 = jnp.dot(x, wb3[...], preferred_element_type=F32)
        mid = (_silu(a1) * a3).astype(BF16)
        _store_row_slabs(ybuf.at[slot], jnp.dot(mid, wb2[...], preferred_element_type=F32))

        @pl.when(i >= 1)
        def _():
            scatter_wait(1 - slot)

        for r in range(B):
            pltpu.make_async_copy(ybuf.at[slot, pl.ds(r, 1)], y_hbm.at[pl.ds(dst_ref[0, r], 1)],
                                  ssem.at[slot]).start()

        @pl.when(i == nused - 1)
        def _():
            scatter_wait(slot)


def experts(h, block_e, nused, src_row, dst_row, w1, w3, w2, layer):
    T = h.shape[0]
    D = w1.shape[2]
    slab = h.shape[1:]
    nb = block_e.shape[0]
    B = EXPERT_ROWS
    idx = lambda f: pl.BlockSpec((None, 1, B), f, memory_space=pltpu.SMEM)
    wspec = lambda shp: pl.BlockSpec((None, None) + shp, lambda i, be, nu: (layer, be[i], 0, 0))
    grid_spec = pltpu.PrefetchScalarGridSpec(
        num_scalar_prefetch=2,
        grid=(nb,),
        in_specs=[idx(lambda i, be, nu: (i, 0, 0)),
                  idx(lambda i, be, nu: (jnp.minimum(i + 1, nb - 1), 0, 0)),
                  idx(lambda i, be, nu: (i, 0, 0)),
                  pl.BlockSpec(memory_space=pl.ANY),
                  wspec((D, D_EXPERT)), wspec((D, D_EXPERT)), wspec((D_EXPERT, D))],
        out_specs=pl.BlockSpec(memory_space=pl.ANY),
        scratch_shapes=[pltpu.VMEM((2, B) + slab, F32), pltpu.VMEM((2, B) + slab, F32),
                        pltpu.VMEM((D, D_EXPERT), BF16), pltpu.VMEM((D, D_EXPERT), BF16),
                        pltpu.VMEM((D_EXPERT, D), BF16),
                        pltpu.SemaphoreType.DMA((2,)), pltpu.SemaphoreType.DMA((2,))])
    src3 = src_row.reshape(nb, 1, B)
    return pl.pallas_call(
        _expert_kernel,
        grid_spec=grid_spec,
        out_shape=jax.ShapeDtypeStruct((2 * (T + B),) + slab, F32),
        compiler_params=_params("arbitrary"),
        name="experts",
    )(block_e, nused, src3, src3, dst_row.reshape(nb, 1, B), h, w1, w3, w2)


def dispatch_plan(rec, counts, T):
    B = EXPERT_ROWS
    A = 2 * T
    nb = A // B + N_EXPERTS
    cnt = counts[0, N_GROUPS:N_GROUPS + N_EXPERTS].astype(jnp.int32)
    padded = ((cnt + B - 1) // B) * B
    pend = jnp.cumsum(padded)
    pstart = pend - padded
    e = rec[:, R_E0:R_E1 + 1].astype(jnp.int32)
    rank = rec[:, R_RANK0:R_RANK1 + 1].astype(jnp.int32)
    dest = (pstart[e] + rank).reshape(-1)
    slot_a = jnp.full((nb * B,), -1, jnp.int32).at[dest].set(jnp.arange(A, dtype=jnp.int32))
    slots = jnp.arange(nb * B, dtype=jnp.int32)
    src_row = jnp.maximum(slot_a, 0) >> 1
    dump = (slots // B % 2) * (T + B) + T + slots % B
    dst_row = jnp.where(slot_a >= 0, (slot_a & 1) * (T + B) + src_row, dump)
    first = jnp.arange(nb, dtype=jnp.int32) * B
    block_e = jnp.minimum(jnp.sum((first[:, None] >= pend[None, :]).astype(jnp.int32), axis=1),
                          N_EXPERTS - 1)
    nused = (pend[-1] // B).astype(jnp.int32).reshape(1)
    return block_e, nused, src_row, dst_row


def _combine_kernel(final, x_ref, y0_ref, y1_ref, rec_ref, gt_ref, fg_ref, o_ref):
    rec = rec_ref[...]
    moe = rec[:, R_W0:R_W0 + 1] * _load_row_slabs(y0_ref) + rec[:, R_W1:R_W1 + 1] * _load_row_slabs(y1_ref)
    x = x_ref[...] + gt_ref[...] * moe
    o_ref[...] = _rms(x) * fg_ref[...] if final else x


def combine(x, y, rec, gt2, final_g, final):
    S, D = x.shape
    tm = min(512, S)
    y3 = y.reshape((2, y.shape[0] // 2) + y.shape[1:])
    rows = pl.BlockSpec((tm, D), lambda i: (i, 0))
    vec = pl.BlockSpec((1, D), lambda i: (0, 0))
    plane = lambda k: pl.BlockSpec((None, tm) + y.shape[1:], lambda i, k=k: (k, i, 0, 0))
    return pl.pallas_call(
        functools.partial(_combine_kernel, final),
        grid=(S // tm,),
        in_specs=[rows, plane(0), plane(1), pl.BlockSpec((tm, ROUTE_LANES), lambda i: (i, 0)), vec, vec],
        out_specs=rows,
        out_shape=jax.ShapeDtypeStruct((S, D), F32),
        compiler_params=_params("parallel"),
        name="combine",
    )(x, y3, y3, rec, gt2, final_g)


def _rope_tables(positions):
    half = RET_DK // 2
    inv_freq = ROPE_BASE ** (-jnp.arange(half, dtype=F32) * (2.0 / RET_DK))
    ang = positions.astype(F32)[:, None] * inv_freq
    cos = jnp.cos(ang)
    sin = jnp.sin(ang)
    return jnp.concatenate([cos, cos], axis=-1), jnp.concatenate([-sin, sin], axis=-1)


def _proj_weights(w_in):
    L, D, _ = w_in.shape
    w_all = w_in[:, :, :MAIN_COLS].astype(BF16)
    ab = w_in[:, :, MAIN_COLS:MAIN_COLS + N_AB]
    w_ab = jnp.concatenate([ab, jnp.zeros((L, D, LANES - N_AB), w_in.dtype)], axis=2).astype(BF16)
    w_gates = w_in[:, :, MAIN_COLS + N_AB:].astype(BF16)
    return w_all, w_ab, w_gates


def kernel(x, c, positions, ada_w, ada_b, norm1_g, norm2_g, w_in, lru_conv_w, lru_conv_b, lru_wa, lru_ba,
           lru_wx, lru_bx, lru_lambda, gdn_conv_w, gdn_a_log, gdn_dt_bias, gdn_norm_g, branch_proj, w_out,
           router_grp_w, router_grp_b, router_exp_w, router_exp_b, moe_w1, moe_w3, moe_w2, final_g):
    B, S, D = x.shape
    assert B == 1
    xs = x.reshape(S, D)
    mod = ada_mod(c, ada_w, ada_b)
    cos2, sin2 = _rope_tables(positions.reshape(S))
    row = lambda v: v.reshape(1, -1).astype(F32)
    w_all, w_ab, w_gates = _proj_weights(w_in)
    for l in range(DEPTH):
        sh1, sc1, gt1, sh2, sc2, gt2 = (row(m) for m in jnp.split(mod[l], 6))
        z, z_ab = norm_proj(xs, row(norm1_g[l]), sc1, sh1, w_all, w_ab, l)
        gates = gate_proj(xs, row(norm1_g[l]), sc1, sh1, w_gates, l)
        y_ret = retention(z, cos2, sin2)
        y_lru = rglru(z, lru_conv_w[l], lru_conv_b[l], lru_wa[l], lru_ba[l], lru_wx[l], lru_bx[l],
                      lru_lambda[l])
        y_gdn = gdn(z, z_ab, gdn_conv_w[l], gdn_a_log[l], gdn_dt_bias[l], gdn_norm_g[l])
        n_r = N_GROUPS + N_EXPERTS
        w_route = jnp.concatenate([router_grp_w[l], router_exp_w[l],
                                   jnp.zeros((D, ROUTE_LANES - n_r), F32)], axis=1)
        b_route = jnp.concatenate([router_grp_b[l], router_exp_b[l],
                                   jnp.zeros((ROUTE_LANES - n_r,), F32)]).reshape(1, ROUTE_LANES)
        xs, h2, logits = merge(y_ret, y_lru, y_gdn, gates, xs, gt1, branch_proj[l].astype(BF16),
                               w_out[l].astype(BF16), row(norm2_g[l]), sc2, sh2, w_route, b_route)
        rec, counts = route(logits)
        block_e, nused, src_row, dst_row = dispatch_plan(rec, counts, S)
        y = experts(h2, block_e, nused, src_row, dst_row, moe_w1, moe_w3, moe_w2, l)
        xs = combine(xs, y, rec, gt2, row(final_g), final=(l == DEPTH - 1))
    return xs.reshape(B, S, D)
```

```python
import functools
import math

import numpy as np
import jax
import jax.numpy as jnp
from jax import lax
from jax.experimental import pallas as pl
from jax.experimental.pallas import tpu as pltpu

F32 = jnp.float32
BF16 = jnp.bfloat16
HIGHEST = lax.Precision.HIGHEST

D_MODEL = 2048
DEPTH = 2
RET_HEADS = 8
RET_DK = 128
RET_CHUNK = 128
ROPE_BASE = 10000.0
LRU_W = 1024
LRU_BLOCKS = 8
LRU_BS = LRU_W // LRU_BLOCKS
LRU_C = 8.0
CONV_K = 4
GDN_HEADS = 8
GDN_DK = 128
GDN_CHUNK = 64
N_BRANCH = 3
N_GROUPS = 4
EXPERTS_PER_GROUP = 8
N_EXPERTS = N_GROUPS * EXPERTS_PER_GROUP
D_EXPERT = 512
EPS = 1e-6

HEAD_W = 1024
MAIN_COLS = 10 * HEAD_W
N_AB = 2 * GDN_HEADS
GATE_COLS = N_BRANCH * D_MODEL
PROJ_TN = 1280
GATE_TN = 1024
GATE_ROW_CHUNK = 256
LANES = 128
SUBLANES = 8
EXPERT_ROWS = 256
EXPERT_TN = 256
GDN_ROWS = 2 * GDN_CHUNK
RET_ROWS = 2 * RET_CHUNK
VMEM_LIMIT = 56 * 1024 * 1024

NT_DIMS = (((1,), (1,)), ((), ()))
TN_DIMS = (((0,), (0,)), ((), ()))


def _params(*sem):
    return pltpu.CompilerParams(dimension_semantics=sem, vmem_limit_bytes=VMEM_LIMIT)


def _dot32(a, b):
    return jnp.dot(a, b, preferred_element_type=F32, precision=HIGHEST)


def _dotg32(a, b, dims):
    return lax.dot_general(a, b, dims, preferred_element_type=F32, precision=HIGHEST)


def _dot(a, b):
    return jnp.dot(a.astype(BF16), b.astype(BF16), preferred_element_type=F32)


def _dotg(a, b, dims):
    return lax.dot_general(a.astype(BF16), b.astype(BF16), dims, preferred_element_type=F32)


def _dot3(a, b):
    ah = a.astype(BF16)
    bh = b.astype(BF16)
    al = (a - ah.astype(F32)).astype(BF16)
    bl = (b - bh.astype(F32)).astype(BF16)
    d = functools.partial(jnp.dot, preferred_element_type=F32)
    return d(ah, bh) + (d(ah, bl) + d(al, bh))


def _silu(x):
    return x * jax.nn.sigmoid(x)


def _softplus(x):
    return jnp.maximum(x, 0.0) + jnp.log1p(jnp.exp(-jnp.abs(x)))


def _rms(x):
    return x * lax.rsqrt(jnp.mean(x * x, axis=-1, keepdims=True) + EPS)


def _shift_rows(x, s, fill):
    n = x.shape[0]
    if s % SUBLANES == 0:
        return jnp.concatenate([jnp.full((s, x.shape[1]), fill, x.dtype), x[:n - s]], axis=0)
    row = lax.broadcasted_iota(jnp.int32, x.shape, 0)
    return jnp.where(row < s, fill, pltpu.roll(x, s, 0))


def _causal_conv(x, tail, w):
    n = x.shape[0]
    row = lax.broadcasted_iota(jnp.int32, (SUBLANES, x.shape[1]), 0)
    y = x * w[CONV_K - 1:CONV_K]
    for s in range(1, CONV_K):
        xs = pltpu.roll(x, s, 0)
        head = jnp.where(row < s, pltpu.roll(tail, s, 0), xs[:SUBLANES])
        xs = jnp.concatenate([head, xs[SUBLANES:]], axis=0) if n > SUBLANES else head
        y = y + xs * w[CONV_K - 1 - s:CONV_K - s]
    return y


def _mod_kernel(c_ref, w_ref, b_ref, o_ref):
    c = c_ref[...]
    o_ref[...] = _dot32(_silu(c), w_ref[...]) + b_ref[...]


def ada_mod(c, ada_w, ada_b):
    L, D, N = ada_w.shape
    tn = 1536
    c8 = jnp.broadcast_to(c.astype(F32), (SUBLANES, D))
    out = pl.pallas_call(
        _mod_kernel,
        grid=(L, N // tn),
        in_specs=[pl.BlockSpec((SUBLANES, D), lambda l, j: (0, 0)),
                  pl.BlockSpec((None, D, tn), lambda l, j: (l, 0, j)),
                  pl.BlockSpec((None, 1, tn), lambda l, j: (l, 0, j))],
        out_specs=pl.BlockSpec((None, SUBLANES, tn), lambda l, j: (l, 0, j)),
        out_shape=jax.ShapeDtypeStruct((L, SUBLANES, N), F32),
        compiler_params=_params("parallel", "parallel"),
        name="ada_mod",
    )(c8, ada_w, ada_b.reshape(L, 1, N))
    return out[:, 0, :]


def _norm_proj_kernel(x_ref, g_ref, sc_ref, sh_ref, w_ref, wab_ref, o_ref, ab_ref, h_ref):
    @pl.when(pl.program_id(1) == 0)
    def _():
        h = (_rms(x_ref[...]) * g_ref[...]) * (1.0 + sc_ref[...]) + sh_ref[...]
        h_ref[...] = h.astype(BF16)
        ab_ref[...] = jnp.dot(h_ref[...], wab_ref[...], preferred_element_type=F32)

    o_ref[...] = jnp.dot(h_ref[...], w_ref[...], preferred_element_type=F32)


def norm_proj(x, g, sc, sh, w, w_ab, layer):
    S, D = x.shape
    tm = min(1024, S)
    vec = pl.BlockSpec((1, D), lambda i, j: (0, 0))
    return pl.pallas_call(
        _norm_proj_kernel,
        grid=(S // tm, MAIN_COLS // PROJ_TN),
        in_specs=[pl.BlockSpec((tm, D), lambda i, j: (i, 0)), vec, vec, vec,
                  pl.BlockSpec((None, D, PROJ_TN), lambda i, j: (layer, 0, j)),
                  pl.BlockSpec((None, D, LANES), lambda i, j: (layer, 0, 0))],
        out_specs=[pl.BlockSpec((tm, PROJ_TN), lambda i, j: (i, j)),
                   pl.BlockSpec((tm, LANES), lambda i, j: (i, 0))],
        out_shape=[jax.ShapeDtypeStruct((S, MAIN_COLS), F32), jax.ShapeDtypeStruct((S, LANES), F32)],
        scratch_shapes=[pltpu.VMEM((tm, D), BF16)],
        compiler_params=_params("parallel", "arbitrary"),
        name="norm_proj",
    )(x, g, sc, sh, w, w_ab)


def _gate_proj_kernel(x_ref, g_ref, sc_ref, sh_ref, w_ref, o_ref, h_ref):
    @pl.when(pl.program_id(1) == 0)
    def _():
        h = (_rms(x_ref[...]) * g_ref[...]) * (1.0 + sc_ref[...]) + sh_ref[...]
        h_ref[...] = h.astype(BF16)

    rows = GATE_ROW_CHUNK
    for r in range(h_ref.shape[0] // rows):
        rs = slice(r * rows, (r + 1) * rows)
        o_ref[rs] = jax.nn.sigmoid(jnp.dot(h_ref[rs], w_ref[...], preferred_element_type=F32)).astype(BF16)


def gate_proj(x, g, sc, sh, w_gates, layer):
    S, D = x.shape
    tm = min(1024, S)
    tn = GATE_TN
    vec = pl.BlockSpec((1, D), lambda i, j: (0, 0))
    return pl.pallas_call(
        _gate_proj_kernel,
        grid=(S // tm, GATE_COLS // tn),
        in_specs=[pl.BlockSpec((tm, D), lambda i, j: (i, 0)), vec, vec, vec,
                  pl.BlockSpec((None, D, tn), lambda i, j: (layer, 0, j))],
        out_specs=pl.BlockSpec((tm, tn), lambda i, j: (i, j)),
        out_shape=jax.ShapeDtypeStruct((S, GATE_COLS), BF16),
        scratch_shapes=[pltpu.VMEM((tm, D), BF16)],
        compiler_params=_params("parallel", "arbitrary"),
        name="gate_proj",
    )(x, g, sc, sh, w_gates)


def _ret_tables():
    H, C = RET_HEADS, RET_CHUNK
    log_g = np.log1p(-np.exp2(-5.0 - np.arange(H, dtype=np.float64)))
    pos = np.arange(C, dtype=np.float64)
    rel = pos[:, None] - pos[None, :]
    intra = np.where(rel >= 0, np.exp(log_g[:, None, None] * np.maximum(rel, 0.0)), 0.0)
    qd = np.exp(log_g[:, None] * (pos + 1.0))
    kd = np.exp(log_g[:, None] * (C - 1.0 - pos))
    ones = np.ones((1, 1, LANES))
    cd = [float(np.exp(lg * C)) for lg in log_g]
    return (jnp.asarray(intra, F32), jnp.asarray(qd[:, :, None] * ones, F32),
            jnp.asarray(kd[:, :, None] * ones, F32), cd)


def _ret_kernel(cd, q_ref, k_ref, v_ref, g_ref, cos_ref, sin_ref, intra_ref, qd_ref, kd_ref,
                o_ref, state_ref):
    @pl.when(pl.program_id(0) == 0)
    def _():
        state_ref[...] = jnp.zeros_like(state_ref)

    C = RET_CHUNK
    half = RET_DK // 2
    for c in range(q_ref.shape[0] // C):
        rs = slice(c * C, (c + 1) * C)
        cos2 = cos_ref[rs]
        sin2 = sin_ref[rs]
        for h in range(RET_HEADS):
            sl = slice(h * RET_DK, (h + 1) * RET_DK)
            q = q_ref[rs, sl]
            k = k_ref[rs, sl]
            v = v_ref[rs, sl]
            q = q * cos2 + pltpu.roll(q, half, 1) * sin2
            k = (k * cos2 + pltpu.roll(k, half, 1) * sin2) * (RET_DK ** -0.5)
            scores = _dotg(q, k, NT_DIMS) * intra_ref[h]
            state = state_ref[h]
            o = _dot(scores, v) + _dot(q * qd_ref[h], state)
            state_ref[h] = state * cd[h] + _dotg(k * kd_ref[h], v, TN_DIMS)
            o_ref[rs, sl] = (_rms(o) * _silu(g_ref[rs, sl])).astype(o_ref.dtype)


def retention(z, cos2, sin2):
    S = z.shape[0]
    C = min(RET_ROWS, S)
    intra, qd, kd, cd = _ret_tables()
    col = lambda b: pl.BlockSpec((C, HEAD_W), lambda n, b=b: (n, b))
    tab = lambda a: pl.BlockSpec(a.shape, lambda n: (0, 0, 0))
    rot = pl.BlockSpec((C, RET_DK), lambda n: (n, 0))
    return pl.pallas_call(
        functools.partial(_ret_kernel, cd),
        grid=(S // C,),
        in_specs=[col(0), col(1), col(2), col(3), rot, rot, tab(intra), tab(qd), tab(kd)],
        out_specs=pl.BlockSpec((C, HEAD_W), lambda n: (n, 0)),
        out_shape=jax.ShapeDtypeStruct((S, HEAD_W), BF16),
        scratch_shapes=[pltpu.VMEM((RET_HEADS, RET_DK, RET_DK), F32)],
        compiler_params=_params("arbitrary"),
        name="retention",
    )(z, z, z, z, cos2, sin2, intra, qd, kd)


def _lru_kernel(x_ref, gate_ref, cw_ref, cb_ref, wa_ref, ba_ref, wx_ref, bx_ref, lam_ref,
                o_ref, tail_ref, carry_ref):
    @pl.when(pl.program_id(0) == 0)
    def _():
        tail_ref[...] = jnp.zeros_like(tail_ref)
        carry_ref[...] = jnp.zeros_like(carry_ref)

    tm = x_ref.shape[0]
    for j in range(LRU_BLOCKS):
        sl = slice(j * LRU_BS, (j + 1) * LRU_BS)
        x = x_ref[:, sl]
        xc = _causal_conv(x, tail_ref[:, sl], cw_ref[:, sl]) + cb_ref[:, sl]
        tail_ref[:, sl] = x[tm - SUBLANES:]
        r = jax.nn.sigmoid(_dot(xc, wa_ref[j]) + ba_ref[:, sl])
        i = jax.nn.sigmoid(_dot(xc, wx_ref[j]) + bx_ref[:, sl])
        log_a = (-LRU_C * r) * _softplus(-lam_ref[:, sl])
        a = jnp.exp(log_a)
        t = jnp.tanh(log_a)
        b = jnp.sqrt(-2.0 * t / (1.0 - t)) * (i * xc)
        s = 1
        while s < tm:
            b = b + a * _shift_rows(b, s, 0.0)
            a = a * _shift_rows(a, s, 1.0)
            s *= 2
        h = b + a * carry_ref[0:1, sl]
        carry_ref[0:1, sl] = h[tm - 1:tm]
        g = gate_ref[:, sl]
        gelu = g * (0.5 * (1.0 + jnp.tanh(math.sqrt(2.0 / math.pi) * (g + 0.044715 * (g * g * g)))))
        o_ref[:, sl] = (h * gelu).astype(o_ref.dtype)


def rglru(z, conv_w, conv_b, wa, ba, wx, bx, lam):
    S = z.shape[0]
    tm = min(256, S)
    row = lambda a: a.reshape(1, LRU_W)
    vec = pl.BlockSpec((1, LRU_W), lambda n: (0, 0))
    wspec = pl.BlockSpec((LRU_BLOCKS, LRU_BS, LRU_BS), lambda n: (0, 0, 0))
    return pl.pallas_call(
        _lru_kernel,
        grid=(S // tm,),
        in_specs=[pl.BlockSpec((tm, HEAD_W), lambda n: (n, 4)),
                  pl.BlockSpec((tm, HEAD_W), lambda n: (n, 5)),
                  pl.BlockSpec((CONV_K, LRU_W), lambda n: (0, 0)), vec, wspec, vec, wspec, vec, vec],
        out_specs=pl.BlockSpec((tm, HEAD_W), lambda n: (n, 0)),
        out_shape=jax.ShapeDtypeStruct((S, HEAD_W), BF16),
        scratch_shapes=[pltpu.VMEM((SUBLANES, LRU_W), F32), pltpu.VMEM((SUBLANES, LRU_W), F32)],
        compiler_params=_params("arbitrary"),
        name="rglru",
    )(z, z, conv_w, row(conv_b), wa, row(ba), wx, row(bx), row(lam))


def _gdn_tables():
    C = GDN_CHUNK
    i = np.arange(C)[:, None]
    j = np.arange(C)[None, :]
    levels = []
    s = 1
    while s < C:
        levels.append((i // (2 * s) == j // (2 * s)) & (i % (2 * s) >= s) & (j % (2 * s) < s))
        s *= 2
    return jnp.asarray(np.stack(levels), F32), jnp.asarray(i >= j, F32), jnp.asarray(i <= j, F32)


def _gdn_kernel(q_ref, k_ref, v_ref, z_ref, ab_ref, cw_ref, alog_ref, dtb_ref, ng_ref, lvl_ref, tri_ref,
                trit_ref, o_ref, tail_ref, state_ref):
    @pl.when(pl.program_id(0) == 0)
    def _():
        tail_ref[...] = jnp.zeros_like(tail_ref)
        state_ref[...] = jnp.zeros_like(state_ref)

    C = GDN_CHUNK
    H = GDN_HEADS
    DK = GDN_DK
    rows = q_ref.shape[0]
    incl = tri_ref[...]
    rowi = lax.broadcasted_iota(jnp.int32, (C, C), 0)
    coli = lax.broadcasted_iota(jnp.int32, (C, C), 1)
    eye = (rowi == coli).astype(F32)

    ab = ab_ref[...]
    g_all = -jnp.exp(alog_ref[...]) * _softplus(ab + dtb_ref[...])
    beta_all = jax.nn.sigmoid(ab)

    conv = []
    for p, ref in enumerate((q_ref, k_ref, v_ref)):
        x = ref[...]
        conv.append(_silu(_causal_conv(x, tail_ref[p], cw_ref[p])))
        tail_ref[p] = x[rows - SUBLANES:]
    qc, kc, vc = conv

    pairs = [(c, h) for c in range(rows // C) for h in range(H)]
    st = {}
    for c in range(rows // C):
        g_c = g_all[c * C:(c + 1) * C]
        st[c] = (_dot32(incl, g_c),
                 _dotg32(g_c, trit_ref[...], TN_DIMS))
    for c, h in pairs:
        rs = slice(c * C, (c + 1) * C)
        sl = slice(h * DK, (h + 1) * DK)
        q = qc[rs, sl]
        k = kc[rs, sl]
        q = q * lax.rsqrt(jnp.sum(q * q, axis=-1, keepdims=True) + EPS) * (DK ** -0.5)
        k = k * lax.rsqrt(jnp.sum(k * k, axis=-1, keepdims=True) + EPS)
        beta = beta_all[rs, H + h:H + h + 1]
        gc = st[c][0][:, h:h + 1]
        gcr = st[c][1][h:h + 1, :]
        decay = incl * jnp.exp(jnp.where(incl > 0, gc - gcr, 0.0))
        kb = k * beta
        egc = jnp.exp(gc)
        gl = gc[C - 1:C]
        qk = _dotg(jnp.concatenate([q, kb], axis=0), k, NT_DIMS)
        st[c, h] = dict(
            attn=qk[:C] * decay,
            m=qk[C:] * (decay - eye),
            rhs=jnp.concatenate([vc[rs, sl] * beta, kb * egc], axis=1),
            qdec=q * egc,
            kdt=(k * jnp.exp(gl - gc)).T,
            egl=jnp.exp(gl))
    for p in pairs:
        st[p]['inv'] = eye - st[p]['m'] * lvl_ref[0]
    for lv in range(1, lvl_ref.shape[0]):
        for p in pairs:
            st[p]['t'] = _dot(st[p]['m'] * lvl_ref[lv], st[p]['inv'])
        for p in pairs:
            st[p]['inv'] = st[p]['inv'] - _dot(st[p]['inv'], st[p]['t'])
    for p in pairs:
        st[p]['uw'] = _dot(st[p]['inv'], st[p]['rhs'])
    for c, h in pairs:
        d = st[c, h]
        state = state_ref[h]
        ws = _dot(jnp.concatenate([d['uw'][:, DK:], d['qdec']], axis=0), state)
        v_new = d['uw'][:, :DK] - ws[:C]
        o = ws[C:] + _dot(d['attn'], v_new)
        state_ref[h] = state * d['egl'] + _dot(d['kdt'], v_new)
        rs = slice(c * C, (c + 1) * C)
        sl = slice(h * DK, (h + 1) * DK)
        o_ref[rs, sl] = (_rms(o) * ng_ref[...] * _silu(z_ref[rs, sl])).astype(o_ref.dtype)


def gdn(z, z_ab, conv_w, a_log, dt_bias, norm_g):
    S = z.shape[0]
    C = GDN_CHUNK
    H = GDN_HEADS
    lvl, tri, trit = _gdn_tables()
    alog = jnp.zeros((1, LANES), F32).at[0, :H].set(a_log)
    dtb = jnp.zeros((1, LANES), F32).at[0, :H].set(dt_bias)
    cw = conv_w.reshape(CONV_K, 3, HEAD_W).transpose(1, 0, 2)
    tm = min(GDN_ROWS, S)
    col = lambda b: pl.BlockSpec((tm, HEAD_W), lambda n, b=b: (n, b))
    lane = pl.BlockSpec((1, LANES), lambda n: (0, 0))
    return pl.pallas_call(
        _gdn_kernel,
        grid=(S // tm,),
        in_specs=[col(6), col(7), col(8), col(9),
                  pl.BlockSpec((tm, LANES), lambda n: (n, 0)),
                  pl.BlockSpec((3, CONV_K, HEAD_W), lambda n: (0, 0, 0)),
                  lane, lane, lane,
                  pl.BlockSpec(lvl.shape, lambda n: (0, 0, 0)),
                  pl.BlockSpec(tri.shape, lambda n: (0, 0)),
                  pl.BlockSpec(trit.shape, lambda n: (0, 0))],
        out_specs=pl.BlockSpec((tm, HEAD_W), lambda n: (n, 0)),
        out_shape=jax.ShapeDtypeStruct((S, HEAD_W), BF16),
        scratch_shapes=[pltpu.VMEM((3, SUBLANES, HEAD_W), F32),
                        pltpu.VMEM((H, GDN_DK, GDN_DK), F32)],
        compiler_params=_params("arbitrary"),
        name="gdn",
    )(z, z, z, z, z_ab, cw, alog, dtb, norm_g.reshape(1, GDN_DK), lvl, tri, trit)


ROUTE_LANES = LANES


def _merge_kernel(yr_ref, yl_ref, yg_ref, g0_ref, g1_ref, g2_ref, x_ref, gt_ref, bp_ref, wo_ref,
                  ng_ref, sc_ref, sh_ref, wr_ref, br_ref, xo_ref, h_ref, lg_ref):
    merged = g0_ref[...] * jnp.dot(yr_ref[...], bp_ref[0], preferred_element_type=F32)
    merged += g1_ref[...] * jnp.dot(yl_ref[...], bp_ref[1], preferred_element_type=F32)
    merged += g2_ref[...] * jnp.dot(yg_ref[...], bp_ref[2], preferred_element_type=F32)
    out = jnp.dot(merged.astype(BF16), wo_ref[...], preferred_element_type=F32)
    x = x_ref[...] + gt_ref[...] * out
    xo_ref[...] = x
    h = (_rms(x) * ng_ref[...]) * (1.0 + sc_ref[...]) + sh_ref[...]
    h_ref[...] = h
    lg_ref[...] = _dot3(h, wr_ref[...]) + br_ref[...]


def merge(y_ret, y_lru, y_gdn, gates, x, gt1, bp, w_out, norm2_g, sc2, sh2, w_route, b_route):
    S, D = x.shape
    tm = min(256, S)
    ycol = pl.BlockSpec((tm, HEAD_W), lambda i: (i, 0))
    gate = lambda b: pl.BlockSpec((tm, D), lambda i, b=b: (i, b))
    rows = pl.BlockSpec((tm, D), lambda i: (i, 0))
    vec = pl.BlockSpec((1, D), lambda i: (0, 0))
    once = dict(pipeline_mode=pl.Buffered(1))
    return pl.pallas_call(
        _merge_kernel,
        grid=(S // tm,),
        in_specs=[ycol, ycol, ycol, gate(0), gate(1), gate(2), rows, vec,
                  pl.BlockSpec((N_BRANCH, HEAD_W, D), lambda i: (0, 0, 0), **once),
                  pl.BlockSpec((D, D), lambda i: (0, 0), **once),
                  vec, vec, vec,
                  pl.BlockSpec((D, ROUTE_LANES), lambda i: (0, 0)),
                  pl.BlockSpec((1, ROUTE_LANES), lambda i: (0, 0))],
        out_specs=[rows, rows, pl.BlockSpec((tm, ROUTE_LANES), lambda i: (i, 0))],
        out_shape=[jax.ShapeDtypeStruct((S, D), F32), jax.ShapeDtypeStruct((S, D), F32),
                   jax.ShapeDtypeStruct((S, ROUTE_LANES), F32)],
        compiler_params=_params("parallel"),
        name="merge",
    )(y_ret, y_lru, y_gdn, gates, gates, gates, x, gt1, bp, w_out, norm2_g, sc2, sh2, w_route, b_route)


R_E0, R_E1, R_W0, R_W1, R_RANK0, R_RANK1 = range(6)


def _first_max(vals, lane):
    m = jnp.max(vals, axis=-1, keepdims=True)
    idx = jnp.min(jnp.where(vals == m, lane, ROUTE_LANES), axis=-1, keepdims=True)
    return m, idx


def _route_kernel(lg_ref, o_ref, cnt_ref, carry_ref):
    @pl.when(pl.program_id(0) == 0)
    def _():
        carry_ref[...] = jnp.zeros_like(carry_ref)

    lg = lg_ref[...]
    tm = lg.shape[0]
    lane = lax.broadcasted_iota(jnp.int32, lg.shape, 1)
    neg = jnp.float32(-jnp.inf)
    grp = jnp.where(lane < N_GROUPS, lg, neg)
    gmax, gsel = _first_max(grp, lane)
    pg_top = 1.0 / jnp.sum(jnp.exp(grp - gmax), axis=-1, keepdims=True)
    lo = N_GROUPS + gsel * EXPERTS_PER_GROUP
    inside = jnp.where((lane >= lo) & (lane < lo + EXPERTS_PER_GROUP), lg, neg)
    v0, i0 = _first_max(inside, lane)
    v1, i1 = _first_max(jnp.where(lane == i0, neg, inside), lane)
    e1w = jnp.exp(v1 - v0)
    w0 = pg_top / (1.0 + e1w)
    w1 = pg_top * e1w / (1.0 + e1w)
    hit0 = lane == i0
    hit1 = lane == i1
    hits = (hit0 | hit1).astype(BF16)
    row = lax.broadcasted_iota(jnp.int32, (tm, tm), 0)
    col = lax.broadcasted_iota(jnp.int32, (tm, tm), 1)
    before = jnp.dot((row > col).astype(BF16), hits, preferred_element_type=F32) + carry_ref[0:1, :]
    rank0 = jnp.sum(jnp.where(hit0, before, 0.0), axis=-1, keepdims=True)
    rank1 = jnp.sum(jnp.where(hit1, before, 0.0), axis=-1, keepdims=True)
    total = carry_ref[0:1, :] + jnp.sum(hits.astype(F32), axis=0, keepdims=True)
    carry_ref[0:1, :] = total
    cnt_ref[...] = jnp.broadcast_to(total, cnt_ref.shape)
    rec = jnp.zeros(lg.shape, F32)
    for slot, val in ((R_E0, (i0 - N_GROUPS).astype(F32)), (R_E1, (i1 - N_GROUPS).astype(F32)),
                      (R_W0, w0), (R_W1, w1), (R_RANK0, rank0), (R_RANK1, rank1)):
        rec = jnp.where(lane == slot, val, rec)
    o_ref[...] = rec


def route(logits):
    S = logits.shape[0]
    tm = min(256, S)
    return pl.pallas_call(
        _route_kernel,
        grid=(S // tm,),
        in_specs=[pl.BlockSpec((tm, ROUTE_LANES), lambda i: (i, 0))],
        out_specs=[pl.BlockSpec((tm, ROUTE_LANES), lambda i: (i, 0)),
                   pl.BlockSpec((SUBLANES, ROUTE_LANES), lambda i: (0, 0))],
        out_shape=[jax.ShapeDtypeStruct((S, ROUTE_LANES), F32),
                   jax.ShapeDtypeStruct((SUBLANES, ROUTE_LANES), F32)],
        scratch_shapes=[pltpu.VMEM((SUBLANES, ROUTE_LANES), F32)],
        compiler_params=_params("arbitrary"),
        name="route",
    )(logits)


def _expert_kernel(be_ref, nused_ref, src_ref, nsrc_ref, pdst_ref, dst_ref, h_hbm, w1_ref, w3_ref, w2_ref,
                   y_hbm, xbuf, ybuf, wb1, wb3, wb2, gsem, ssem):
    i = pl.program_id(0)
    nused = nused_ref[0]
    B = EXPERT_ROWS
    T, D = h_hbm.shape
    slot = i % 2

    def get(idx_ref, s, r):
        pltpu.make_async_copy(h_hbm.at[pl.ds(idx_ref[0, r], 1)], xbuf.at[s, pl.ds(r, 1)],
                              gsem.at[s]).start(priority=r % 2)

    def put(idx_ref, s, r):
        pltpu.make_async_copy(ybuf.at[s, pl.ds(r, 1)], y_hbm.at[pl.ds(idx_ref[0, r], 1)],
                              ssem.at[s]).start(priority=r % 2)

    def gather_wait(s):
        pltpu.make_async_copy(h_hbm.at[pl.ds(0, B)], xbuf.at[s], gsem.at[s]).wait()

    def scatter_wait(s):
        pltpu.make_async_copy(ybuf.at[s], y_hbm.at[pl.ds(0, B)], ssem.at[s]).wait()

    @pl.when(i == 0)
    def _():
        for r in range(B):
            get(src_ref, 0, r)
        ybuf[1] = jnp.zeros(ybuf.shape[1:], ybuf.dtype)
        fill = pltpu.make_async_copy(ybuf.at[1], y_hbm.at[pl.ds(T, B)], ssem.at[1])
        fill.start()
        fill.wait()

    @pl.when(i < nused)
    def _():
        @pl.when(i >= 1)
        def _():
            scatter_wait(slot)

        @pl.when((i == 0) | (be_ref[i] != be_ref[jnp.maximum(i - 1, 0)]))
        def _():
            wb1[...] = w1_ref[...].astype(BF16)
            wb3[...] = w3_ref[...].astype(BF16)
            wb2[...] = w2_ref[...].astype(BF16)

        gather_wait(slot)
        x = xbuf[slot].astype(BF16)

        rows = iter(range(B))

        def issue(n):
            for _ in range(n):
                r = next(rows)
                get(nsrc_ref, 1 - slot, r)
                put(pdst_ref, 1 - slot, r)

        tn = EXPERT_TN
        up_chunks = D_EXPERT // tn
        down_chunks = D // tn
        up = []
        for w_ref in (wb1, wb3):
            parts = []
            for c in range(up_chunks):
                issue(B // 2 // (2 * up_chunks))
                parts.append(jnp.dot(x, w_ref[:, c * tn:(c + 1) * tn], preferred_element_type=F32))
            up.append(jnp.concatenate(parts, axis=1))
        mid = (_silu(up[0]) * up[1]).astype(BF16)
        for c in range(down_chunks):
            issue(B // 2 // down_chunks)
            ybuf[slot, :, c * tn:(c + 1) * tn] = jnp.dot(mid, wb2[:, c * tn:(c + 1) * tn],
                                                         preferred_element_type=F32)

        @pl.when(i == nused - 1)
        def _():
            scatter_wait(1 - slot)
            for r in range(B):
                put(dst_ref, slot, r)
            scatter_wait(slot)
            gather_wait(1 - slot)


def experts(h, block_e, nused, src_row, dst_row, w1, w3, w2, layer):
    T, D = h.shape
    nb = block_e.shape[0]
    B = EXPERT_ROWS
    idx = lambda f: pl.BlockSpec((None, 1, B), f, memory_space=pltpu.SMEM)
    wspec = lambda shp: pl.BlockSpec((None, None) + shp, lambda i, be, nu: (layer, be[i], 0, 0))
    grid_spec = pltpu.PrefetchScalarGridSpec(
        num_scalar_prefetch=2,
        grid=(nb,),
        in_specs=[idx(lambda i, be, nu: (i, 0, 0)),
                  idx(lambda i, be, nu: (jnp.minimum(i + 1, nb - 1), 0, 0)),
                  idx(lambda i, be, nu: (i, 0, 0)),
                  idx(lambda i, be, nu: (i, 0, 0)),
                  pl.BlockSpec(memory_space=pl.ANY),
                  wspec((D, D_EXPERT)), wspec((D, D_EXPERT)), wspec((D_EXPERT, D))],
        out_specs=pl.BlockSpec(memory_space=pl.ANY),
        scratch_shapes=[pltpu.VMEM((2, B, D), F32), pltpu.VMEM((2, B, D), F32),
                        pltpu.VMEM((D, D_EXPERT), BF16), pltpu.VMEM((D, D_EXPERT), BF16),
                        pltpu.VMEM((D_EXPERT, D), BF16),
                        pltpu.SemaphoreType.DMA((2,)), pltpu.SemaphoreType.DMA((2,))])
    src3 = src_row.reshape(nb, 1, B)
    dst3 = dst_row.reshape(nb, 1, B)
    dump1 = (2 * T + B + jnp.arange(B, dtype=jnp.int32)).reshape(1, 1, B)
    pdst3 = jnp.concatenate([dump1, dst3], axis=0)
    return pl.pallas_call(
        _expert_kernel,
        grid_spec=grid_spec,
        out_shape=jax.ShapeDtypeStruct((2 * (T + B), D), F32),
        compiler_params=_params("arbitrary"),
        name="experts",
    )(block_e, nused, src3, src3, pdst3, dst3, h, w1, w3, w2)


def dispatch_plan(rec, counts, T):
    B = EXPERT_ROWS
    A = 2 * T
    nb = A // B + N_EXPERTS
    cnt = counts[0, N_GROUPS:N_GROUPS + N_EXPERTS].astype(jnp.int32)
    padded = ((cnt + B - 1) // B) * B
    pend = jnp.cumsum(padded)
    pstart = pend - padded
    e = rec[:, R_E0:R_E1 + 1].astype(jnp.int32)
    rank = rec[:, R_RANK0:R_RANK1 + 1].astype(jnp.int32)
    dest = (pstart[e] + rank).reshape(-1)
    slot_a = jnp.full((nb * B,), -1, jnp.int32).at[dest].set(jnp.arange(A, dtype=jnp.int32))
    slots = jnp.arange(nb * B, dtype=jnp.int32)
    src_row = jnp.maximum(slot_a, 0) >> 1
    dump = (slots // B % 2) * (T + B) + T + slots % B
    dst_row = jnp.where(slot_a >= 0, (slot_a & 1) * (T + B) + src_row, dump)
    first = jnp.arange(nb, dtype=jnp.int32) * B
    block_e = jnp.minimum(jnp.sum((first[:, None] >= pend[None, :]).astype(jnp.int32), axis=1),
                          N_EXPERTS - 1)
    nused = (pend[-1] // B).astype(jnp.int32).reshape(1)
    return block_e, nused, src_row, dst_row


def _combine_kernel(final, x_ref, y0_ref, y1_ref, rec_ref, gt_ref, fg_ref, o_ref):
    rec = rec_ref[...]
    moe = rec[:, R_W0:R_W0 + 1] * y0_ref[...] + rec[:, R_W1:R_W1 + 1] * y1_ref[...]
    x = x_ref[...] + gt_ref[...] * moe
    o_ref[...] = _rms(x) * fg_ref[...] if final else x


def combine(x, y, rec, gt2, final_g, final):
    S, D = x.shape
    tm = min(512, S)
    y3 = y.reshape(2, y.shape[0] // 2, D)
    rows = pl.BlockSpec((tm, D), lambda i: (i, 0))
    vec = pl.BlockSpec((1, D), lambda i: (0, 0))
    plane = lambda k: pl.BlockSpec((None, tm, D), lambda i, k=k: (k, i, 0))
    return pl.pallas_call(
        functools.partial(_combine_kernel, final),
        grid=(S // tm,),
        in_specs=[rows, plane(0), plane(1), pl.BlockSpec((tm, ROUTE_LANES), lambda i: (i, 0)), vec, vec],
        out_specs=rows,
        out_shape=jax.ShapeDtypeStruct((S, D), F32),
        compiler_params=_params("parallel"),
        name="combine",
    )(x, y3, y3, rec, gt2, final_g)


def _rope_tables(positions):
    half = RET_DK // 2
    inv_freq = ROPE_BASE ** (-jnp.arange(half, dtype=F32) * (2.0 / RET_DK))
    ang = positions.astype(F32)[:, None] * inv_freq
    cos = jnp.cos(ang)
    sin = jnp.sin(ang)
    return jnp.concatenate([cos, cos], axis=-1), jnp.concatenate([-sin, sin], axis=-1)


def _proj_weights(w_in):
    L, D, _ = w_in.shape
    w_all = w_in[:, :, :MAIN_COLS].astype(BF16)
    ab = w_in[:, :, MAIN_COLS:MAIN_COLS + N_AB]
    w_ab = jnp.concatenate([ab, jnp.zeros((L, D, LANES - N_AB), w_in.dtype)], axis=2).astype(BF16)
    w_gates = w_in[:, :, MAIN_COLS + N_AB:].astype(BF16)
    return w_all, w_ab, w_gates


def kernel(x, c, positions, ada_w, ada_b, norm1_g, norm2_g, w_in, lru_conv_w, lru_conv_b, lru_wa, lru_ba,
           lru_wx, lru_bx, lru_lambda, gdn_conv_w, gdn_a_log, gdn_dt_bias, gdn_norm_g, branch_proj, w_out,
           router_grp_w, router_grp_b, router_exp_w, router_exp_b, moe_w1, moe_w3, moe_w2, final_g):
    B, S, D = x.shape
    assert B == 1
    xs = x.reshape(S, D)
    mod = ada_mod(c, ada_w, ada_b)
    cos2, sin2 = _rope_tables(positions.reshape(S))
    row = lambda v: v.reshape(1, -1).astype(F32)
    w_all, w_ab, w_gates = _proj_weights(w_in)
    for l in range(DEPTH):
        sh1, sc1, gt1, sh2, sc2, gt2 = (row(m) for m in jnp.split(mod[l], 6))
        z, z_ab = norm_proj(xs, row(norm1_g[l]), sc1, sh1, w_all, w_ab, l)
        gates = gate_proj(xs, row(norm1_g[l]), sc1, sh1, w_gates, l)
        y_ret = retention(z, cos2, sin2)
        y_lru = rglru(z, lru_conv_w[l], lru_conv_b[l], lru_wa[l], lru_ba[l], lru_wx[l], lru_bx[l],
                      lru_lambda[l])
        y_gdn = gdn(z, z_ab, gdn_conv_w[l], gdn_a_log[l], gdn_dt_bias[l], gdn_norm_g[l])
        n_r = N_GROUPS + N_EXPERTS
        w_route = jnp.concatenate([router_grp_w[l], router_exp_w[l],
                                   jnp.zeros((D, ROUTE_LANES - n_r), F32)], axis=1)
        b_route = jnp.concatenate([router_grp_b[l], router_exp_b[l],
                                   jnp.zeros((ROUTE_LANES - n_r,), F32)]).reshape(1, ROUTE_LANES)
        xs, h2, logits = merge(y_ret, y_lru, y_gdn, gates, xs, gt1, branch_proj[l].astype(BF16),
                               w_out[l].astype(BF16), row(norm2_g[l]), sc2, sh2, w_route, b_route)
        rec, counts = route(logits)
        block_e, nused, src_row, dst_row = dispatch_plan(rec, counts, S)
        y = experts(h2, block_e, nused, src_row, dst_row, moe_w1, moe_w3, moe_w2, l)
        xs = combine(xs, y, rec, gt2, row(final_g), final=(l == DEPTH - 1))
    return xs.reshape(B, S, D)
```

```python
import functools
import math

import numpy as np
import jax
import jax.numpy as jnp
from jax import lax
from jax.experimental import pallas as pl
from jax.experimental.pallas import tpu as pltpu

F32 = jnp.float32
BF16 = jnp.bfloat16
HIGHEST = lax.Precision.HIGHEST

D_MODEL = 2048
DEPTH = 2
RET_HEADS = 8
RET_DK = 128
RET_CHUNK = 128
ROPE_BASE = 10000.0
LRU_W = 1024
LRU_BLOCKS = 8
LRU_BS = LRU_W // LRU_BLOCKS
LRU_C = 8.0
CONV_K = 4
GDN_HEADS = 8
GDN_DK = 128
GDN_CHUNK = 64
N_BRANCH = 3
N_GROUPS = 4
EXPERTS_PER_GROUP = 8
N_EXPERTS = N_GROUPS * EXPERTS_PER_GROUP
D_EXPERT = 512
EPS = 1e-6

HEAD_W = 1024
MAIN_COLS = 10 * HEAD_W
N_AB = 2 * GDN_HEADS
GATE_COLS = N_BRANCH * D_MODEL
PROJ_TN = 1280
GATE_TN = 1024
GATE_ROW_CHUNK = 256
LANES = 128
SUBLANES = 8
EXPERT_ROWS = 256
COMBINE_ROWS = 256
GDN_ROWS = 2 * GDN_CHUNK
RET_ROWS = 4 * RET_CHUNK
VMEM_LIMIT = 56 * 1024 * 1024

NT_DIMS = (((1,), (1,)), ((), ()))
TN_DIMS = (((0,), (0,)), ((), ()))


def _params(*sem):
    return pltpu.CompilerParams(dimension_semantics=sem, vmem_limit_bytes=VMEM_LIMIT)


def _dot32(a, b):
    return jnp.dot(a, b, preferred_element_type=F32, precision=HIGHEST)


def _dotg32(a, b, dims):
    return lax.dot_general(a, b, dims, preferred_element_type=F32, precision=HIGHEST)


def _dot(a, b):
    return jnp.dot(a.astype(BF16), b.astype(BF16), preferred_element_type=F32)


def _dotg(a, b, dims):
    return lax.dot_general(a.astype(BF16), b.astype(BF16), dims, preferred_element_type=F32)


def _dot3(a, b):
    ah = a.astype(BF16)
    bh = b.astype(BF16)
    al = (a - ah.astype(F32)).astype(BF16)
    bl = (b - bh.astype(F32)).astype(BF16)
    d = functools.partial(jnp.dot, preferred_element_type=F32)
    return d(ah, bh) + (d(ah, bl) + d(al, bh))


def _silu(x):
    return x * jax.nn.sigmoid(x)


def _softplus(x):
    return jnp.maximum(x, 0.0) + jnp.log1p(jnp.exp(-jnp.abs(x)))


def _rms(x):
    return x * lax.rsqrt(jnp.mean(x * x, axis=-1, keepdims=True) + EPS)


def _shift_rows(x, s, fill):
    n = x.shape[0]
    if s % SUBLANES == 0:
        return jnp.concatenate([jnp.full((s, x.shape[1]), fill, x.dtype), x[:n - s]], axis=0)
    row = lax.broadcasted_iota(jnp.int32, x.shape, 0)
    return jnp.where(row < s, fill, pltpu.roll(x, s, 0))


def _causal_conv(x, tail, w):
    n = x.shape[0]
    row = lax.broadcasted_iota(jnp.int32, (SUBLANES, x.shape[1]), 0)
    y = x * w[CONV_K - 1:CONV_K]
    for s in range(1, CONV_K):
        xs = pltpu.roll(x, s, 0)
        head = jnp.where(row < s, pltpu.roll(tail, s, 0), xs[:SUBLANES])
        xs = jnp.concatenate([head, xs[SUBLANES:]], axis=0) if n > SUBLANES else head
        y = y + xs * w[CONV_K - 1 - s:CONV_K - s]
    return y


def _mod_kernel(c_ref, w_ref, b_ref, o_ref):
    c = c_ref[...]
    o_ref[...] = _dot32(_silu(c), w_ref[...]) + b_ref[...]


def ada_mod(c, ada_w, ada_b):
    L, D, N = ada_w.shape
    tn = 1536
    c8 = jnp.broadcast_to(c.astype(F32), (SUBLANES, D))
    out = pl.pallas_call(
        _mod_kernel,
        grid=(L, N // tn),
        in_specs=[pl.BlockSpec((SUBLANES, D), lambda l, j: (0, 0)),
                  pl.BlockSpec((None, D, tn), lambda l, j: (l, 0, j)),
                  pl.BlockSpec((None, 1, tn), lambda l, j: (l, 0, j))],
        out_specs=pl.BlockSpec((None, SUBLANES, tn), lambda l, j: (l, 0, j)),
        out_shape=jax.ShapeDtypeStruct((L, SUBLANES, N), F32),
        compiler_params=_params("parallel", "parallel"),
        name="ada_mod",
    )(c8, ada_w, ada_b.reshape(L, 1, N))
    return out[:, 0, :]


def _norm_proj_kernel(x_ref, g_ref, sc_ref, sh_ref, w_ref, wab_ref, o_ref, ab_ref, h_ref):
    @pl.when(pl.program_id(1) == 0)
    def _():
        h = (_rms(x_ref[...]) * g_ref[...]) * (1.0 + sc_ref[...]) + sh_ref[...]
        h_ref[...] = h.astype(BF16)
        ab_ref[...] = jnp.dot(h_ref[...], wab_ref[...], preferred_element_type=F32)

    o_ref[...] = jnp.dot(h_ref[...], w_ref[...], preferred_element_type=F32)


def norm_proj(x, g, sc, sh, w, w_ab, layer):
    S, D = x.shape
    tm = min(1024, S)
    vec = pl.BlockSpec((1, D), lambda i, j: (0, 0))
    return pl.pallas_call(
        _norm_proj_kernel,
        grid=(S // tm, MAIN_COLS // PROJ_TN),
        in_specs=[pl.BlockSpec((tm, D), lambda i, j: (i, 0)), vec, vec, vec,
                  pl.BlockSpec((None, D, PROJ_TN), lambda i, j: (layer, 0, j)),
                  pl.BlockSpec((None, D, LANES), lambda i, j: (layer, 0, 0))],
        out_specs=[pl.BlockSpec((tm, PROJ_TN), lambda i, j: (i, j)),
                   pl.BlockSpec((tm, LANES), lambda i, j: (i, 0))],
        out_shape=[jax.ShapeDtypeStruct((S, MAIN_COLS), F32), jax.ShapeDtypeStruct((S, LANES), F32)],
        scratch_shapes=[pltpu.VMEM((tm, D), BF16)],
        compiler_params=_params("parallel", "arbitrary"),
        name="norm_proj",
    )(x, g, sc, sh, w, w_ab)


def _gate_proj_kernel(x_ref, g_ref, sc_ref, sh_ref, w_ref, o_ref, h_ref):
    @pl.when(pl.program_id(1) == 0)
    def _():
        h = (_rms(x_ref[...]) * g_ref[...]) * (1.0 + sc_ref[...]) + sh_ref[...]
        h_ref[...] = h.astype(BF16)

    rows = GATE_ROW_CHUNK
    for r in range(h_ref.shape[0] // rows):
        rs = slice(r * rows, (r + 1) * rows)
        o_ref[rs] = jax.nn.sigmoid(jnp.dot(h_ref[rs], w_ref[...], preferred_element_type=F32)).astype(BF16)


def gate_proj(x, g, sc, sh, w_gates, layer):
    S, D = x.shape
    tm = min(1024, S)
    tn = GATE_TN
    vec = pl.BlockSpec((1, D), lambda i, j: (0, 0))
    return pl.pallas_call(
        _gate_proj_kernel,
        grid=(S // tm, GATE_COLS // tn),
        in_specs=[pl.BlockSpec((tm, D), lambda i, j: (i, 0)), vec, vec, vec,
                  pl.BlockSpec((None, D, tn), lambda i, j: (layer, 0, j))],
        out_specs=pl.BlockSpec((tm, tn), lambda i, j: (i, j)),
        out_shape=jax.ShapeDtypeStruct((S, GATE_COLS), BF16),
        scratch_shapes=[pltpu.VMEM((tm, D), BF16)],
        compiler_params=_params("parallel", "arbitrary"),
        name="gate_proj",
    )(x, g, sc, sh, w_gates)


def _ret_tables():
    H, C = RET_HEADS, RET_CHUNK
    log_g = np.log1p(-np.exp2(-5.0 - np.arange(H, dtype=np.float64)))
    pos = np.arange(C, dtype=np.float64)
    rel = pos[:, None] - pos[None, :]
    intra = np.where(rel >= 0, np.exp(log_g[:, None, None] * np.maximum(rel, 0.0)), 0.0)
    qd = np.exp(log_g[:, None] * (pos + 1.0))
    kd = np.exp(log_g[:, None] * (C - 1.0 - pos))
    ones = np.ones((1, 1, LANES))
    cd = [float(np.exp(lg * C)) for lg in log_g]
    return (jnp.asarray(intra, F32), jnp.asarray(qd[:, :, None] * ones, F32),
            jnp.asarray(kd[:, :, None] * ones, F32), cd)


def _ret_kernel(cd, q_ref, k_ref, v_ref, g_ref, cos_ref, sin_ref, intra_ref, qd_ref, kd_ref,
                o_ref, state_ref):
    @pl.when(pl.program_id(0) == 0)
    def _():
        state_ref[...] = jnp.zeros_like(state_ref)

    C = RET_CHUNK
    half = RET_DK // 2
    for c in range(q_ref.shape[0] // C):
        rs = slice(c * C, (c + 1) * C)
        cos2 = cos_ref[rs]
        sin2 = sin_ref[rs]
        for h in range(RET_HEADS):
            sl = slice(h * RET_DK, (h + 1) * RET_DK)
            q = q_ref[rs, sl]
            k = k_ref[rs, sl]
            v = v_ref[rs, sl]
            q = q * cos2 + pltpu.roll(q, half, 1) * sin2
            k = (k * cos2 + pltpu.roll(k, half, 1) * sin2) * (RET_DK ** -0.5)
            scores = _dotg(q, k, NT_DIMS) * intra_ref[h]
            state = state_ref[h]
            o = _dot(scores, v) + _dot(q * qd_ref[h], state)
            state_ref[h] = state * cd[h] + _dotg(k * kd_ref[h], v, TN_DIMS)
            o_ref[rs, sl] = (_rms(o) * _silu(g_ref[rs, sl])).astype(o_ref.dtype)


def retention(z, cos2, sin2):
    S = z.shape[0]
    C = min(RET_ROWS, S)
    intra, qd, kd, cd = _ret_tables()
    col = lambda b: pl.BlockSpec((C, HEAD_W), lambda n, b=b: (n, b))
    tab = lambda a: pl.BlockSpec(a.shape, lambda n: (0, 0, 0))
    rot = pl.BlockSpec((C, RET_DK), lambda n: (n, 0))
    return pl.pallas_call(
        functools.partial(_ret_kernel, cd),
        grid=(S // C,),
        in_specs=[col(0), col(1), col(2), col(3), rot, rot, tab(intra), tab(qd), tab(kd)],
        out_specs=pl.BlockSpec((C, HEAD_W), lambda n: (n, 0)),
        out_shape=jax.ShapeDtypeStruct((S, HEAD_W), BF16),
        scratch_shapes=[pltpu.VMEM((RET_HEADS, RET_DK, RET_DK), F32)],
        compiler_params=_params("arbitrary"),
        name="retention",
    )(z, z, z, z, cos2, sin2, intra, qd, kd)


def _lru_kernel(x_ref, gate_ref, cw_ref, cb_ref, wa_ref, ba_ref, wx_ref, bx_ref, lam_ref,
                o_ref, tail_ref, carry_ref):
    @pl.when(pl.program_id(0) == 0)
    def _():
        tail_ref[...] = jnp.zeros_like(tail_ref)
        carry_ref[...] = jnp.zeros_like(carry_ref)

    tm = x_ref.shape[0]
    for j in range(LRU_BLOCKS):
        sl = slice(j * LRU_BS, (j + 1) * LRU_BS)
        x = x_ref[:, sl]
        xc = _causal_conv(x, tail_ref[:, sl], cw_ref[:, sl]) + cb_ref[:, sl]
        tail_ref[:, sl] = x[tm - SUBLANES:]
        r = jax.nn.sigmoid(_dot(xc, wa_ref[j]) + ba_ref[:, sl])
        i = jax.nn.sigmoid(_dot(xc, wx_ref[j]) + bx_ref[:, sl])
        log_a = (-LRU_C * r) * _softplus(-lam_ref[:, sl])
        a = jnp.exp(log_a)
        t = jnp.tanh(log_a)
        b = jnp.sqrt(-2.0 * t / (1.0 - t)) * (i * xc)
        s = 1
        while s < tm:
            b = b + a * _shift_rows(b, s, 0.0)
            a = a * _shift_rows(a, s, 1.0)
            s *= 2
        h = b + a * carry_ref[0:1, sl]
        carry_ref[0:1, sl] = h[tm - 1:tm]
        g = gate_ref[:, sl]
        gelu = g * (0.5 * (1.0 + jnp.tanh(math.sqrt(2.0 / math.pi) * (g + 0.044715 * (g * g * g)))))
        o_ref[:, sl] = (h * gelu).astype(o_ref.dtype)


def rglru(z, conv_w, conv_b, wa, ba, wx, bx, lam):
    S = z.shape[0]
    tm = min(256, S)
    row = lambda a: a.reshape(1, LRU_W)
    vec = pl.BlockSpec((1, LRU_W), lambda n: (0, 0))
    wspec = pl.BlockSpec((LRU_BLOCKS, LRU_BS, LRU_BS), lambda n: (0, 0, 0))
    return pl.pallas_call(
        _lru_kernel,
        grid=(S // tm,),
        in_specs=[pl.BlockSpec((tm, HEAD_W), lambda n: (n, 4)),
                  pl.BlockSpec((tm, HEAD_W), lambda n: (n, 5)),
                  pl.BlockSpec((CONV_K, LRU_W), lambda n: (0, 0)), vec, wspec, vec, wspec, vec, vec],
        out_specs=pl.BlockSpec((tm, HEAD_W), lambda n: (n, 0)),
        out_shape=jax.ShapeDtypeStruct((S, HEAD_W), BF16),
        scratch_shapes=[pltpu.VMEM((SUBLANES, LRU_W), F32), pltpu.VMEM((SUBLANES, LRU_W), F32)],
        compiler_params=_params("arbitrary"),
        name="rglru",
    )(z, z, conv_w, row(conv_b), wa, row(ba), wx, row(bx), row(lam))


def _gdn_tables():
    C = GDN_CHUNK
    i = np.arange(C)[:, None]
    j = np.arange(C)[None, :]
    levels = []
    s = 1
    while s < C:
        levels.append((i // (2 * s) == j // (2 * s)) & (i % (2 * s) >= s) & (j % (2 * s) < s))
        s *= 2
    return jnp.asarray(np.stack(levels), F32), jnp.asarray(i >= j, F32), jnp.asarray(i <= j, F32)


def _gdn_kernel(q_ref, k_ref, v_ref, z_ref, ab_ref, cw_ref, alog_ref, dtb_ref, ng_ref, lvl_ref, tri_ref,
                trit_ref, o_ref, tail_ref, state_ref):
    @pl.when(pl.program_id(0) == 0)
    def _():
        tail_ref[...] = jnp.zeros_like(tail_ref)
        state_ref[...] = jnp.zeros_like(state_ref)

    C = GDN_CHUNK
    H = GDN_HEADS
    DK = GDN_DK
    rows = q_ref.shape[0]
    incl = tri_ref[...]
    rowi = lax.broadcasted_iota(jnp.int32, (C, C), 0)
    coli = lax.broadcasted_iota(jnp.int32, (C, C), 1)
    eye = (rowi == coli).astype(F32)

    ab = ab_ref[...]
    g_all = -jnp.exp(alog_ref[...]) * _softplus(ab + dtb_ref[...])
    beta_all = jax.nn.sigmoid(ab)

    conv = []
    for p, ref in enumerate((q_ref, k_ref, v_ref)):
        x = ref[...]
        conv.append(_silu(_causal_conv(x, tail_ref[p], cw_ref[p])))
        tail_ref[p] = x[rows - SUBLANES:]
    qc, kc, vc = conv

    pairs = [(c, h) for c in range(rows // C) for h in range(H)]
    st = {}
    for c in range(rows // C):
        g_c = g_all[c * C:(c + 1) * C]
        st[c] = (_dot32(incl, g_c),
                 _dotg32(g_c, trit_ref[...], TN_DIMS))
    for c, h in pairs:
        rs = slice(c * C, (c + 1) * C)
        sl = slice(h * DK, (h + 1) * DK)
        q = qc[rs, sl]
        k = kc[rs, sl]
        q = q * lax.rsqrt(jnp.sum(q * q, axis=-1, keepdims=True) + EPS) * (DK ** -0.5)
        k = k * lax.rsqrt(jnp.sum(k * k, axis=-1, keepdims=True) + EPS)
        beta = beta_all[rs, H + h:H + h + 1]
        gc = st[c][0][:, h:h + 1]
        gcr = st[c][1][h:h + 1, :]
        decay = incl * jnp.exp(jnp.where(incl > 0, gc - gcr, 0.0))
        kb = k * beta
        egc = jnp.exp(gc)
        gl = gc[C - 1:C]
        qk = _dotg(jnp.concatenate([q, kb], axis=0), k, NT_DIMS)
        st[c, h] = dict(
            attn=qk[:C] * decay,
            m=qk[C:] * (decay - eye),
            rhs=jnp.concatenate([vc[rs, sl] * beta, kb * egc], axis=1),
            qdec=q * egc,
            kdt=(k * jnp.exp(gl - gc)).T,
            egl=jnp.exp(gl))
    for p in pairs:
        st[p]['inv'] = eye - st[p]['m'] * lvl_ref[0]
    for lv in range(1, lvl_ref.shape[0]):
        for p in pairs:
            st[p]['t'] = _dot(st[p]['m'] * lvl_ref[lv], st[p]['inv'])
        for p in pairs:
            st[p]['inv'] = st[p]['inv'] - _dot(st[p]['inv'], st[p]['t'])
    for p in pairs:
        st[p]['uw'] = _dot(st[p]['inv'], st[p]['rhs'])
    for c, h in pairs:
        d = st[c, h]
        state = state_ref[h]
        ws = _dot(jnp.concatenate([d['uw'][:, DK:], d['qdec']], axis=0), state)
        v_new = d['uw'][:, :DK] - ws[:C]
        o = ws[C:] + _dot(d['attn'], v_new)
        state_ref[h] = state * d['egl'] + _dot(d['kdt'], v_new)
        rs = slice(c * C, (c + 1) * C)
        sl = slice(h * DK, (h + 1) * DK)
        o_ref[rs, sl] = (_rms(o) * ng_ref[...] * _silu(z_ref[rs, sl])).astype(o_ref.dtype)


def gdn(z, z_ab, conv_w, a_log, dt_bias, norm_g):
    S = z.shape[0]
    C = GDN_CHUNK
    H = GDN_HEADS
    lvl, tri, trit = _gdn_tables()
    alog = jnp.zeros((1, LANES), F32).at[0, :H].set(a_log)
    dtb = jnp.zeros((1, LANES), F32).at[0, :H].set(dt_bias)
    cw = conv_w.reshape(CONV_K, 3, HEAD_W).transpose(1, 0, 2)
    tm = min(GDN_ROWS, S)
    col = lambda b: pl.BlockSpec((tm, HEAD_W), lambda n, b=b: (n, b))
    lane = pl.BlockSpec((1, LANES), lambda n: (0, 0))
    return pl.pallas_call(
        _gdn_kernel,
        grid=(S // tm,),
        in_specs=[col(6), col(7), col(8), col(9),
                  pl.BlockSpec((tm, LANES), lambda n: (n, 0)),
                  pl.BlockSpec((3, CONV_K, HEAD_W), lambda n: (0, 0, 0)),
                  lane, lane, lane,
                  pl.BlockSpec(lvl.shape, lambda n: (0, 0, 0)),
                  pl.BlockSpec(tri.shape, lambda n: (0, 0)),
                  pl.BlockSpec(trit.shape, lambda n: (0, 0))],
        out_specs=pl.BlockSpec((tm, HEAD_W), lambda n: (n, 0)),
        out_shape=jax.ShapeDtypeStruct((S, HEAD_W), BF16),
        scratch_shapes=[pltpu.VMEM((3, SUBLANES, HEAD_W), F32),
                        pltpu.VMEM((H, GDN_DK, GDN_DK), F32)],
        compiler_params=_params("arbitrary"),
        name="gdn",
    )(z, z, z, z, z_ab, cw, alog, dtb, norm_g.reshape(1, GDN_DK), lvl, tri, trit)


ROUTE_LANES = LANES


def _merge_kernel(yr_ref, yl_ref, yg_ref, g0_ref, g1_ref, g2_ref, x_ref, gt_ref, bp_ref, wo_ref,
                  ng_ref, sc_ref, sh_ref, wr_ref, br_ref, xo_ref, h_ref, lg_ref):
    merged = g0_ref[...] * jnp.dot(yr_ref[...], bp_ref[0], preferred_element_type=F32)
    merged += g1_ref[...] * jnp.dot(yl_ref[...], bp_ref[1], preferred_element_type=F32)
    merged += g2_ref[...] * jnp.dot(yg_ref[...], bp_ref[2], preferred_element_type=F32)
    out = jnp.dot(merged.astype(BF16), wo_ref[...], preferred_element_type=F32)
    x = x_ref[...] + gt_ref[...] * out
    xo_ref[...] = x
    h = (_rms(x) * ng_ref[...]) * (1.0 + sc_ref[...]) + sh_ref[...]
    h_ref[...] = h
    lg_ref[...] = _dot3(h, wr_ref[...]) + br_ref[...]


def merge(y_ret, y_lru, y_gdn, gates, x, gt1, bp, w_out, norm2_g, sc2, sh2, w_route, b_route):
    S, D = x.shape
    tm = min(256, S)
    ycol = pl.BlockSpec((tm, HEAD_W), lambda i: (i, 0))
    gate = lambda b: pl.BlockSpec((tm, D), lambda i, b=b: (i, b))
    rows = pl.BlockSpec((tm, D), lambda i: (i, 0))
    vec = pl.BlockSpec((1, D), lambda i: (0, 0))
    once = dict(pipeline_mode=pl.Buffered(1))
    return pl.pallas_call(
        _merge_kernel,
        grid=(S // tm,),
        in_specs=[ycol, ycol, ycol, gate(0), gate(1), gate(2), rows, vec,
                  pl.BlockSpec((N_BRANCH, HEAD_W, D), lambda i: (0, 0, 0), **once),
                  pl.BlockSpec((D, D), lambda i: (0, 0), **once),
                  vec, vec, vec,
                  pl.BlockSpec((D, ROUTE_LANES), lambda i: (0, 0)),
                  pl.BlockSpec((1, ROUTE_LANES), lambda i: (0, 0))],
        out_specs=[rows, rows, pl.BlockSpec((tm, ROUTE_LANES), lambda i: (i, 0))],
        out_shape=[jax.ShapeDtypeStruct((S, D), F32), jax.ShapeDtypeStruct((S, D), F32),
                   jax.ShapeDtypeStruct((S, ROUTE_LANES), F32)],
        compiler_params=_params("parallel"),
        name="merge",
    )(y_ret, y_lru, y_gdn, gates, gates, gates, x, gt1, bp, w_out, norm2_g, sc2, sh2, w_route, b_route)


R_E0, R_E1, R_W0, R_W1, R_RANK0, R_RANK1 = range(6)


def _first_max(vals, lane):
    m = jnp.max(vals, axis=-1, keepdims=True)
    idx = jnp.min(jnp.where(vals == m, lane, ROUTE_LANES), axis=-1, keepdims=True)
    return m, idx


def _route_kernel(lg_ref, o_ref, cnt_ref, carry_ref):
    @pl.when(pl.program_id(0) == 0)
    def _():
        carry_ref[...] = jnp.zeros_like(carry_ref)

    lg = lg_ref[...]
    tm = lg.shape[0]
    lane = lax.broadcasted_iota(jnp.int32, lg.shape, 1)
    neg = jnp.float32(-jnp.inf)
    grp = jnp.where(lane < N_GROUPS, lg, neg)
    gmax, gsel = _first_max(grp, lane)
    pg_top = 1.0 / jnp.sum(jnp.exp(grp - gmax), axis=-1, keepdims=True)
    lo = N_GROUPS + gsel * EXPERTS_PER_GROUP
    inside = jnp.where((lane >= lo) & (lane < lo + EXPERTS_PER_GROUP), lg, neg)
    v0, i0 = _first_max(inside, lane)
    v1, i1 = _first_max(jnp.where(lane == i0, neg, inside), lane)
    e1w = jnp.exp(v1 - v0)
    w0 = pg_top / (1.0 + e1w)
    w1 = pg_top * e1w / (1.0 + e1w)
    hit0 = lane == i0
    hit1 = lane == i1
    hits = (hit0 | hit1).astype(BF16)
    row = lax.broadcasted_iota(jnp.int32, (tm, tm), 0)
    col = lax.broadcasted_iota(jnp.int32, (tm, tm), 1)
    before = jnp.dot((row > col).astype(BF16), hits, preferred_element_type=F32) + carry_ref[0:1, :]
    rank0 = jnp.sum(jnp.where(hit0, before, 0.0), axis=-1, keepdims=True)
    rank1 = jnp.sum(jnp.where(hit1, before, 0.0), axis=-1, keepdims=True)
    total = carry_ref[0:1, :] + jnp.sum(hits.astype(F32), axis=0, keepdims=True)
    carry_ref[0:1, :] = total
    cnt_ref[...] = jnp.broadcast_to(total, cnt_ref.shape)
    rec = jnp.zeros(lg.shape, F32)
    for slot, val in ((R_E0, (i0 - N_GROUPS).astype(F32)), (R_E1, (i1 - N_GROUPS).astype(F32)),
                      (R_W0, w0), (R_W1, w1), (R_RANK0, rank0), (R_RANK1, rank1)):
        rec = jnp.where(lane == slot, val, rec)
    o_ref[...] = rec


def route(logits):
    S = logits.shape[0]
    tm = min(256, S)
    return pl.pallas_call(
        _route_kernel,
        grid=(S // tm,),
        in_specs=[pl.BlockSpec((tm, ROUTE_LANES), lambda i: (i, 0))],
        out_specs=[pl.BlockSpec((tm, ROUTE_LANES), lambda i: (i, 0)),
                   pl.BlockSpec((SUBLANES, ROUTE_LANES), lambda i: (0, 0))],
        out_shape=[jax.ShapeDtypeStruct((S, ROUTE_LANES), F32),
                   jax.ShapeDtypeStruct((SUBLANES, ROUTE_LANES), F32)],
        scratch_shapes=[pltpu.VMEM((SUBLANES, ROUTE_LANES), F32)],
        compiler_params=_params("arbitrary"),
        name="route",
    )(logits)


def _expert_kernel(be_ref, nused_ref, src_ref, nsrc_ref, h_hbm, w1_ref, w3_ref, w2_ref, o_ref,
                   xbuf, wb1, wb3, wb2, gsem):
    i = pl.program_id(0)
    nused = nused_ref[0]
    B = EXPERT_ROWS
    slot = i % 2

    def gather(idx_ref, s):
        for r in range(B):
            pltpu.make_async_copy(h_hbm.at[pl.ds(idx_ref[0, r], 1)], xbuf.at[s, pl.ds(r, 1)],
                                  gsem.at[s]).start()

    @pl.when(i == 0)
    def _():
        gather(src_ref, 0)

    @pl.when(i < nused)
    def _():
        @pl.when(i + 1 < nused)
        def _():
            gather(nsrc_ref, 1 - slot)

        @pl.when((i == 0) | (be_ref[i] != be_ref[jnp.maximum(i - 1, 0)]))
        def _():
            wb1[...] = w1_ref[...].astype(BF16)
            wb3[...] = w3_ref[...].astype(BF16)
            wb2[...] = w2_ref[...].astype(BF16)

        pltpu.make_async_copy(h_hbm.at[pl.ds(0, B)], xbuf.at[slot], gsem.at[slot]).wait()
        x = xbuf[slot].astype(BF16)
        a1 = jnp.dot(x, wb1[...], preferred_element_type=F32)
        a3 = jnp.dot(x, wb3[...], preferred_element_type=F32)
        mid = (_silu(a1) * a3).astype(BF16)
        o_ref[...] = jnp.dot(mid, wb2[...], preferred_element_type=F32)

    @pl.when(i >= nused)
    def _():
        o_ref[...] = jnp.zeros(o_ref.shape, o_ref.dtype)


def experts(h, block_e, nused, src_row, w1, w3, w2, layer):
    T, D = h.shape
    nb = block_e.shape[0]
    B = EXPERT_ROWS
    idx = lambda f: pl.BlockSpec((None, 1, B), f, memory_space=pltpu.SMEM)
    wspec = lambda shp: pl.BlockSpec((None, None) + shp, lambda i, be, nu: (layer, be[i], 0, 0))
    grid_spec = pltpu.PrefetchScalarGridSpec(
        num_scalar_prefetch=2,
        grid=(nb,),
        in_specs=[idx(lambda i, be, nu: (i, 0, 0)),
                  idx(lambda i, be, nu: (jnp.minimum(i + 1, nb - 1), 0, 0)),
                  pl.BlockSpec(memory_space=pl.ANY),
                  wspec((D, D_EXPERT)), wspec((D, D_EXPERT)), wspec((D_EXPERT, D))],
        out_specs=pl.BlockSpec((B, D), lambda i, be, nu: (i, 0)),
        scratch_shapes=[pltpu.VMEM((2, B, D), F32),
                        pltpu.VMEM((D, D_EXPERT), BF16), pltpu.VMEM((D, D_EXPERT), BF16),
                        pltpu.VMEM((D_EXPERT, D), BF16),
                        pltpu.SemaphoreType.DMA((2,))])
    src3 = src_row.reshape(nb, 1, B)
    return pl.pallas_call(
        _expert_kernel,
        grid_spec=grid_spec,
        out_shape=jax.ShapeDtypeStruct((nb * B, D), F32),
        compiler_params=_params("arbitrary"),
        name="experts",
    )(block_e, nused, src3, src3, h, w1, w3, w2)


def dispatch_plan(rec, counts, T):
    B = EXPERT_ROWS
    A = 2 * T
    nb = A // B + N_EXPERTS
    cnt = counts[0, N_GROUPS:N_GROUPS + N_EXPERTS].astype(jnp.int32)
    padded = ((cnt + B - 1) // B) * B
    pend = jnp.cumsum(padded)
    pstart = pend - padded
    e = rec[:, R_E0:R_E1 + 1].astype(jnp.int32)
    rank = rec[:, R_RANK0:R_RANK1 + 1].astype(jnp.int32)
    dest = pstart[e] + rank
    tok = jnp.broadcast_to(jnp.arange(T, dtype=jnp.int32)[:, None], (T, 2))
    src_row = jnp.zeros((nb * B,), jnp.int32).at[dest.reshape(-1)].set(tok.reshape(-1))
    first = jnp.arange(nb, dtype=jnp.int32) * B
    block_e = jnp.minimum(jnp.sum((first[:, None] >= pend[None, :]).astype(jnp.int32), axis=1),
                          N_EXPERTS - 1)
    nused = (pend[-1] // B).astype(jnp.int32).reshape(1)
    return block_e, nused, src_row, dest


def _combine_kernel(final, d0_ref, d1_ref, nd0_ref, nd1_ref, x_ref, rec_ref, gt_ref, fg_ref, y_hbm,
                    o_ref, ybuf, sem):
    i = pl.program_id(0)
    tm = x_ref.shape[0]
    slot = i % 2

    def fetch(r0_ref, r1_ref, s):
        for k, idx_ref in enumerate((r0_ref, r1_ref)):
            for r in range(tm):
                pltpu.make_async_copy(y_hbm.at[pl.ds(idx_ref[0, r], 1)], ybuf.at[s, k, pl.ds(r, 1)],
                                      sem.at[s]).start()

    @pl.when(i == 0)
    def _():
        fetch(d0_ref, d1_ref, 0)

    @pl.when(i + 1 < pl.num_programs(0))
    def _():
        fetch(nd0_ref, nd1_ref, 1 - slot)

    for k in range(2):
        pltpu.make_async_copy(y_hbm.at[pl.ds(0, tm)], ybuf.at[slot, k], sem.at[slot]).wait()
    rec = rec_ref[...]
    moe = rec[:, R_W0:R_W0 + 1] * ybuf[slot, 0] + rec[:, R_W1:R_W1 + 1] * ybuf[slot, 1]
    x = x_ref[...] + gt_ref[...] * moe
    o_ref[...] = _rms(x) * fg_ref[...] if final else x


def combine(x, y, dest, rec, gt2, final_g, final):
    S, D = x.shape
    tm = min(COMBINE_ROWS, S)
    nt = S // tm
    d0 = dest[:, 0].reshape(nt, 1, tm)
    d1 = dest[:, 1].reshape(nt, 1, tm)
    rows = pl.BlockSpec((tm, D), lambda i: (i, 0))
    vec = pl.BlockSpec((1, D), lambda i: (0, 0))
    cur = pl.BlockSpec((None, 1, tm), lambda i: (i, 0, 0), memory_space=pltpu.SMEM)
    nxt = pl.BlockSpec((None, 1, tm), lambda i: (jnp.minimum(i + 1, nt - 1), 0, 0), memory_space=pltpu.SMEM)
    return pl.pallas_call(
        functools.partial(_combine_kernel, final),
        grid=(nt,),
        in_specs=[cur, cur, nxt, nxt, rows, pl.BlockSpec((tm, ROUTE_LANES), lambda i: (i, 0)), vec, vec,
                  pl.BlockSpec(memory_space=pl.ANY)],
        out_specs=rows,
        out_shape=jax.ShapeDtypeStruct((S, D), F32),
        scratch_shapes=[pltpu.VMEM((2, 2, tm, D), F32), pltpu.SemaphoreType.DMA((2,))],
        compiler_params=_params("arbitrary"),
        name="combine",
    )(d0, d1, d0, d1, x, rec, gt2, final_g, y)


def _rope_tables(positions):
    half = RET_DK // 2
    inv_freq = ROPE_BASE ** (-jnp.arange(half, dtype=F32) * (2.0 / RET_DK))
    ang = positions.astype(F32)[:, None] * inv_freq
    cos = jnp.cos(ang)
    sin = jnp.sin(ang)
    return jnp.concatenate([cos, cos], axis=-1), jnp.concatenate([-sin, sin], axis=-1)


def _proj_weights(w_in):
    L, D, _ = w_in.shape
    w_all = w_in[:, :, :MAIN_COLS].astype(BF16)
    ab = w_in[:, :, MAIN_COLS:MAIN_COLS + N_AB]
    w_ab = jnp.concatenate([ab, jnp.zeros((L, D, LANES - N_AB), w_in.dtype)], axis=2).astype(BF16)
    w_gates = w_in[:, :, MAIN_COLS + N_AB:].astype(BF16)
    return w_all, w_ab, w_gates


def kernel(x, c, positions, ada_w, ada_b, norm1_g, norm2_g, w_in, lru_conv_w, lru_conv_b, lru_wa, lru_ba,
           lru_wx, lru_bx, lru_lambda, gdn_conv_w, gdn_a_log, gdn_dt_bias, gdn_norm_g, branch_proj, w_out,
           router_grp_w, router_grp_b, router_exp_w, router_exp_b, moe_w1, moe_w3, moe_w2, final_g):
    B, S, D = x.shape
    assert B == 1
    xs = x.reshape(S, D)
    mod = ada_mod(c, ada_w, ada_b)
    cos2, sin2 = _rope_tables(positions.reshape(S))
    row = lambda v: v.reshape(1, -1).astype(F32)
    w_all, w_ab, w_gates = _proj_weights(w_in)
    for l in range(DEPTH):
        sh1, sc1, gt1, sh2, sc2, gt2 = (row(m) for m in jnp.split(mod[l], 6))
        z, z_ab = norm_proj(xs, row(norm1_g[l]), sc1, sh1, w_all, w_ab, l)
        gates = gate_proj(xs, row(norm1_g[l]), sc1, sh1, w_gates, l)
        y_ret = retention(z, cos2, sin2)
        y_lru = rglru(z, lru_conv_w[l], lru_conv_b[l], lru_wa[l], lru_ba[l], lru_wx[l], lru_bx[l],
                      lru_lambda[l])
        y_gdn = gdn(z, z_ab, gdn_conv_w[l], gdn_a_log[l], gdn_dt_bias[l], gdn_norm_g[l])
        n_r = N_GROUPS + N_EXPERTS
        w_route = jnp.concatenate([router_grp_w[l], router_exp_w[l],
                                   jnp.zeros((D, ROUTE_LANES - n_r), F32)], axis=1)
        b_route = jnp.concatenate([router_grp_b[l], router_exp_b[l],
                                   jnp.zeros((ROUTE_LANES - n_r,), F32)]).reshape(1, ROUTE_LANES)
        xs, h2, logits = merge(y_ret, y_lru, y_gdn, gates, xs, gt1, branch_proj[l].astype(BF16),
                               w_out[l].astype(BF16), row(norm2_g[l]), sc2, sh2, w_route, b_route)
        rec, counts = route(logits)
        block_e, nused, src_row, dest = dispatch_plan(rec, counts, S)
        y = experts(h2, block_e, nused, src_row, moe_w1, moe_w3, moe_w2, l)
        xs = combine(xs, y, dest, rec, gt2, row(final_g), final=(l == DEPTH - 1))
    return xs.reshape(B, S, D)
```

```python
import functools
import math

import numpy as np
import jax
import jax.numpy as jnp
from jax import lax
from jax.experimental import pallas as pl
from jax.experimental.pallas import tpu as pltpu

F32 = jnp.float32
BF16 = jnp.bfloat16
HIGHEST = lax.Precision.HIGHEST

D_MODEL = 2048
DEPTH = 2
RET_HEADS = 8
RET_DK = 128
RET_CHUNK = 128
ROPE_BASE = 10000.0
LRU_W = 1024
LRU_BLOCKS = 8
LRU_BS = LRU_W // LRU_BLOCKS
LRU_C = 8.0
CONV_K = 4
GDN_HEADS = 8
GDN_DK = 128
GDN_CHUNK = 64
N_BRANCH = 3
N_GROUPS = 4
EXPERTS_PER_GROUP = 8
N_EXPERTS = N_GROUPS * EXPERTS_PER_GROUP
D_EXPERT = 512
EPS = 1e-6

HEAD_W = 1024
MAIN_COLS = 10 * HEAD_W
N_AB = 2 * GDN_HEADS
GATE_COLS = N_BRANCH * D_MODEL
PROJ_TN = 1280
GATE_TN = 1024
GATE_ROW_CHUNK = 256
LANES = 128
SUBLANES = 8
EXPERT_ROWS = 256
COMBINE_ROWS = 256
GATHER_DEPTH = 3
GDN_ROWS = 2 * GDN_CHUNK
RET_ROWS = 4 * RET_CHUNK
VMEM_LIMIT = 56 * 1024 * 1024

NT_DIMS = (((1,), (1,)), ((), ()))
TN_DIMS = (((0,), (0,)), ((), ()))


def _params(*sem):
    return pltpu.CompilerParams(dimension_semantics=sem, vmem_limit_bytes=VMEM_LIMIT)


def _dot32(a, b):
    return jnp.dot(a, b, preferred_element_type=F32, precision=HIGHEST)


def _dotg32(a, b, dims):
    return lax.dot_general(a, b, dims, preferred_element_type=F32, precision=HIGHEST)


def _dot(a, b):
    return jnp.dot(a.astype(BF16), b.astype(BF16), preferred_element_type=F32)


def _dotg(a, b, dims):
    return lax.dot_general(a.astype(BF16), b.astype(BF16), dims, preferred_element_type=F32)


def _dot3(a, b):
    ah = a.astype(BF16)
    bh = b.astype(BF16)
    al = (a - ah.astype(F32)).astype(BF16)
    bl = (b - bh.astype(F32)).astype(BF16)
    d = functools.partial(jnp.dot, preferred_element_type=F32)
    return d(ah, bh) + (d(ah, bl) + d(al, bh))


def _silu(x):
    return x * jax.nn.sigmoid(x)


def _softplus(x):
    return jnp.maximum(x, 0.0) + jnp.log1p(jnp.exp(-jnp.abs(x)))


def _rms(x):
    return x * lax.rsqrt(jnp.mean(x * x, axis=-1, keepdims=True) + EPS)


def _shift_rows(x, s, fill):
    n = x.shape[0]
    if s % SUBLANES == 0:
        return jnp.concatenate([jnp.full((s, x.shape[1]), fill, x.dtype), x[:n - s]], axis=0)
    row = lax.broadcasted_iota(jnp.int32, x.shape, 0)
    return jnp.where(row < s, fill, pltpu.roll(x, s, 0))


def _causal_conv(x, tail, w):
    n = x.shape[0]
    row = lax.broadcasted_iota(jnp.int32, (SUBLANES, x.shape[1]), 0)
    y = x * w[CONV_K - 1:CONV_K]
    for s in range(1, CONV_K):
        xs = pltpu.roll(x, s, 0)
        head = jnp.where(row < s, pltpu.roll(tail, s, 0), xs[:SUBLANES])
        xs = jnp.concatenate([head, xs[SUBLANES:]], axis=0) if n > SUBLANES else head
        y = y + xs * w[CONV_K - 1 - s:CONV_K - s]
    return y


def _mod_kernel(c_ref, w_ref, b_ref, o_ref):
    c = c_ref[...]
    o_ref[...] = _dot32(_silu(c), w_ref[...]) + b_ref[...]


def ada_mod(c, ada_w, ada_b):
    L, D, N = ada_w.shape
    tn = 1536
    c8 = jnp.broadcast_to(c.astype(F32), (SUBLANES, D))
    out = pl.pallas_call(
        _mod_kernel,
        grid=(L, N // tn),
        in_specs=[pl.BlockSpec((SUBLANES, D), lambda l, j: (0, 0)),
                  pl.BlockSpec((None, D, tn), lambda l, j: (l, 0, j)),
                  pl.BlockSpec((None, 1, tn), lambda l, j: (l, 0, j))],
        out_specs=pl.BlockSpec((None, SUBLANES, tn), lambda l, j: (l, 0, j)),
        out_shape=jax.ShapeDtypeStruct((L, SUBLANES, N), F32),
        compiler_params=_params("parallel", "parallel"),
        name="ada_mod",
    )(c8, ada_w, ada_b.reshape(L, 1, N))
    return out[:, 0, :]


def _norm_proj_kernel(x_ref, g_ref, sc_ref, sh_ref, w_ref, wab_ref, o_ref, ab_ref, h_ref):
    @pl.when(pl.program_id(1) == 0)
    def _():
        h = (_rms(x_ref[...]) * g_ref[...]) * (1.0 + sc_ref[...]) + sh_ref[...]
        h_ref[...] = h.astype(BF16)
        ab_ref[...] = jnp.dot(h_ref[...], wab_ref[...], preferred_element_type=F32)

    o_ref[...] = jnp.dot(h_ref[...], w_ref[...], preferred_element_type=F32)


def norm_proj(x, g, sc, sh, w, w_ab, layer):
    S, D = x.shape
    tm = min(1024, S)
    vec = pl.BlockSpec((1, D), lambda i, j: (0, 0))
    return pl.pallas_call(
        _norm_proj_kernel,
        grid=(S // tm, MAIN_COLS // PROJ_TN),
        in_specs=[pl.BlockSpec((tm, D), lambda i, j: (i, 0)), vec, vec, vec,
                  pl.BlockSpec((None, D, PROJ_TN), lambda i, j: (layer, 0, j)),
                  pl.BlockSpec((None, D, LANES), lambda i, j: (layer, 0, 0))],
        out_specs=[pl.BlockSpec((tm, PROJ_TN), lambda i, j: (i, j)),
                   pl.BlockSpec((tm, LANES), lambda i, j: (i, 0))],
        out_shape=[jax.ShapeDtypeStruct((S, MAIN_COLS), F32), jax.ShapeDtypeStruct((S, LANES), F32)],
        scratch_shapes=[pltpu.VMEM((tm, D), BF16)],
        compiler_params=_params("parallel", "arbitrary"),
        name="norm_proj",
    )(x, g, sc, sh, w, w_ab)


def _gate_proj_kernel(x_ref, g_ref, sc_ref, sh_ref, w_ref, o_ref, h_ref):
    @pl.when(pl.program_id(1) == 0)
    def _():
        h = (_rms(x_ref[...]) * g_ref[...]) * (1.0 + sc_ref[...]) + sh_ref[...]
        h_ref[...] = h.astype(BF16)

    rows = GATE_ROW_CHUNK
    for r in range(h_ref.shape[0] // rows):
        rs = slice(r * rows, (r + 1) * rows)
        o_ref[rs] = jax.nn.sigmoid(jnp.dot(h_ref[rs], w_ref[...], preferred_element_type=F32)).astype(BF16)


def gate_proj(x, g, sc, sh, w_gates, layer):
    S, D = x.shape
    tm = min(1024, S)
    tn = GATE_TN
    vec = pl.BlockSpec((1, D), lambda i, j: (0, 0))
    return pl.pallas_call(
        _gate_proj_kernel,
        grid=(S // tm, GATE_COLS // tn),
        in_specs=[pl.BlockSpec((tm, D), lambda i, j: (i, 0)), vec, vec, vec,
                  pl.BlockSpec((None, D, tn), lambda i, j: (layer, 0, j))],
        out_specs=pl.BlockSpec((tm, tn), lambda i, j: (i, j)),
        out_shape=jax.ShapeDtypeStruct((S, GATE_COLS), BF16),
        scratch_shapes=[pltpu.VMEM((tm, D), BF16)],
        compiler_params=_params("parallel", "arbitrary"),
        name="gate_proj",
    )(x, g, sc, sh, w_gates)


def _ret_tables():
    H, C = RET_HEADS, RET_CHUNK
    log_g = np.log1p(-np.exp2(-5.0 - np.arange(H, dtype=np.float64)))
    pos = np.arange(C, dtype=np.float64)
    rel = pos[:, None] - pos[None, :]
    intra = np.where(rel >= 0, np.exp(log_g[:, None, None] * np.maximum(rel, 0.0)), 0.0)
    qd = np.exp(log_g[:, None] * (pos + 1.0))
    kd = np.exp(log_g[:, None] * (C - 1.0 - pos))
    ones = np.ones((1, 1, LANES))
    cd = [float(np.exp(lg * C)) for lg in log_g]
    return (jnp.asarray(intra, F32), jnp.asarray(qd[:, :, None] * ones, F32),
            jnp.asarray(kd[:, :, None] * ones, F32), cd)


def _ret_kernel(cd, q_ref, k_ref, v_ref, g_ref, cos_ref, sin_ref, intra_ref, qd_ref, kd_ref,
                o_ref, state_ref):
    @pl.when(pl.program_id(0) == 0)
    def _():
        state_ref[...] = jnp.zeros_like(state_ref)

    C = RET_CHUNK
    half = RET_DK // 2
    for c in range(q_ref.shape[0] // C):
        rs = slice(c * C, (c + 1) * C)
        cos2 = cos_ref[rs]
        sin2 = sin_ref[rs]
        for h in range(RET_HEADS):
            sl = slice(h * RET_DK, (h + 1) * RET_DK)
            q = q_ref[rs, sl]
            k = k_ref[rs, sl]
            v = v_ref[rs, sl]
            q = q * cos2 + pltpu.roll(q, half, 1) * sin2
            k = (k * cos2 + pltpu.roll(k, half, 1) * sin2) * (RET_DK ** -0.5)
            scores = _dotg(q, k, NT_DIMS) * intra_ref[h]
            state = state_ref[h]
            o = _dot(scores, v) + _dot(q * qd_ref[h], state)
            state_ref[h] = state * cd[h] + _dotg(k * kd_ref[h], v, TN_DIMS)
            o_ref[rs, sl] = (_rms(o) * _silu(g_ref[rs, sl])).astype(o_ref.dtype)


def retention(z, cos2, sin2):
    S = z.shape[0]
    C = min(RET_ROWS, S)
    intra, qd, kd, cd = _ret_tables()
    col = lambda b: pl.BlockSpec((C, HEAD_W), lambda n, b=b: (n, b))
    tab = lambda a: pl.BlockSpec(a.shape, lambda n: (0, 0, 0))
    rot = pl.BlockSpec((C, RET_DK), lambda n: (n, 0))
    return pl.pallas_call(
        functools.partial(_ret_kernel, cd),
        grid=(S // C,),
        in_specs=[col(0), col(1), col(2), col(3), rot, rot, tab(intra), tab(qd), tab(kd)],
        out_specs=pl.BlockSpec((C, HEAD_W), lambda n: (n, 0)),
        out_shape=jax.ShapeDtypeStruct((S, HEAD_W), BF16),
        scratch_shapes=[pltpu.VMEM((RET_HEADS, RET_DK, RET_DK), F32)],
        compiler_params=_params("arbitrary"),
        name="retention",
    )(z, z, z, z, cos2, sin2, intra, qd, kd)


def _lru_kernel(x_ref, gate_ref, cw_ref, cb_ref, wa_ref, ba_ref, wx_ref, bx_ref, lam_ref,
                o_ref, tail_ref, carry_ref):
    @pl.when(pl.program_id(0) == 0)
    def _():
        tail_ref[...] = jnp.zeros_like(tail_ref)
        carry_ref[...] = jnp.zeros_like(carry_ref)

    tm = x_ref.shape[0]
    for j in range(LRU_BLOCKS):
        sl = slice(j * LRU_BS, (j + 1) * LRU_BS)
        x = x_ref[:, sl]
        xc = _causal_conv(x, tail_ref[:, sl], cw_ref[:, sl]) + cb_ref[:, sl]
        tail_ref[:, sl] = x[tm - SUBLANES:]
        r = jax.nn.sigmoid(_dot(xc, wa_ref[j]) + ba_ref[:, sl])
        i = jax.nn.sigmoid(_dot(xc, wx_ref[j]) + bx_ref[:, sl])
        log_a = (-LRU_C * r) * _softplus(-lam_ref[:, sl])
        a = jnp.exp(log_a)
        t = jnp.tanh(log_a)
        b = jnp.sqrt(-2.0 * t / (1.0 - t)) * (i * xc)
        s = 1
        while s < tm:
            b = b + a * _shift_rows(b, s, 0.0)
            a = a * _shift_rows(a, s, 1.0)
            s *= 2
        h = b + a * carry_ref[0:1, sl]
        carry_ref[0:1, sl] = h[tm - 1:tm]
        g = gate_ref[:, sl]
        gelu = g * (0.5 * (1.0 + jnp.tanh(math.sqrt(2.0 / math.pi) * (g + 0.044715 * (g * g * g)))))
        o_ref[:, sl] = (h * gelu).astype(o_ref.dtype)


def rglru(z, conv_w, conv_b, wa, ba, wx, bx, lam):
    S = z.shape[0]
    tm = min(256, S)
    row = lambda a: a.reshape(1, LRU_W)
    vec = pl.BlockSpec((1, LRU_W), lambda n: (0, 0))
    wspec = pl.BlockSpec((LRU_BLOCKS, LRU_BS, LRU_BS), lambda n: (0, 0, 0))
    return pl.pallas_call(
        _lru_kernel,
        grid=(S // tm,),
        in_specs=[pl.BlockSpec((tm, HEAD_W), lambda n: (n, 4)),
                  pl.BlockSpec((tm, HEAD_W), lambda n: (n, 5)),
                  pl.BlockSpec((CONV_K, LRU_W), lambda n: (0, 0)), vec, wspec, vec, wspec, vec, vec],
        out_specs=pl.BlockSpec((tm, HEAD_W), lambda n: (n, 0)),
        out_shape=jax.ShapeDtypeStruct((S, HEAD_W), BF16),
        scratch_shapes=[pltpu.VMEM((SUBLANES, LRU_W), F32), pltpu.VMEM((SUBLANES, LRU_W), F32)],
        compiler_params=_params("arbitrary"),
        name="rglru",
    )(z, z, conv_w, row(conv_b), wa, row(ba), wx, row(bx), row(lam))


def _gdn_tables():
    C = GDN_CHUNK
    i = np.arange(C)[:, None]
    j = np.arange(C)[None, :]
    levels = []
    s = 1
    while s < C:
        levels.append((i // (2 * s) == j // (2 * s)) & (i % (2 * s) >= s) & (j % (2 * s) < s))
        s *= 2
    return jnp.asarray(np.stack(levels), F32), jnp.asarray(i >= j, F32), jnp.asarray(i <= j, F32)


def _gdn_kernel(q_ref, k_ref, v_ref, z_ref, ab_ref, cw_ref, alog_ref, dtb_ref, ng_ref, lvl_ref, tri_ref,
                trit_ref, o_ref, tail_ref, state_ref):
    @pl.when(pl.program_id(0) == 0)
    def _():
        tail_ref[...] = jnp.zeros_like(tail_ref)
        state_ref[...] = jnp.zeros_like(state_ref)

    C = GDN_CHUNK
    H = GDN_HEADS
    DK = GDN_DK
    rows = q_ref.shape[0]
    incl = tri_ref[...]
    rowi = lax.broadcasted_iota(jnp.int32, (C, C), 0)
    coli = lax.broadcasted_iota(jnp.int32, (C, C), 1)
    eye = (rowi == coli).astype(F32)

    ab = ab_ref[...]
    g_all = -jnp.exp(alog_ref[...]) * _softplus(ab + dtb_ref[...])
    beta_all = jax.nn.sigmoid(ab)

    conv = []
    for p, ref in enumerate((q_ref, k_ref, v_ref)):
        x = ref[...]
        conv.append(_silu(_causal_conv(x, tail_ref[p], cw_ref[p])))
        tail_ref[p] = x[rows - SUBLANES:]
    qc, kc, vc = conv

    ones = jnp.ones((DK, DK), BF16)

    def unit_rows(x, scale):
        ssq = jnp.concatenate(
            [jnp.dot((x[:, h * DK:(h + 1) * DK] ** 2).astype(BF16), ones, preferred_element_type=F32)
             for h in range(H)], axis=1)
        return x * (lax.rsqrt(ssq + EPS) * scale)
    qc = unit_rows(qc, DK ** -0.5)
    kc = unit_rows(kc, 1.0)

    pairs = [(c, h) for c in range(rows // C) for h in range(H)]
    st = {}
    for c in range(rows // C):
        g_c = g_all[c * C:(c + 1) * C]
        st[c] = (_dot32(incl, g_c),
                 _dotg32(g_c, trit_ref[...], TN_DIMS))
    for c, h in pairs:
        rs = slice(c * C, (c + 1) * C)
        sl = slice(h * DK, (h + 1) * DK)
        q = qc[rs, sl]
        k = kc[rs, sl]
        beta = beta_all[rs, H + h:H + h + 1]
        gc = st[c][0][:, h:h + 1]
        gcr = st[c][1][h:h + 1, :]
        decay = incl * jnp.exp(jnp.where(incl > 0, gc - gcr, 0.0))
        kb = k * beta
        egc = jnp.exp(gc)
        gl = gc[C - 1:C]
        qk = _dotg(jnp.concatenate([q, kb], axis=0), k, NT_DIMS)
        st[c, h] = dict(
            attn=qk[:C] * decay,
            m=qk[C:] * (decay - eye),
            rhs=jnp.concatenate([vc[rs, sl] * beta, kb * egc], axis=1),
            qdec=q * egc,
            kdt=(k * jnp.exp(gl - gc)).T,
            egl=jnp.exp(gl))
    for p in pairs:
        st[p]['inv'] = eye - st[p]['m'] * lvl_ref[0]
    for lv in range(1, lvl_ref.shape[0]):
        for p in pairs:
            st[p]['t'] = _dot(st[p]['m'] * lvl_ref[lv], st[p]['inv'])
        for p in pairs:
            st[p]['inv'] = st[p]['inv'] - _dot(st[p]['inv'], st[p]['t'])
    for p in pairs:
        st[p]['uw'] = _dot(st[p]['inv'], st[p]['rhs'])
    for c, h in pairs:
        d = st[c, h]
        state = state_ref[h]
        ws = _dot(jnp.concatenate([d['uw'][:, DK:], d['qdec']], axis=0), state)
        v_new = d['uw'][:, :DK] - ws[:C]
        o = ws[C:] + _dot(d['attn'], v_new)
        state_ref[h] = state * d['egl'] + _dot(d['kdt'], v_new)
        rs = slice(c * C, (c + 1) * C)
        sl = slice(h * DK, (h + 1) * DK)
        o_ref[rs, sl] = (_rms(o) * ng_ref[...] * _silu(z_ref[rs, sl])).astype(o_ref.dtype)


def gdn(z, z_ab, conv_w, a_log, dt_bias, norm_g):
    S = z.shape[0]
    C = GDN_CHUNK
    H = GDN_HEADS
    lvl, tri, trit = _gdn_tables()
    alog = jnp.zeros((1, LANES), F32).at[0, :H].set(a_log)
    dtb = jnp.zeros((1, LANES), F32).at[0, :H].set(dt_bias)
    cw = conv_w.reshape(CONV_K, 3, HEAD_W).transpose(1, 0, 2)
    tm = min(GDN_ROWS, S)
    col = lambda b: pl.BlockSpec((tm, HEAD_W), lambda n, b=b: (n, b))
    lane = pl.BlockSpec((1, LANES), lambda n: (0, 0))
    return pl.pallas_call(
        _gdn_kernel,
        grid=(S // tm,),
        in_specs=[col(6), col(7), col(8), col(9),
                  pl.BlockSpec((tm, LANES), lambda n: (n, 0)),
                  pl.BlockSpec((3, CONV_K, HEAD_W), lambda n: (0, 0, 0)),
                  lane, lane, lane,
                  pl.BlockSpec(lvl.shape, lambda n: (0, 0, 0)),
                  pl.BlockSpec(tri.shape, lambda n: (0, 0)),
                  pl.BlockSpec(trit.shape, lambda n: (0, 0))],
        out_specs=pl.BlockSpec((tm, HEAD_W), lambda n: (n, 0)),
        out_shape=jax.ShapeDtypeStruct((S, HEAD_W), BF16),
        scratch_shapes=[pltpu.VMEM((3, SUBLANES, HEAD_W), F32),
                        pltpu.VMEM((H, GDN_DK, GDN_DK), F32)],
        compiler_params=_params("arbitrary"),
        name="gdn",
    )(z, z, z, z, z_ab, cw, alog, dtb, norm_g.reshape(1, GDN_DK), lvl, tri, trit)


ROUTE_LANES = LANES


def _merge_kernel(yr_ref, yl_ref, yg_ref, g0_ref, g1_ref, g2_ref, x_ref, gt_ref, bp_ref, wo_ref,
                  ng_ref, sc_ref, sh_ref, wr_ref, br_ref, xo_ref, h_ref, lg_ref):
    merged = g0_ref[...] * jnp.dot(yr_ref[...], bp_ref[0], preferred_element_type=F32)
    merged += g1_ref[...] * jnp.dot(yl_ref[...], bp_ref[1], preferred_element_type=F32)
    merged += g2_ref[...] * jnp.dot(yg_ref[...], bp_ref[2], preferred_element_type=F32)
    out = jnp.dot(merged.astype(BF16), wo_ref[...], preferred_element_type=F32)
    x = x_ref[...] + gt_ref[...] * out
    xo_ref[...] = x
    h = (_rms(x) * ng_ref[...]) * (1.0 + sc_ref[...]) + sh_ref[...]
    h_ref[...] = h
    lg_ref[...] = _dot3(h, wr_ref[...]) + br_ref[...]


def merge(y_ret, y_lru, y_gdn, gates, x, gt1, bp, w_out, norm2_g, sc2, sh2, w_route, b_route):
    S, D = x.shape
    tm = min(256, S)
    ycol = pl.BlockSpec((tm, HEAD_W), lambda i: (i, 0))
    gate = lambda b: pl.BlockSpec((tm, D), lambda i, b=b: (i, b))
    rows = pl.BlockSpec((tm, D), lambda i: (i, 0))
    vec = pl.BlockSpec((1, D), lambda i: (0, 0))
    once = dict(pipeline_mode=pl.Buffered(1))
    return pl.pallas_call(
        _merge_kernel,
        grid=(S // tm,),
        in_specs=[ycol, ycol, ycol, gate(0), gate(1), gate(2), rows, vec,
                  pl.BlockSpec((N_BRANCH, HEAD_W, D), lambda i: (0, 0, 0), **once),
                  pl.BlockSpec((D, D), lambda i: (0, 0), **once),
                  vec, vec, vec,
                  pl.BlockSpec((D, ROUTE_LANES), lambda i: (0, 0)),
                  pl.BlockSpec((1, ROUTE_LANES), lambda i: (0, 0))],
        out_specs=[rows, rows, pl.BlockSpec((tm, ROUTE_LANES), lambda i: (i, 0))],
        out_shape=[jax.ShapeDtypeStruct((S, D), F32), jax.ShapeDtypeStruct((S, D), F32),
                   jax.ShapeDtypeStruct((S, ROUTE_LANES), F32)],
        compiler_params=_params("parallel"),
        name="merge",
    )(y_ret, y_lru, y_gdn, gates, gates, gates, x, gt1, bp, w_out, norm2_g, sc2, sh2, w_route, b_route)


R_E0, R_E1, R_W0, R_W1, R_RANK0, R_RANK1 = range(6)


def _first_max(vals, lane):
    m = jnp.max(vals, axis=-1, keepdims=True)
    idx = jnp.min(jnp.where(vals == m, lane, ROUTE_LANES), axis=-1, keepdims=True)
    return m, idx


def _route_kernel(lg_ref, o_ref, cnt_ref, carry_ref):
    @pl.when(pl.program_id(0) == 0)
    def _():
        carry_ref[...] = jnp.zeros_like(carry_ref)

    lg = lg_ref[...]
    tm = lg.shape[0]
    lane = lax.broadcasted_iota(jnp.int32, lg.shape, 1)
    neg = jnp.float32(-jnp.inf)
    grp = jnp.where(lane < N_GROUPS, lg, neg)
    gmax, gsel = _first_max(grp, lane)
    pg_top = 1.0 / jnp.sum(jnp.exp(grp - gmax), axis=-1, keepdims=True)
    lo = N_GROUPS + gsel * EXPERTS_PER_GROUP
    inside = jnp.where((lane >= lo) & (lane < lo + EXPERTS_PER_GROUP), lg, neg)
    v0, i0 = _first_max(inside, lane)
    v1, i1 = _first_max(jnp.where(lane == i0, neg, inside), lane)
    e1w = jnp.exp(v1 - v0)
    w0 = pg_top / (1.0 + e1w)
    w1 = pg_top * e1w / (1.0 + e1w)
    hit0 = lane == i0
    hit1 = lane == i1
    hits = (hit0 | hit1).astype(BF16)
    row = lax.broadcasted_iota(jnp.int32, (tm, tm), 0)
    col = lax.broadcasted_iota(jnp.int32, (tm, tm), 1)
    before = jnp.dot((row > col).astype(BF16), hits, preferred_element_type=F32) + carry_ref[0:1, :]
    rank0 = jnp.sum(jnp.where(hit0, before, 0.0), axis=-1, keepdims=True)
    rank1 = jnp.sum(jnp.where(hit1, before, 0.0), axis=-1, keepdims=True)
    total = carry_ref[0:1, :] + jnp.sum(hits.astype(F32), axis=0, keepdims=True)
    carry_ref[0:1, :] = total
    cnt_ref[...] = jnp.broadcast_to(total, cnt_ref.shape)
    rec = jnp.zeros(lg.shape, F32)
    for slot, val in ((R_E0, (i0 - N_GROUPS).astype(F32)), (R_E1, (i1 - N_GROUPS).astype(F32)),
                      (R_W0, w0), (R_W1, w1), (R_RANK0, rank0), (R_RANK1, rank1)):
        rec = jnp.where(lane == slot, val, rec)
    o_ref[...] = rec


def route(logits):
    S = logits.shape[0]
    tm = min(256, S)
    return pl.pallas_call(
        _route_kernel,
        grid=(S // tm,),
        in_specs=[pl.BlockSpec((tm, ROUTE_LANES), lambda i: (i, 0))],
        out_specs=[pl.BlockSpec((tm, ROUTE_LANES), lambda i: (i, 0)),
                   pl.BlockSpec((SUBLANES, ROUTE_LANES), lambda i: (0, 0))],
        out_shape=[jax.ShapeDtypeStruct((S, ROUTE_LANES), F32),
                   jax.ShapeDtypeStruct((SUBLANES, ROUTE_LANES), F32)],
        scratch_shapes=[pltpu.VMEM((SUBLANES, ROUTE_LANES), F32)],
        compiler_params=_params("arbitrary"),
        name="route",
    )(logits)


def _expert_kernel(be_ref, nused_ref, src_ref, src1_ref, src2_ref, h_hbm, w1_ref, w3_ref, w2_ref, o_ref,
                   xbuf, wb1, wb3, wb2, gsem):
    i = pl.program_id(0)
    nused = nused_ref[0]
    B = EXPERT_ROWS
    slot = i % GATHER_DEPTH

    def gather(idx_ref, s):
        for r in range(B):
            pltpu.make_async_copy(h_hbm.at[pl.ds(idx_ref[0, r], 1)], xbuf.at[s, pl.ds(r, 1)],
                                  gsem.at[s]).start()

    @pl.when(i == 0)
    def _():
        gather(src_ref, 0)

        @pl.when(nused > 1)
        def _():
            gather(src1_ref, 1)

    @pl.when(i < nused)
    def _():
        @pl.when(i + 2 < nused)
        def _():
            gather(src2_ref, (i + 2) % GATHER_DEPTH)

        @pl.when((i == 0) | (be_ref[i] != be_ref[jnp.maximum(i - 1, 0)]))
        def _():
            wb1[...] = w1_ref[...].astype(BF16)
            wb3[...] = w3_ref[...].astype(BF16)
            wb2[...] = w2_ref[...].astype(BF16)

        pltpu.make_async_copy(h_hbm.at[pl.ds(0, B)], xbuf.at[slot], gsem.at[slot]).wait()
        x = xbuf[slot].astype(BF16)
        a1 = jnp.dot(x, wb1[...], preferred_element_type=F32)
        a3 = jnp.dot(x, wb3[...], preferred_element_type=F32)
        mid = (_silu(a1) * a3).astype(BF16)
        o_ref[...] = jnp.dot(mid, wb2[...], preferred_element_type=F32)

    @pl.when(i >= nused)
    def _():
        o_ref[...] = jnp.zeros(o_ref.shape, o_ref.dtype)


def experts(h, block_e, nused, src_row, w1, w3, w2, layer):
    T, D = h.shape
    nb = block_e.shape[0]
    B = EXPERT_ROWS
    idx = lambda f: pl.BlockSpec((None, 1, B), f, memory_space=pltpu.SMEM)
    wspec = lambda shp: pl.BlockSpec((None, None) + shp, lambda i, be, nu: (layer, be[i], 0, 0))
    grid_spec = pltpu.PrefetchScalarGridSpec(
        num_scalar_prefetch=2,
        grid=(nb,),
        in_specs=[idx(lambda i, be, nu: (i, 0, 0)),
                  idx(lambda i, be, nu: (jnp.minimum(i + 1, nb - 1), 0, 0)),
                  idx(lambda i, be, nu: (jnp.minimum(i + 2, nb - 1), 0, 0)),
                  pl.BlockSpec(memory_space=pl.ANY),
                  wspec((D, D_EXPERT)), wspec((D, D_EXPERT)), wspec((D_EXPERT, D))],
        out_specs=pl.BlockSpec((B, D), lambda i, be, nu: (i, 0)),
        scratch_shapes=[pltpu.VMEM((GATHER_DEPTH, B, D), F32),
                        pltpu.VMEM((D, D_EXPERT), BF16), pltpu.VMEM((D, D_EXPERT), BF16),
                        pltpu.VMEM((D_EXPERT, D), BF16),
                        pltpu.SemaphoreType.DMA((GATHER_DEPTH,))])
    src3 = src_row.reshape(nb, 1, B)
    return pl.pallas_call(
        _expert_kernel,
        grid_spec=grid_spec,
        out_shape=jax.ShapeDtypeStruct((nb * B, D), F32),
        compiler_params=_params("arbitrary"),
        name="experts",
    )(block_e, nused, src3, src3, src3, h, w1, w3, w2)


def dispatch_plan(rec, counts, T):
    B = EXPERT_ROWS
    A = 2 * T
    nb = A // B + N_EXPERTS
    cnt = counts[0, N_GROUPS:N_GROUPS + N_EXPERTS].astype(jnp.int32)
    padded = ((cnt + B - 1) // B) * B
    pend = jnp.cumsum(padded)
    pstart = pend - padded
    e = rec[:, R_E0:R_E1 + 1].astype(jnp.int32)
    rank = rec[:, R_RANK0:R_RANK1 + 1].astype(jnp.int32)
    dest = pstart[e] + rank
    tok = jnp.broadcast_to(jnp.arange(T, dtype=jnp.int32)[:, None], (T, 2))
    src_row = jnp.zeros((nb * B,), jnp.int32).at[dest.reshape(-1)].set(tok.reshape(-1))
    first = jnp.arange(nb, dtype=jnp.int32) * B
    block_e = jnp.minimum(jnp.sum((first[:, None] >= pend[None, :]).astype(jnp.int32), axis=1),
                          N_EXPERTS - 1)
    nused = (pend[-1] // B).astype(jnp.int32).reshape(1)
    return block_e, nused, src_row, dest


def _combine_kernel(final, d0_ref, d1_ref, nd0_ref, nd1_ref, x_ref, rec_ref, gt_ref, fg_ref, y_hbm,
                    o_ref, ybuf, sem):
    i = pl.program_id(0)
    tm = x_ref.shape[0]
    slot = i % 2

    def fetch(r0_ref, r1_ref, s):
        for k, idx_ref in enumerate((r0_ref, r1_ref)):
            for r in range(tm):
                pltpu.make_async_copy(y_hbm.at[pl.ds(idx_ref[0, r], 1)], ybuf.at[s, k, pl.ds(r, 1)],
                                      sem.at[s]).start()

    @pl.when(i == 0)
    def _():
        fetch(d0_ref, d1_ref, 0)

    @pl.when(i + 1 < pl.num_programs(0))
    def _():
        fetch(nd0_ref, nd1_ref, 1 - slot)

    for k in range(2):
        pltpu.make_async_copy(y_hbm.at[pl.ds(0, tm)], ybuf.at[slot, k], sem.at[slot]).wait()
    rec = rec_ref[...]
    moe = rec[:, R_W0:R_W0 + 1] * ybuf[slot, 0] + rec[:, R_W1:R_W1 + 1] * ybuf[slot, 1]
    x = x_ref[...] + gt_ref[...] * moe
    o_ref[...] = _rms(x) * fg_ref[...] if final else x


def combine(x, y, dest, rec, gt2, final_g, final):
    S, D = x.shape
    tm = min(COMBINE_ROWS, S)
    nt = S // tm
    d0 = dest[:, 0].reshape(nt, 1, tm)
    d1 = dest[:, 1].reshape(nt, 1, tm)
    rows = pl.BlockSpec((tm, D), lambda i: (i, 0))
    vec = pl.BlockSpec((1, D), lambda i: (0, 0))
    cur = pl.BlockSpec((None, 1, tm), lambda i: (i, 0, 0), memory_space=pltpu.SMEM)
    nxt = pl.BlockSpec((None, 1, tm), lambda i: (jnp.minimum(i + 1, nt - 1), 0, 0), memory_space=pltpu.SMEM)
    return pl.pallas_call(
        functools.partial(_combine_kernel, final),
        grid=(nt,),
        in_specs=[cur, cur, nxt, nxt, rows, pl.BlockSpec((tm, ROUTE_LANES), lambda i: (i, 0)), vec, vec,
                  pl.BlockSpec(memory_space=pl.ANY)],
        out_specs=rows,
        out_shape=jax.ShapeDtypeStruct((S, D), F32),
        scratch_shapes=[pltpu.VMEM((2, 2, tm, D), F32), pltpu.SemaphoreType.DMA((2,))],
        compiler_params=_params("arbitrary"),
        name="combine",
    )(d0, d1, d0, d1, x, rec, gt2, final_g, y)


def _rope_tables(positions):
    half = RET_DK // 2
    inv_freq = ROPE_BASE ** (-jnp.arange(half, dtype=F32) * (2.0 / RET_DK))
    ang = positions.astype(F32)[:, None] * inv_freq
    cos = jnp.cos(ang)
    sin = jnp.sin(ang)
    return jnp.concatenate([cos, cos], axis=-1), jnp.concatenate([-sin, sin], axis=-1)


def _proj_weights(w_in):
    L, D, _ = w_in.shape
    w_all = w_in[:, :, :MAIN_COLS].astype(BF16)
    ab = w_in[:, :, MAIN_COLS:MAIN_COLS + N_AB]
    w_ab = jnp.concatenate([ab, jnp.zeros((L, D, LANES - N_AB), w_in.dtype)], axis=2).astype(BF16)
    w_gates = w_in[:, :, MAIN_COLS + N_AB:].astype(BF16)
    return w_all, w_ab, w_gates


def kernel(x, c, positions, ada_w, ada_b, norm1_g, norm2_g, w_in, lru_conv_w, lru_conv_b, lru_wa, lru_ba,
           lru_wx, lru_bx, lru_lambda, gdn_conv_w, gdn_a_log, gdn_dt_bias, gdn_norm_g, branch_proj, w_out,
           router_grp_w, router_grp_b, router_exp_w, router_exp_b, moe_w1, moe_w3, moe_w2, final_g):
    B, S, D = x.shape
    assert B == 1
    xs = x.reshape(S, D)
    mod = ada_mod(c, ada_w, ada_b)
    cos2, sin2 = _rope_tables(positions.reshape(S))
    row = lambda v: v.reshape(1, -1).astype(F32)
    w_all, w_ab, w_gates = _proj_weights(w_in)
    for l in range(DEPTH):
        sh1, sc1, gt1, sh2, sc2, gt2 = (row(m) for m in jnp.split(mod[l], 6))
        z, z_ab = norm_proj(xs, row(norm1_g[l]), sc1, sh1, w_all, w_ab, l)
        gates = gate_proj(xs, row(norm1_g[l]), sc1, sh1, w_gates, l)
        y_ret = retention(z, cos2, sin2)
        y_lru = rglru(z, lru_conv_w[l], lru_conv_b[l], lru_wa[l], lru_ba[l], lru_wx[l], lru_bx[l],
                      lru_lambda[l])
        y_gdn = gdn(z, z_ab, gdn_conv_w[l], gdn_a_log[l], gdn_dt_bias[l], gdn_norm_g[l])
        n_r = N_GROUPS + N_EXPERTS
        w_route = jnp.concatenate([router_grp_w[l], router_exp_w[l],
                                   jnp.zeros((D, ROUTE_LANES - n_r), F32)], axis=1)
        b_route = jnp.concatenate([router_grp_b[l], router_exp_b[l],
                                   jnp.zeros((ROUTE_LANES - n_r,), F32)]).reshape(1, ROUTE_LANES)
        xs, h2, logits = merge(y_ret, y_lru, y_gdn, gates, xs, gt1, branch_proj[l].astype(BF16),
                               w_out[l].astype(BF16), row(norm2_g[l]), sc2, sh2, w_route, b_route)
        rec, counts = route(logits)
        block_e, nused, src_row, dest = dispatch_plan(rec, counts, S)
        y = experts(h2, block_e, nused, src_row, moe_w1, moe_w3, moe_w2, l)
        xs = combine(xs, y, dest, rec, gt2, row(final_g), final=(l == DEPTH - 1))
    return xs.reshape(B, S, D)
```

```python
import functools
import math

import numpy as np
import jax
import jax.numpy as jnp
from jax import lax
from jax.experimental import pallas as pl
from jax.experimental.pallas import tpu as pltpu

F32 = jnp.float32
BF16 = jnp.bfloat16
HIGHEST = lax.Precision.HIGHEST

D_MODEL = 2048
DEPTH = 2
RET_HEADS = 8
RET_DK = 128
RET_CHUNK = 128
ROPE_BASE = 10000.0
LRU_W = 1024
LRU_BLOCKS = 8
LRU_BS = LRU_W // LRU_BLOCKS
LRU_C = 8.0
CONV_K = 4
GDN_HEADS = 8
GDN_DK = 128
GDN_CHUNK = 64
N_BRANCH = 3
N_GROUPS = 4
EXPERTS_PER_GROUP = 8
N_EXPERTS = N_GROUPS * EXPERTS_PER_GROUP
D_EXPERT = 512
EPS = 1e-6

HEAD_W = 1024
MAIN_COLS = 10 * HEAD_W
N_AB = 2 * GDN_HEADS
GATE_COLS = N_BRANCH * D_MODEL
MOD_TN = 1536
PROJ_ROWS = 1024
PROJ_TN = 1280
LRU_ROWS = 256
MERGE_ROWS = 256
ROUTE_ROWS = 256
GATE_TN = 1024
GATE_ROW_CHUNK = 256
LANES = 128
SUBLANES = 8
EXPERT_ROWS = 256
COMBINE_ROWS = 256
GATHER_DEPTH = 3
GDN_ROWS = 2 * GDN_CHUNK
RET_ROWS = 4 * RET_CHUNK
VMEM_LIMIT = 56 * 1024 * 1024

NT_DIMS = (((1,), (1,)), ((), ()))
TN_DIMS = (((0,), (0,)), ((), ()))


def _params(*sem):
    return pltpu.CompilerParams(dimension_semantics=sem, vmem_limit_bytes=VMEM_LIMIT)


def _dot32(a, b):
    return jnp.dot(a, b, preferred_element_type=F32, precision=HIGHEST)


def _dotg32(a, b, dims):
    return lax.dot_general(a, b, dims, preferred_element_type=F32, precision=HIGHEST)


def _dot(a, b):
    return jnp.dot(a.astype(BF16), b.astype(BF16), preferred_element_type=F32)


def _dotg(a, b, dims):
    return lax.dot_general(a.astype(BF16), b.astype(BF16), dims, preferred_element_type=F32)


def _dot3(a, b):
    ah = a.astype(BF16)
    bh = b.astype(BF16)
    al = (a - ah.astype(F32)).astype(BF16)
    bl = (b - bh.astype(F32)).astype(BF16)
    d = functools.partial(jnp.dot, preferred_element_type=F32)
    return d(ah, bh) + (d(ah, bl) + d(al, bh))


def _silu(x):
    return x * jax.nn.sigmoid(x)


def _softplus(x):
    return jnp.maximum(x, 0.0) + jnp.log1p(jnp.exp(-jnp.abs(x)))


def _rms(x):
    return x * lax.rsqrt(jnp.mean(x * x, axis=-1, keepdims=True) + EPS)


def _shift_rows(x, s, fill):
    n = x.shape[0]
    if s % SUBLANES == 0:
        return jnp.concatenate([jnp.full((s, x.shape[1]), fill, x.dtype), x[:n - s]], axis=0)
    row = lax.broadcasted_iota(jnp.int32, x.shape, 0)
    return jnp.where(row < s, fill, pltpu.roll(x, s, 0))


def _causal_conv(x, tail, w):
    n = x.shape[0]
    row = lax.broadcasted_iota(jnp.int32, (SUBLANES, x.shape[1]), 0)
    y = x * w[CONV_K - 1:CONV_K]
    for s in range(1, CONV_K):
        xs = pltpu.roll(x, s, 0)
        head = jnp.where(row < s, pltpu.roll(tail, s, 0), xs[:SUBLANES])
        xs = jnp.concatenate([head, xs[SUBLANES:]], axis=0) if n > SUBLANES else head
        y = y + xs * w[CONV_K - 1 - s:CONV_K - s]
    return y


def _mod_kernel(c_ref, w_ref, b_ref, o_ref):
    c = c_ref[...]
    o_ref[...] = _dot3(_silu(c), w_ref[...]) + b_ref[...]


def ada_mod(c, ada_w, ada_b):
    L, D, N = ada_w.shape
    tn = MOD_TN
    c8 = jnp.broadcast_to(c.astype(F32), (SUBLANES, D))
    out = pl.pallas_call(
        _mod_kernel,
        grid=(L, N // tn),
        in_specs=[pl.BlockSpec((SUBLANES, D), lambda l, j: (0, 0)),
                  pl.BlockSpec((None, D, tn), lambda l, j: (l, 0, j)),
                  pl.BlockSpec((None, 1, tn), lambda l, j: (l, 0, j))],
        out_specs=pl.BlockSpec((None, SUBLANES, tn), lambda l, j: (l, 0, j)),
        out_shape=jax.ShapeDtypeStruct((L, SUBLANES, N), F32),
        compiler_params=_params("parallel", "parallel"),
        name="ada_mod",
    )(c8, ada_w, ada_b.reshape(L, 1, N))
    return out[:, 0, :]


def _norm_proj_kernel(x_ref, g_ref, sc_ref, sh_ref, w_ref, wab_ref, o_ref, ab_ref, h_ref):
    @pl.when(pl.program_id(1) == 0)
    def _():
        h = (_rms(x_ref[...]) * g_ref[...]) * (1.0 + sc_ref[...]) + sh_ref[...]
        h_ref[...] = h.astype(BF16)
        ab_ref[...] = jnp.dot(h_ref[...], wab_ref[...], preferred_element_type=F32)

    o_ref[...] = jnp.dot(h_ref[...], w_ref[...], preferred_element_type=F32)


def norm_proj(x, g, sc, sh, w, w_ab, layer):
    S, D = x.shape
    tm = min(PROJ_ROWS, S)
    vec = pl.BlockSpec((1, D), lambda i, j: (0, 0))
    return pl.pallas_call(
        _norm_proj_kernel,
        grid=(S // tm, MAIN_COLS // PROJ_TN),
        in_specs=[pl.BlockSpec((tm, D), lambda i, j: (i, 0)), vec, vec, vec,
                  pl.BlockSpec((None, D, PROJ_TN), lambda i, j: (layer, 0, j)),
                  pl.BlockSpec((None, D, LANES), lambda i, j: (layer, 0, 0))],
        out_specs=[pl.BlockSpec((tm, PROJ_TN), lambda i, j: (i, j)),
                   pl.BlockSpec((tm, LANES), lambda i, j: (i, 0))],
        out_shape=[jax.ShapeDtypeStruct((S, MAIN_COLS), F32), jax.ShapeDtypeStruct((S, LANES), F32)],
        scratch_shapes=[pltpu.VMEM((tm, D), BF16)],
        compiler_params=_params("parallel", "arbitrary"),
        name="norm_proj",
    )(x, g, sc, sh, w, w_ab)


def _gate_proj_kernel(x_ref, g_ref, sc_ref, sh_ref, w_ref, o_ref, h_ref):
    @pl.when(pl.program_id(1) == 0)
    def _():
        h = (_rms(x_ref[...]) * g_ref[...]) * (1.0 + sc_ref[...]) + sh_ref[...]
        h_ref[...] = h.astype(BF16)

    rows = GATE_ROW_CHUNK
    for r in range(h_ref.shape[0] // rows):
        rs = slice(r * rows, (r + 1) * rows)
        o_ref[rs] = jax.nn.sigmoid(jnp.dot(h_ref[rs], w_ref[...], preferred_element_type=F32)).astype(BF16)


def gate_proj(x, g, sc, sh, w_gates, layer):
    S, D = x.shape
    tm = min(PROJ_ROWS, S)
    tn = GATE_TN
    vec = pl.BlockSpec((1, D), lambda i, j: (0, 0))
    return pl.pallas_call(
        _gate_proj_kernel,
        grid=(S // tm, GATE_COLS // tn),
        in_specs=[pl.BlockSpec((tm, D), lambda i, j: (i, 0)), vec, vec, vec,
                  pl.BlockSpec((None, D, tn), lambda i, j: (layer, 0, j))],
        out_specs=pl.BlockSpec((tm, tn), lambda i, j: (i, j)),
        out_shape=jax.ShapeDtypeStruct((S, GATE_COLS), BF16),
        scratch_shapes=[pltpu.VMEM((tm, D), BF16)],
        compiler_params=_params("parallel", "arbitrary"),
        name="gate_proj",
    )(x, g, sc, sh, w_gates)


def _ret_tables():
    H, C = RET_HEADS, RET_CHUNK
    log_g = np.log1p(-np.exp2(-5.0 - np.arange(H, dtype=np.float64)))
    pos = np.arange(C, dtype=np.float64)
    rel = pos[:, None] - pos[None, :]
    intra = np.where(rel >= 0, np.exp(log_g[:, None, None] * np.maximum(rel, 0.0)), 0.0)
    qd = np.exp(log_g[:, None] * (pos + 1.0))
    kd = np.exp(log_g[:, None] * (C - 1.0 - pos))
    ones = np.ones((1, 1, LANES))
    cd = [float(np.exp(lg * C)) for lg in log_g]
    return (jnp.asarray(intra, F32), jnp.asarray(qd[:, :, None] * ones, F32),
            jnp.asarray(kd[:, :, None] * ones, F32), cd)


def _ret_kernel(cd, q_ref, k_ref, v_ref, g_ref, cos_ref, sin_ref, intra_ref, qd_ref, kd_ref,
                o_ref, state_ref):
    @pl.when(pl.program_id(0) == 0)
    def _():
        state_ref[...] = jnp.zeros_like(state_ref)

    C = RET_CHUNK
    half = RET_DK // 2
    for c in range(q_ref.shape[0] // C):
        rs = slice(c * C, (c + 1) * C)
        cos2 = cos_ref[rs]
        sin2 = sin_ref[rs]
        for h in range(RET_HEADS):
            sl = slice(h * RET_DK, (h + 1) * RET_DK)
            q = q_ref[rs, sl]
            k = k_ref[rs, sl]
            v = v_ref[rs, sl]
            q = q * cos2 + pltpu.roll(q, half, 1) * sin2
            k = (k * cos2 + pltpu.roll(k, half, 1) * sin2) * (RET_DK ** -0.5)
            scores = _dotg(q, k, NT_DIMS) * intra_ref[h]
            state = state_ref[h]
            o = _dot(scores, v) + _dot(q * qd_ref[h], state)
            state_ref[h] = state * cd[h] + _dotg(k * kd_ref[h], v, TN_DIMS)
            o_ref[rs, sl] = (_rms(o) * _silu(g_ref[rs, sl])).astype(o_ref.dtype)


def retention(z, cos2, sin2):
    S = z.shape[0]
    C = min(RET_ROWS, S)
    intra, qd, kd, cd = _ret_tables()
    col = lambda b: pl.BlockSpec((C, HEAD_W), lambda n, b=b: (n, b))
    tab = lambda a: pl.BlockSpec(a.shape, lambda n: (0, 0, 0))
    rot = pl.BlockSpec((C, RET_DK), lambda n: (n, 0))
    return pl.pallas_call(
        functools.partial(_ret_kernel, cd),
        grid=(S // C,),
        in_specs=[col(0), col(1), col(2), col(3), rot, rot, tab(intra), tab(qd), tab(kd)],
        out_specs=pl.BlockSpec((C, HEAD_W), lambda n: (n, 0)),
        out_shape=jax.ShapeDtypeStruct((S, HEAD_W), BF16),
        scratch_shapes=[pltpu.VMEM((RET_HEADS, RET_DK, RET_DK), F32)],
        compiler_params=_params("arbitrary"),
        name="retention",
    )(z, z, z, z, cos2, sin2, intra, qd, kd)


def _lru_kernel(x_ref, gate_ref, cw_ref, cb_ref, wa_ref, ba_ref, wx_ref, bx_ref, lam_ref,
                o_ref, tail_ref, carry_ref):
    @pl.when(pl.program_id(0) == 0)
    def _():
        tail_ref[...] = jnp.zeros_like(tail_ref)
        carry_ref[...] = jnp.zeros_like(carry_ref)

    tm = x_ref.shape[0]
    for j in range(LRU_BLOCKS):
        sl = slice(j * LRU_BS, (j + 1) * LRU_BS)
        x = x_ref[:, sl]
        xc = _causal_conv(x, tail_ref[:, sl], cw_ref[:, sl]) + cb_ref[:, sl]
        tail_ref[:, sl] = x[tm - SUBLANES:]
        r = jax.nn.sigmoid(_dot(xc, wa_ref[j]) + ba_ref[:, sl])
        i = jax.nn.sigmoid(_dot(xc, wx_ref[j]) + bx_ref[:, sl])
        log_a = (-LRU_C * r) * _softplus(-lam_ref[:, sl])
        a = jnp.exp(log_a)
        t = jnp.tanh(log_a)
        b = jnp.sqrt(-2.0 * t / (1.0 - t)) * (i * xc)
        s = 1
        while s < tm:
            b = b + a * _shift_rows(b, s, 0.0)
            a = a * _shift_rows(a, s, 1.0)
            s *= 2
        h = b + a * carry_ref[0:1, sl]
        carry_ref[0:1, sl] = h[tm - 1:tm]
        g = gate_ref[:, sl]
        gelu = g * (0.5 * (1.0 + jnp.tanh(math.sqrt(2.0 / math.pi) * (g + 0.044715 * (g * g * g)))))
        o_ref[:, sl] = (h * gelu).astype(o_ref.dtype)


def rglru(z, conv_w, conv_b, wa, ba, wx, bx, lam):
    S = z.shape[0]
    tm = min(LRU_ROWS, S)
    row = lambda a: a.reshape(1, LRU_W)
    vec = pl.BlockSpec((1, LRU_W), lambda n: (0, 0))
    wspec = pl.BlockSpec((LRU_BLOCKS, LRU_BS, LRU_BS), lambda n: (0, 0, 0))
    return pl.pallas_call(
        _lru_kernel,
        grid=(S // tm,),
        in_specs=[pl.BlockSpec((tm, HEAD_W), lambda n: (n, 4)),
                  pl.BlockSpec((tm, HEAD_W), lambda n: (n, 5)),
                  pl.BlockSpec((CONV_K, LRU_W), lambda n: (0, 0)), vec, wspec, vec, wspec, vec, vec],
        out_specs=pl.BlockSpec((tm, HEAD_W), lambda n: (n, 0)),
        out_shape=jax.ShapeDtypeStruct((S, HEAD_W), BF16),
        scratch_shapes=[pltpu.VMEM((SUBLANES, LRU_W), F32), pltpu.VMEM((SUBLANES, LRU_W), F32)],
        compiler_params=_params("arbitrary"),
        name="rglru",
    )(z, z, conv_w, row(conv_b), wa, row(ba), wx, row(bx), row(lam))


def _gdn_tables():
    C = GDN_CHUNK
    i = np.arange(C)[:, None]
    j = np.arange(C)[None, :]
    levels = []
    s = 1
    while s < C:
        levels.append((i // (2 * s) == j // (2 * s)) & (i % (2 * s) >= s) & (j % (2 * s) < s))
        s *= 2
    return jnp.asarray(np.stack(levels), F32), jnp.asarray(i >= j, F32), jnp.asarray(i <= j, F32)


def _gdn_kernel(q_ref, k_ref, v_ref, z_ref, ab_ref, cw_ref, alog_ref, dtb_ref, ng_ref, lvl_ref, tri_ref,
                trit_ref, o_ref, tail_ref, state_ref):
    @pl.when(pl.program_id(0) == 0)
    def _():
        tail_ref[...] = jnp.zeros_like(tail_ref)
        state_ref[...] = jnp.zeros_like(state_ref)

    C = GDN_CHUNK
    H = GDN_HEADS
    DK = GDN_DK
    rows = q_ref.shape[0]
    incl = tri_ref[...]
    rowi = lax.broadcasted_iota(jnp.int32, (C, C), 0)
    coli = lax.broadcasted_iota(jnp.int32, (C, C), 1)
    eye = (rowi == coli).astype(F32)

    ab = ab_ref[...]
    g_all = -jnp.exp(alog_ref[...]) * _softplus(ab + dtb_ref[...])
    beta_all = jax.nn.sigmoid(ab)

    conv = []
    for p, ref in enumerate((q_ref, k_ref, v_ref)):
        x = ref[...]
        conv.append(_silu(_causal_conv(x, tail_ref[p], cw_ref[p])))
        tail_ref[p] = x[rows - SUBLANES:]
    qc, kc, vc = conv

    ones = jnp.ones((DK, DK), BF16)

    def unit_rows(x, scale):
        ssq = jnp.concatenate(
            [jnp.dot((x[:, h * DK:(h + 1) * DK] ** 2).astype(BF16), ones, preferred_element_type=F32)
             for h in range(H)], axis=1)
        return x * (lax.rsqrt(ssq + EPS) * scale)
    qc = unit_rows(qc, DK ** -0.5)
    kc = unit_rows(kc, 1.0)

    pairs = [(c, h) for c in range(rows // C) for h in range(H)]
    st = {}
    for c in range(rows // C):
        g_c = g_all[c * C:(c + 1) * C]
        st[c] = (_dot32(incl, g_c),
                 _dotg32(g_c, trit_ref[...], TN_DIMS))
    for c, h in pairs:
        rs = slice(c * C, (c + 1) * C)
        sl = slice(h * DK, (h + 1) * DK)
        q = qc[rs, sl]
        k = kc[rs, sl]
        beta = beta_all[rs, H + h:H + h + 1]
        gc = st[c][0][:, h:h + 1]
        gcr = st[c][1][h:h + 1, :]
        decay = incl * jnp.exp(jnp.where(incl > 0, gc - gcr, 0.0))
        kb = k * beta
        egc = jnp.exp(gc)
        gl = gc[C - 1:C]
        qk = _dotg(jnp.concatenate([q, kb], axis=0), k, NT_DIMS)
        st[c, h] = dict(
            attn=qk[:C] * decay,
            m=qk[C:] * (decay - eye),
            rhs=jnp.concatenate([vc[rs, sl] * beta, kb * egc], axis=1),
            qdec=q * egc,
            kdt=(k * jnp.exp(gl - gc)).T,
            egl=jnp.exp(gl))
    for p in pairs:
        st[p]['inv'] = eye - st[p]['m'] * lvl_ref[0]
    for lv in range(1, lvl_ref.shape[0]):
        for p in pairs:
            st[p]['t'] = _dot(st[p]['m'] * lvl_ref[lv], st[p]['inv'])
        for p in pairs:
            st[p]['inv'] = st[p]['inv'] - _dot(st[p]['inv'], st[p]['t'])
    for p in pairs:
        st[p]['uw'] = _dot(st[p]['inv'], st[p]['rhs'])
    for c, h in pairs:
        d = st[c, h]
        state = state_ref[h]
        ws = _dot(jnp.concatenate([d['uw'][:, DK:], d['qdec']], axis=0), state)
        v_new = d['uw'][:, :DK] - ws[:C]
        o = ws[C:] + _dot(d['attn'], v_new)
        state_ref[h] = state * d['egl'] + _dot(d['kdt'], v_new)
        rs = slice(c * C, (c + 1) * C)
        sl = slice(h * DK, (h + 1) * DK)
        o_ref[rs, sl] = (_rms(o) * ng_ref[...] * _silu(z_ref[rs, sl])).astype(o_ref.dtype)


def gdn(z, z_ab, conv_w, a_log, dt_bias, norm_g):
    S = z.shape[0]
    C = GDN_CHUNK
    H = GDN_HEADS
    lvl, tri, trit = _gdn_tables()
    alog = jnp.zeros((1, LANES), F32).at[0, :H].set(a_log)
    dtb = jnp.zeros((1, LANES), F32).at[0, :H].set(dt_bias)
    cw = conv_w.reshape(CONV_K, 3, HEAD_W).transpose(1, 0, 2)
    tm = min(GDN_ROWS, S)
    col = lambda b: pl.BlockSpec((tm, HEAD_W), lambda n, b=b: (n, b))
    lane = pl.BlockSpec((1, LANES), lambda n: (0, 0))
    return pl.pallas_call(
        _gdn_kernel,
        grid=(S // tm,),
        in_specs=[col(6), col(7), col(8), col(9),
                  pl.BlockSpec((tm, LANES), lambda n: (n, 0)),
                  pl.BlockSpec((3, CONV_K, HEAD_W), lambda n: (0, 0, 0)),
                  lane, lane, lane,
                  pl.BlockSpec(lvl.shape, lambda n: (0, 0, 0)),
                  pl.BlockSpec(tri.shape, lambda n: (0, 0)),
                  pl.BlockSpec(trit.shape, lambda n: (0, 0))],
        out_specs=pl.BlockSpec((tm, HEAD_W), lambda n: (n, 0)),
        out_shape=jax.ShapeDtypeStruct((S, HEAD_W), BF16),
        scratch_shapes=[pltpu.VMEM((3, SUBLANES, HEAD_W), F32),
                        pltpu.VMEM((H, GDN_DK, GDN_DK), F32)],
        compiler_params=_params("arbitrary"),
        name="gdn",
    )(z, z, z, z, z_ab, cw, alog, dtb, norm_g.reshape(1, GDN_DK), lvl, tri, trit)


ROUTE_LANES = LANES


def _merge_kernel(yr_ref, yl_ref, yg_ref, g0_ref, g1_ref, g2_ref, x_ref, gt_ref, bp_ref, wo_ref,
                  ng_ref, sc_ref, sh_ref, wr_ref, br_ref, xo_ref, h_ref, lg_ref):
    merged = g0_ref[...] * jnp.dot(yr_ref[...], bp_ref[0], preferred_element_type=F32)
    merged += g1_ref[...] * jnp.dot(yl_ref[...], bp_ref[1], preferred_element_type=F32)
    merged += g2_ref[...] * jnp.dot(yg_ref[...], bp_ref[2], preferred_element_type=F32)
    out = jnp.dot(merged.astype(BF16), wo_ref[...], preferred_element_type=F32)
    x = x_ref[...] + gt_ref[...] * out
    xo_ref[...] = x
    h = (_rms(x) * ng_ref[...]) * (1.0 + sc_ref[...]) + sh_ref[...]
    h_ref[...] = h
    lg_ref[...] = _dot3(h, wr_ref[...]) + br_ref[...]


def merge(y_ret, y_lru, y_gdn, gates, x, gt1, bp, w_out, norm2_g, sc2, sh2, w_route, b_route):
    S, D = x.shape
    tm = min(MERGE_ROWS, S)
    ycol = pl.BlockSpec((tm, HEAD_W), lambda i: (i, 0))
    gate = lambda b: pl.BlockSpec((tm, D), lambda i, b=b: (i, b))
    rows = pl.BlockSpec((tm, D), lambda i: (i, 0))
    vec = pl.BlockSpec((1, D), lambda i: (0, 0))
    once = dict(pipeline_mode=pl.Buffered(1))
    return pl.pallas_call(
        _merge_kernel,
        grid=(S // tm,),
        in_specs=[ycol, ycol, ycol, gate(0), gate(1), gate(2), rows, vec,
                  pl.BlockSpec((N_BRANCH, HEAD_W, D), lambda i: (0, 0, 0), **once),
                  pl.BlockSpec((D, D), lambda i: (0, 0), **once),
                  vec, vec, vec,
                  pl.BlockSpec((D, ROUTE_LANES), lambda i: (0, 0)),
                  pl.BlockSpec((1, ROUTE_LANES), lambda i: (0, 0))],
        out_specs=[rows, rows, pl.BlockSpec((tm, ROUTE_LANES), lambda i: (i, 0))],
        out_shape=[jax.ShapeDtypeStruct((S, D), F32), jax.ShapeDtypeStruct((S, D), F32),
                   jax.ShapeDtypeStruct((S, ROUTE_LANES), F32)],
        compiler_params=_params("parallel"),
        name="merge",
    )(y_ret, y_lru, y_gdn, gates, gates, gates, x, gt1, bp, w_out, norm2_g, sc2, sh2, w_route, b_route)


R_E0, R_E1, R_W0, R_W1, R_RANK0, R_RANK1 = range(6)


def _first_max(vals, lane):
    m = jnp.max(vals, axis=-1, keepdims=True)
    idx = jnp.min(jnp.where(vals == m, lane, ROUTE_LANES), axis=-1, keepdims=True)
    return m, idx


def _route_kernel(lg_ref, o_ref, cnt_ref, carry_ref):
    @pl.when(pl.program_id(0) == 0)
    def _():
        carry_ref[...] = jnp.zeros_like(carry_ref)

    lg = lg_ref[...]
    tm = lg.shape[0]
    lane = lax.broadcasted_iota(jnp.int32, lg.shape, 1)
    neg = jnp.float32(-jnp.inf)
    grp = jnp.where(lane < N_GROUPS, lg, neg)
    gmax, gsel = _first_max(grp, lane)
    pg_top = 1.0 / jnp.sum(jnp.exp(grp - gmax), axis=-1, keepdims=True)
    lo = N_GROUPS + gsel * EXPERTS_PER_GROUP
    inside = jnp.where((lane >= lo) & (lane < lo + EXPERTS_PER_GROUP), lg, neg)
    v0, i0 = _first_max(inside, lane)
    v1, i1 = _first_max(jnp.where(lane == i0, neg, inside), lane)
    e1w = jnp.exp(v1 - v0)
    w0 = pg_top / (1.0 + e1w)
    w1 = pg_top * e1w / (1.0 + e1w)
    hit0 = lane == i0
    hit1 = lane == i1
    hits = (hit0 | hit1).astype(BF16)
    row = lax.broadcasted_iota(jnp.int32, (tm, tm), 0)
    col = lax.broadcasted_iota(jnp.int32, (tm, tm), 1)
    before = jnp.dot((row > col).astype(BF16), hits, preferred_element_type=F32) + carry_ref[0:1, :]
    rank0 = jnp.sum(jnp.where(hit0, before, 0.0), axis=-1, keepdims=True)
    rank1 = jnp.sum(jnp.where(hit1, before, 0.0), axis=-1, keepdims=True)
    total = carry_ref[0:1, :] + jnp.sum(hits.astype(F32), axis=0, keepdims=True)
    carry_ref[0:1, :] = total
    cnt_ref[...] = jnp.broadcast_to(total, cnt_ref.shape)
    rec = jnp.zeros(lg.shape, F32)
    for slot, val in ((R_E0, (i0 - N_GROUPS).astype(F32)), (R_E1, (i1 - N_GROUPS).astype(F32)),
                      (R_W0, w0), (R_W1, w1), (R_RANK0, rank0), (R_RANK1, rank1)):
        rec = jnp.where(lane == slot, val, rec)
    o_ref[...] = rec


def route(logits):
    S = logits.shape[0]
    tm = min(ROUTE_ROWS, S)
    return pl.pallas_call(
        _route_kernel,
        grid=(S // tm,),
        in_specs=[pl.BlockSpec((tm, ROUTE_LANES), lambda i: (i, 0))],
        out_specs=[pl.BlockSpec((tm, ROUTE_LANES), lambda i: (i, 0)),
                   pl.BlockSpec((SUBLANES, ROUTE_LANES), lambda i: (0, 0))],
        out_shape=[jax.ShapeDtypeStruct((S, ROUTE_LANES), F32),
                   jax.ShapeDtypeStruct((SUBLANES, ROUTE_LANES), F32)],
        scratch_shapes=[pltpu.VMEM((SUBLANES, ROUTE_LANES), F32)],
        compiler_params=_params("arbitrary"),
        name="route",
    )(logits)


def _expert_kernel(be_ref, nused_ref, src_ref, src1_ref, src2_ref, h_hbm, w1_ref, w3_ref, w2_ref, o_ref,
                   xbuf, wb1, wb3, wb2, gsem):
    i = pl.program_id(0)
    nused = nused_ref[0]
    B = EXPERT_ROWS
    slot = i % GATHER_DEPTH

    def gather(idx_ref, s):
        for r in range(B):
            pltpu.make_async_copy(h_hbm.at[pl.ds(idx_ref[0, r], 1)], xbuf.at[s, pl.ds(r, 1)],
                                  gsem.at[s]).start()

    def gather_wait(s):
        pltpu.make_async_copy(h_hbm.at[pl.ds(0, B)], xbuf.at[s], gsem.at[s]).wait()

    @pl.when(i == 0)
    def _():
        gather(src_ref, 0)
        gather(src1_ref, 1)

    @pl.when(i < nused)
    def _():
        @pl.when((i == 0) | (be_ref[i] != be_ref[jnp.maximum(i - 1, 0)]))
        def _():
            wb1[...] = w1_ref[...].astype(BF16)
            wb3[...] = w3_ref[...].astype(BF16)
            wb2[...] = w2_ref[...].astype(BF16)

        gather_wait(slot)
        x = xbuf[slot].astype(BF16)
        gather(src2_ref, (i + 2) % GATHER_DEPTH)
        a1 = jnp.dot(x, wb1[...], preferred_element_type=F32)
        a3 = jnp.dot(x, wb3[...], preferred_element_type=F32)
        mid = (_silu(a1) * a3).astype(BF16)
        o_ref[...] = jnp.dot(mid, wb2[...], preferred_element_type=F32)

        @pl.when(i == nused - 1)
        def _():
            gather_wait((i + 1) % GATHER_DEPTH)
            gather_wait((i + 2) % GATHER_DEPTH)

    @pl.when(i >= nused)
    def _():
        o_ref[...] = jnp.zeros(o_ref.shape, o_ref.dtype)


def experts(h, block_e, nused, src_row, w1, w3, w2, layer):
    T, D = h.shape
    nb = block_e.shape[0]
    B = EXPERT_ROWS
    idx = lambda f: pl.BlockSpec((None, 1, B), f, memory_space=pltpu.SMEM)
    wspec = lambda shp: pl.BlockSpec((None, None) + shp, lambda i, be, nu: (layer, be[i], 0, 0))
    grid_spec = pltpu.PrefetchScalarGridSpec(
        num_scalar_prefetch=2,
        grid=(nb,),
        in_specs=[idx(lambda i, be, nu: (i, 0, 0)),
                  idx(lambda i, be, nu: (jnp.minimum(i + 1, nb - 1), 0, 0)),
                  idx(lambda i, be, nu: (jnp.minimum(i + 2, nb - 1), 0, 0)),
                  pl.BlockSpec(memory_space=pl.ANY),
                  wspec((D, D_EXPERT)), wspec((D, D_EXPERT)), wspec((D_EXPERT, D))],
        out_specs=pl.BlockSpec((B, D), lambda i, be, nu: (i, 0)),
        scratch_shapes=[pltpu.VMEM((GATHER_DEPTH, B, D), F32),
                        pltpu.VMEM((D, D_EXPERT), BF16), pltpu.VMEM((D, D_EXPERT), BF16),
                        pltpu.VMEM((D_EXPERT, D), BF16),
                        pltpu.SemaphoreType.DMA((GATHER_DEPTH,))])
    src3 = src_row.reshape(nb, 1, B)
    return pl.pallas_call(
        _expert_kernel,
        grid_spec=grid_spec,
        out_shape=jax.ShapeDtypeStruct((nb * B, D), F32),
        compiler_params=_params("arbitrary"),
        name="experts",
    )(block_e, nused, src3, src3, src3, h, w1, w3, w2)


def dispatch_plan(rec, counts, T):
    B = EXPERT_ROWS
    A = 2 * T
    nb = A // B + N_EXPERTS
    cnt = counts[0, N_GROUPS:N_GROUPS + N_EXPERTS].astype(jnp.int32)
    padded = ((cnt + B - 1) // B) * B
    pend = jnp.cumsum(padded)
    pstart = pend - padded
    e = rec[:, R_E0:R_E1 + 1].astype(jnp.int32)
    rank = rec[:, R_RANK0:R_RANK1 + 1].astype(jnp.int32)
    dest = pstart[e] + rank
    tok = jnp.broadcast_to(jnp.arange(T, dtype=jnp.int32)[:, None], (T, 2))
    src_row = jnp.zeros((nb * B,), jnp.int32).at[dest.reshape(-1)].set(tok.reshape(-1))
    first = jnp.arange(nb, dtype=jnp.int32) * B
    block_e = jnp.minimum(jnp.sum((first[:, None] >= pend[None, :]).astype(jnp.int32), axis=1),
                          N_EXPERTS - 1)
    nused = (pend[-1] // B).astype(jnp.int32).reshape(1)
    return block_e, nused, src_row, dest


def _combine_kernel(final, d0_ref, d1_ref, nd0_ref, nd1_ref, x_ref, rec_ref, gt_ref, fg_ref, y_hbm,
                    o_ref, ybuf, sem):
    i = pl.program_id(0)
    tm = x_ref.shape[0]
    slot = i % 2

    def fetch(r0_ref, r1_ref, s):
        for k, idx_ref in enumerate((r0_ref, r1_ref)):
            for r in range(tm):
                pltpu.make_async_copy(y_hbm.at[pl.ds(idx_ref[0, r], 1)], ybuf.at[s, k, pl.ds(r, 1)],
                                      sem.at[s]).start()

    @pl.when(i == 0)
    def _():
        fetch(d0_ref, d1_ref, 0)

    @pl.when(i + 1 < pl.num_programs(0))
    def _():
        fetch(nd0_ref, nd1_ref, 1 - slot)

    for k in range(2):
        pltpu.make_async_copy(y_hbm.at[pl.ds(0, tm)], ybuf.at[slot, k], sem.at[slot]).wait()
    rec = rec_ref[...]
    moe = rec[:, R_W0:R_W0 + 1] * ybuf[slot, 0] + rec[:, R_W1:R_W1 + 1] * ybuf[slot, 1]
    x = x_ref[...] + gt_ref[...] * moe
    o_ref[...] = _rms(x) * fg_ref[...] if final else x


def combine(x, y, dest, rec, gt2, final_g, final):
    S, D = x.shape
    tm = min(COMBINE_ROWS, S)
    nt = S // tm
    d0 = dest[:, 0].reshape(nt, 1, tm)
    d1 = dest[:, 1].reshape(nt, 1, tm)
    rows = pl.BlockSpec((tm, D), lambda i: (i, 0))
    vec = pl.BlockSpec((1, D), lambda i: (0, 0))
    cur = pl.BlockSpec((None, 1, tm), lambda i: (i, 0, 0), memory_space=pltpu.SMEM)
    nxt = pl.BlockSpec((None, 1, tm), lambda i: (jnp.minimum(i + 1, nt - 1), 0, 0), memory_space=pltpu.SMEM)
    return pl.pallas_call(
        functools.partial(_combine_kernel, final),
        grid=(nt,),
        in_specs=[cur, cur, nxt, nxt, rows, pl.BlockSpec((tm, ROUTE_LANES), lambda i: (i, 0)), vec, vec,
                  pl.BlockSpec(memory_space=pl.ANY)],
        out_specs=rows,
        out_shape=jax.ShapeDtypeStruct((S, D), F32),
        scratch_shapes=[pltpu.VMEM((2, 2, tm, D), F32), pltpu.SemaphoreType.DMA((2,))],
        compiler_params=_params("arbitrary"),
        name="combine",
    )(d0, d1, d0, d1, x, rec, gt2, final_g, y)


def _rope_tables(positions):
    half = RET_DK // 2
    inv_freq = ROPE_BASE ** (-jnp.arange(half, dtype=F32) * (2.0 / RET_DK))
    ang = positions.astype(F32)[:, None] * inv_freq
    cos = jnp.cos(ang)
    sin = jnp.sin(ang)
    return jnp.concatenate([cos, cos], axis=-1), jnp.concatenate([-sin, sin], axis=-1)


def _proj_weights(w_in):
    L, D, _ = w_in.shape
    w_all = w_in[:, :, :MAIN_COLS].astype(BF16)
    ab = w_in[:, :, MAIN_COLS:MAIN_COLS + N_AB]
    w_ab = jnp.concatenate([ab, jnp.zeros((L, D, LANES - N_AB), w_in.dtype)], axis=2).astype(BF16)
    w_gates = w_in[:, :, MAIN_COLS + N_AB:].astype(BF16)
    return w_all, w_ab, w_gates


def kernel(x, c, positions, ada_w, ada_b, norm1_g, norm2_g, w_in, lru_conv_w, lru_conv_b, lru_wa, lru_ba,
           lru_wx, lru_bx, lru_lambda, gdn_conv_w, gdn_a_log, gdn_dt_bias, gdn_norm_g, branch_proj, w_out,
           router_grp_w, router_grp_b, router_exp_w, router_exp_b, moe_w1, moe_w3, moe_w2, final_g):
    B, S, D = x.shape
    assert B == 1
    xs = x.reshape(S, D)
    mod = ada_mod(c, ada_w, ada_b)
    cos2, sin2 = _rope_tables(positions.reshape(S))
    row = lambda v: v.reshape(1, -1).astype(F32)
    w_all, w_ab, w_gates = _proj_weights(w_in)
    for l in range(DEPTH):
        sh1, sc1, gt1, sh2, sc2, gt2 = (row(m) for m in jnp.split(mod[l], 6))
        z, z_ab = norm_proj(xs, row(norm1_g[l]), sc1, sh1, w_all, w_ab, l)
        gates = gate_proj(xs, row(norm1_g[l]), sc1, sh1, w_gates, l)
        y_ret = retention(z, cos2, sin2)
        y_lru = rglru(z, lru_conv_w[l], lru_conv_b[l], lru_wa[l], lru_ba[l], lru_wx[l], lru_bx[l],
                      lru_lambda[l])
        y_gdn = gdn(z, z_ab, gdn_conv_w[l], gdn_a_log[l], gdn_dt_bias[l], gdn_norm_g[l])
        n_r = N_GROUPS + N_EXPERTS
        w_route = jnp.concatenate([router_grp_w[l], router_exp_w[l],
                                   jnp.zeros((D, ROUTE_LANES - n_r), F32)], axis=1)
        b_route = jnp.concatenate([router_grp_b[l], router_exp_b[l],
                                   jnp.zeros((ROUTE_LANES - n_r,), F32)]).reshape(1, ROUTE_LANES)
        xs, h2, logits = merge(y_ret, y_lru, y_gdn, gates, xs, gt1, branch_proj[l].astype(BF16),
                               w_out[l].astype(BF16), row(norm2_g[l]), sc2, sh2, w_route, b_route)
        rec, counts = route(logits)
        block_e, nused, src_row, dest = dispatch_plan(rec, counts, S)
        y = experts(h2, block_e, nused, src_row, moe_w1, moe_w3, moe_w2, l)
        xs = combine(xs, y, dest, rec, gt2, row(final_g), final=(l == DEPTH - 1))
    return xs.reshape(B, S, D)
```

```python
import functools
import math

import numpy as np
import jax
import jax.numpy as jnp
from jax import lax
from jax.experimental import pallas as pl
from jax.experimental.pallas import tpu as pltpu

F32 = jnp.float32
BF16 = jnp.bfloat16
HIGHEST = lax.Precision.HIGHEST

D_MODEL = 2048
DEPTH = 2
RET_HEADS = 8
RET_DK = 128
RET_CHUNK = 128
ROPE_BASE = 10000.0
LRU_W = 1024
LRU_BLOCKS = 8
LRU_BS = LRU_W // LRU_BLOCKS
LRU_C = 8.0
CONV_K = 4
GDN_HEADS = 8
GDN_DK = 128
GDN_CHUNK = 64
N_BRANCH = 3
N_GROUPS = 4
EXPERTS_PER_GROUP = 8
N_EXPERTS = N_GROUPS * EXPERTS_PER_GROUP
D_EXPERT = 512
EPS = 1e-6

HEAD_W = 1024
MAIN_COLS = 10 * HEAD_W
N_AB = 2 * GDN_HEADS
GATE_COLS = N_BRANCH * D_MODEL
MOD_TN = 1536
PROJ_ROWS = 1024
PROJ_TN = 1280
LRU_ROWS = 256
MERGE_ROWS = 256
ROUTE_ROWS = 512
GATE_TN = 1024
GATE_ROW_CHUNK = 256
LANES = 128
SUBLANES = 8
EXPERT_ROWS = 256
COMBINE_ROWS = 256
GATHER_DEPTH = 3
GDN_ROWS = 2 * GDN_CHUNK
RET_ROWS = 4 * RET_CHUNK
VMEM_LIMIT = 56 * 1024 * 1024

NT_DIMS = (((1,), (1,)), ((), ()))
TN_DIMS = (((0,), (0,)), ((), ()))


def _params(*sem):
    return pltpu.CompilerParams(dimension_semantics=sem, vmem_limit_bytes=VMEM_LIMIT)


def _dot32(a, b):
    return jnp.dot(a, b, preferred_element_type=F32, precision=HIGHEST)


def _dotg32(a, b, dims):
    return lax.dot_general(a, b, dims, preferred_element_type=F32, precision=HIGHEST)


def _dot(a, b):
    return jnp.dot(a.astype(BF16), b.astype(BF16), preferred_element_type=F32)


def _dotg(a, b, dims):
    return lax.dot_general(a.astype(BF16), b.astype(BF16), dims, preferred_element_type=F32)


def _dot3(a, b):
    ah = a.astype(BF16)
    bh = b.astype(BF16)
    al = (a - ah.astype(F32)).astype(BF16)
    bl = (b - bh.astype(F32)).astype(BF16)
    d = functools.partial(jnp.dot, preferred_element_type=F32)
    return d(ah, bh) + (d(ah, bl) + d(al, bh))


def _silu(x):
    return x * jax.nn.sigmoid(x)


def _softplus(x):
    return jnp.maximum(x, 0.0) + jnp.log1p(jnp.exp(-jnp.abs(x)))


def _rms(x):
    return x * lax.rsqrt(jnp.mean(x * x, axis=-1, keepdims=True) + EPS)


def _shift_rows(x, s, fill):
    n = x.shape[0]
    if s % SUBLANES == 0:
        return jnp.concatenate([jnp.full((s, x.shape[1]), fill, x.dtype), x[:n - s]], axis=0)
    row = lax.broadcasted_iota(jnp.int32, x.shape, 0)
    return jnp.where(row < s, fill, pltpu.roll(x, s, 0))


def _causal_conv(x, tail, w):
    n = x.shape[0]
    row = lax.broadcasted_iota(jnp.int32, (SUBLANES, x.shape[1]), 0)
    y = x * w[CONV_K - 1:CONV_K]
    for s in range(1, CONV_K):
        xs = pltpu.roll(x, s, 0)
        head = jnp.where(row < s, pltpu.roll(tail, s, 0), xs[:SUBLANES])
        xs = jnp.concatenate([head, xs[SUBLANES:]], axis=0) if n > SUBLANES else head
        y = y + xs * w[CONV_K - 1 - s:CONV_K - s]
    return y


def _mod_kernel(c_ref, w_ref, b_ref, o_ref):
    c = c_ref[...]
    o_ref[...] = _dot3(_silu(c), w_ref[...]) + b_ref[...]


def ada_mod(c, ada_w, ada_b):
    L, D, N = ada_w.shape
    tn = MOD_TN
    c8 = jnp.broadcast_to(c.astype(F32), (SUBLANES, D))
    out = pl.pallas_call(
        _mod_kernel,
        grid=(L, N // tn),
        in_specs=[pl.BlockSpec((SUBLANES, D), lambda l, j: (0, 0)),
                  pl.BlockSpec((None, D, tn), lambda l, j: (l, 0, j)),
                  pl.BlockSpec((None, 1, tn), lambda l, j: (l, 0, j))],
        out_specs=pl.BlockSpec((None, SUBLANES, tn), lambda l, j: (l, 0, j)),
        out_shape=jax.ShapeDtypeStruct((L, SUBLANES, N), F32),
        compiler_params=_params("parallel", "parallel"),
        name="ada_mod",
    )(c8, ada_w, ada_b.reshape(L, 1, N))
    return out[:, 0, :]


def _norm_proj_kernel(x_ref, g_ref, sc_ref, sh_ref, w_ref, wab_ref, o_ref, ab_ref, h_ref):
    @pl.when(pl.program_id(1) == 0)
    def _():
        h = (_rms(x_ref[...]) * g_ref[...]) * (1.0 + sc_ref[...]) + sh_ref[...]
        h_ref[...] = h.astype(BF16)
        ab_ref[...] = jnp.dot(h_ref[...], wab_ref[...], preferred_element_type=F32)

    o_ref[...] = jnp.dot(h_ref[...], w_ref[...], preferred_element_type=F32)


def norm_proj(x, g, sc, sh, w, w_ab, layer):
    S, D = x.shape
    tm = min(PROJ_ROWS, S)
    vec = pl.BlockSpec((1, D), lambda i, j: (0, 0))
    return pl.pallas_call(
        _norm_proj_kernel,
        grid=(S // tm, MAIN_COLS // PROJ_TN),
        in_specs=[pl.BlockSpec((tm, D), lambda i, j: (i, 0)), vec, vec, vec,
                  pl.BlockSpec((None, D, PROJ_TN), lambda i, j: (layer, 0, j)),
                  pl.BlockSpec((None, D, LANES), lambda i, j: (layer, 0, 0))],
        out_specs=[pl.BlockSpec((tm, PROJ_TN), lambda i, j: (i, j)),
                   pl.BlockSpec((tm, LANES), lambda i, j: (i, 0))],
        out_shape=[jax.ShapeDtypeStruct((S, MAIN_COLS), F32), jax.ShapeDtypeStruct((S, LANES), F32)],
        scratch_shapes=[pltpu.VMEM((tm, D), BF16)],
        compiler_params=_params("parallel", "arbitrary"),
        name="norm_proj",
    )(x, g, sc, sh, w, w_ab)


def _gate_proj_kernel(x_ref, g_ref, sc_ref, sh_ref, w_ref, o_ref, h_ref):
    @pl.when(pl.program_id(1) == 0)
    def _():
        h = (_rms(x_ref[...]) * g_ref[...]) * (1.0 + sc_ref[...]) + sh_ref[...]
        h_ref[...] = h.astype(BF16)

    rows = GATE_ROW_CHUNK
    for r in range(h_ref.shape[0] // rows):
        rs = slice(r * rows, (r + 1) * rows)
        o_ref[rs] = jax.nn.sigmoid(jnp.dot(h_ref[rs], w_ref[...], preferred_element_type=F32)).astype(BF16)


def gate_proj(x, g, sc, sh, w_gates, layer):
    S, D = x.shape
    tm = min(PROJ_ROWS, S)
    tn = GATE_TN
    vec = pl.BlockSpec((1, D), lambda i, j: (0, 0))
    return pl.pallas_call(
        _gate_proj_kernel,
        grid=(S // tm, GATE_COLS // tn),
        in_specs=[pl.BlockSpec((tm, D), lambda i, j: (i, 0)), vec, vec, vec,
                  pl.BlockSpec((None, D, tn), lambda i, j: (layer, 0, j))],
        out_specs=pl.BlockSpec((tm, tn), lambda i, j: (i, j)),
        out_shape=jax.ShapeDtypeStruct((S, GATE_COLS), BF16),
        scratch_shapes=[pltpu.VMEM((tm, D), BF16)],
        compiler_params=_params("parallel", "arbitrary"),
        name="gate_proj",
    )(x, g, sc, sh, w_gates)


def _ret_tables():
    H, C = RET_HEADS, RET_CHUNK
    log_g = np.log1p(-np.exp2(-5.0 - np.arange(H, dtype=np.float64)))
    pos = np.arange(C, dtype=np.float64)
    rel = pos[:, None] - pos[None, :]
    intra = np.where(rel >= 0, np.exp(log_g[:, None, None] * np.maximum(rel, 0.0)), 0.0)
    qd = np.exp(log_g[:, None] * (pos + 1.0))
    kd = np.exp(log_g[:, None] * (C - 1.0 - pos))
    ones = np.ones((1, 1, LANES))
    cd = [float(np.exp(lg * C)) for lg in log_g]
    return (jnp.asarray(intra, F32), jnp.asarray(qd[:, :, None] * ones, F32),
            jnp.asarray(kd[:, :, None] * ones, F32), cd)


def _ret_kernel(cd, q_ref, k_ref, v_ref, g_ref, cos_ref, sin_ref, intra_ref, qd_ref, kd_ref,
                o_ref, state_ref):
    @pl.when(pl.program_id(0) == 0)
    def _():
        state_ref[...] = jnp.zeros_like(state_ref)

    C = RET_CHUNK
    half = RET_DK // 2
    for c in range(q_ref.shape[0] // C):
        rs = slice(c * C, (c + 1) * C)
        cos2 = cos_ref[rs]
        sin2 = sin_ref[rs]
        for h in range(RET_HEADS):
            sl = slice(h * RET_DK, (h + 1) * RET_DK)
            q = q_ref[rs, sl]
            k = k_ref[rs, sl]
            v = v_ref[rs, sl]
            q = q * cos2 + pltpu.roll(q, half, 1) * sin2
            k = (k * cos2 + pltpu.roll(k, half, 1) * sin2) * (RET_DK ** -0.5)
            scores = _dotg(q, k, NT_DIMS) * intra_ref[h]
            state = state_ref[h]
            o = _dot(scores, v) + _dot(q * qd_ref[h], state)
            state_ref[h] = state * cd[h] + _dotg(k * kd_ref[h], v, TN_DIMS)
            o_ref[rs, sl] = (_rms(o) * _silu(g_ref[rs, sl])).astype(o_ref.dtype)


def retention(z, cos2, sin2):
    S = z.shape[0]
    C = min(RET_ROWS, S)
    intra, qd, kd, cd = _ret_tables()
    col = lambda b: pl.BlockSpec((C, HEAD_W), lambda n, b=b: (n, b))
    tab = lambda a: pl.BlockSpec(a.shape, lambda n: (0, 0, 0))
    rot = pl.BlockSpec((C, RET_DK), lambda n: (n, 0))
    return pl.pallas_call(
        functools.partial(_ret_kernel, cd),
        grid=(S // C,),
        in_specs=[col(0), col(1), col(2), col(3), rot, rot, tab(intra), tab(qd), tab(kd)],
        out_specs=pl.BlockSpec((C, HEAD_W), lambda n: (n, 0)),
        out_shape=jax.ShapeDtypeStruct((S, HEAD_W), BF16),
        scratch_shapes=[pltpu.VMEM((RET_HEADS, RET_DK, RET_DK), F32)],
        compiler_params=_params("arbitrary"),
        name="retention",
    )(z, z, z, z, cos2, sin2, intra, qd, kd)


def _lru_kernel(x_ref, gate_ref, cw_ref, cb_ref, wa_ref, ba_ref, wx_ref, bx_ref, lam_ref,
                o_ref, tail_ref, carry_ref):
    @pl.when(pl.program_id(0) == 0)
    def _():
        tail_ref[...] = jnp.zeros_like(tail_ref)
        carry_ref[...] = jnp.zeros_like(carry_ref)

    tm = x_ref.shape[0]
    for j in range(LRU_BLOCKS):
        sl = slice(j * LRU_BS, (j + 1) * LRU_BS)
        x = x_ref[:, sl]
        xc = _causal_conv(x, tail_ref[:, sl], cw_ref[:, sl]) + cb_ref[:, sl]
        tail_ref[:, sl] = x[tm - SUBLANES:]
        r = jax.nn.sigmoid(_dot(xc, wa_ref[j]) + ba_ref[:, sl])
        i = jax.nn.sigmoid(_dot(xc, wx_ref[j]) + bx_ref[:, sl])
        log_a = (-LRU_C * r) * _softplus(-lam_ref[:, sl])
        a = jnp.exp(log_a)
        t = jnp.tanh(log_a)
        b = jnp.sqrt(-2.0 * t / (1.0 - t)) * (i * xc)
        s = 1
        while s < tm:
            b = b + a * _shift_rows(b, s, 0.0)
            a = a * _shift_rows(a, s, 1.0)
            s *= 2
        h = b + a * carry_ref[0:1, sl]
        carry_ref[0:1, sl] = h[tm - 1:tm]
        g = gate_ref[:, sl]
        gelu = g * (0.5 * (1.0 + jnp.tanh(math.sqrt(2.0 / math.pi) * (g + 0.044715 * (g * g * g)))))
        o_ref[:, sl] = (h * gelu).astype(o_ref.dtype)


def rglru(z, conv_w, conv_b, wa, ba, wx, bx, lam):
    S = z.shape[0]
    tm = min(LRU_ROWS, S)
    row = lambda a: a.reshape(1, LRU_W)
    vec = pl.BlockSpec((1, LRU_W), lambda n: (0, 0))
    wspec = pl.BlockSpec((LRU_BLOCKS, LRU_BS, LRU_BS), lambda n: (0, 0, 0))
    return pl.pallas_call(
        _lru_kernel,
        grid=(S // tm,),
        in_specs=[pl.BlockSpec((tm, HEAD_W), lambda n: (n, 4)),
                  pl.BlockSpec((tm, HEAD_W), lambda n: (n, 5)),
                  pl.BlockSpec((CONV_K, LRU_W), lambda n: (0, 0)), vec, wspec, vec, wspec, vec, vec],
        out_specs=pl.BlockSpec((tm, HEAD_W), lambda n: (n, 0)),
        out_shape=jax.ShapeDtypeStruct((S, HEAD_W), BF16),
        scratch_shapes=[pltpu.VMEM((SUBLANES, LRU_W), F32), pltpu.VMEM((SUBLANES, LRU_W), F32)],
        compiler_params=_params("arbitrary"),
        name="rglru",
    )(z, z, conv_w, row(conv_b), wa, row(ba), wx, row(bx), row(lam))


def _gdn_tables():
    C = GDN_CHUNK
    i = np.arange(C)[:, None]
    j = np.arange(C)[None, :]
    levels = []
    s = 1
    while s < C:
        levels.append((i // (2 * s) == j // (2 * s)) & (i % (2 * s) >= s) & (j % (2 * s) < s))
        s *= 2
    return jnp.asarray(np.stack(levels), F32), jnp.asarray(i >= j, F32), jnp.asarray(i <= j, F32)


def _gdn_kernel(q_ref, k_ref, v_ref, z_ref, ab_ref, cw_ref, alog_ref, dtb_ref, ng_ref, lvl_ref, tri_ref,
                trit_ref, o_ref, tail_ref, state_ref):
    @pl.when(pl.program_id(0) == 0)
    def _():
        tail_ref[...] = jnp.zeros_like(tail_ref)
        state_ref[...] = jnp.zeros_like(state_ref)

    C = GDN_CHUNK
    H = GDN_HEADS
    DK = GDN_DK
    rows = q_ref.shape[0]
    incl = tri_ref[...]
    rowi = lax.broadcasted_iota(jnp.int32, (C, C), 0)
    coli = lax.broadcasted_iota(jnp.int32, (C, C), 1)
    eye = (rowi == coli).astype(F32)

    ab = ab_ref[...]
    g_all = -jnp.exp(alog_ref[...]) * _softplus(ab + dtb_ref[...])
    beta_all = jax.nn.sigmoid(ab)

    conv = []
    for p, ref in enumerate((q_ref, k_ref, v_ref)):
        x = ref[...]
        conv.append(_silu(_causal_conv(x, tail_ref[p], cw_ref[p])))
        tail_ref[p] = x[rows - SUBLANES:]
    qc, kc, vc = conv

    ones = jnp.ones((DK, DK), BF16)

    def unit_rows(x, scale):
        ssq = jnp.concatenate(
            [jnp.dot((x[:, h * DK:(h + 1) * DK] ** 2).astype(BF16), ones, preferred_element_type=F32)
             for h in range(H)], axis=1)
        return x * (lax.rsqrt(ssq + EPS) * scale)
    qc = unit_rows(qc, DK ** -0.5)
    kc = unit_rows(kc, 1.0)

    pairs = [(c, h) for c in range(rows // C) for h in range(H)]
    st = {}
    for c in range(rows // C):
        g_c = g_all[c * C:(c + 1) * C]
        st[c] = (_dot32(incl, g_c),
                 _dotg32(g_c, trit_ref[...], TN_DIMS))
    for c, h in pairs:
        rs = slice(c * C, (c + 1) * C)
        sl = slice(h * DK, (h + 1) * DK)
        q = qc[rs, sl]
        k = kc[rs, sl]
        beta = beta_all[rs, H + h:H + h + 1]
        gc = st[c][0][:, h:h + 1]
        gcr = st[c][1][h:h + 1, :]
        decay = incl * jnp.exp(jnp.where(incl > 0, gc - gcr, 0.0))
        kb = k * beta
        egc = jnp.exp(gc)
        gl = gc[C - 1:C]
        qk = _dotg(jnp.concatenate([q, kb], axis=0), k, NT_DIMS)
        st[c, h] = dict(
            attn=qk[:C] * decay,
            m=qk[C:] * (decay - eye),
            rhs=jnp.concatenate([vc[rs, sl] * beta, kb * egc], axis=1),
            qdec=q * egc,
            kdt=(k * jnp.exp(gl - gc)).T,
            egl=jnp.exp(gl))
    for p in pairs:
        st[p]['inv'] = eye - st[p]['m'] * lvl_ref[0]
    for lv in range(1, lvl_ref.shape[0]):
        for p in pairs:
            st[p]['t'] = _dot(st[p]['m'] * lvl_ref[lv], st[p]['inv'])
        for p in pairs:
            st[p]['inv'] = st[p]['inv'] - _dot(st[p]['inv'], st[p]['t'])
    for p in pairs:
        st[p]['uw'] = _dot(st[p]['inv'], st[p]['rhs'])
    for c, h in pairs:
        d = st[c, h]
        state = state_ref[h]
        ws = _dot(jnp.concatenate([d['uw'][:, DK:], d['qdec']], axis=0), state)
        v_new = d['uw'][:, :DK] - ws[:C]
        o = ws[C:] + _dot(d['attn'], v_new)
        state_ref[h] = state * d['egl'] + _dot(d['kdt'], v_new)
        rs = slice(c * C, (c + 1) * C)
        sl = slice(h * DK, (h + 1) * DK)
        o_ref[rs, sl] = (_rms(o) * ng_ref[...] * _silu(z_ref[rs, sl])).astype(o_ref.dtype)


def gdn(z, z_ab, conv_w, a_log, dt_bias, norm_g):
    S = z.shape[0]
    C = GDN_CHUNK
    H = GDN_HEADS
    lvl, tri, trit = _gdn_tables()
    alog = jnp.zeros((1, LANES), F32).at[0, :H].set(a_log)
    dtb = jnp.zeros((1, LANES), F32).at[0, :H].set(dt_bias)
    cw = conv_w.reshape(CONV_K, 3, HEAD_W).transpose(1, 0, 2)
    tm = min(GDN_ROWS, S)
    col = lambda b: pl.BlockSpec((tm, HEAD_W), lambda n, b=b: (n, b))
    lane = pl.BlockSpec((1, LANES), lambda n: (0, 0))
    return pl.pallas_call(
        _gdn_kernel,
        grid=(S // tm,),
        in_specs=[col(6), col(7), col(8), col(9),
                  pl.BlockSpec((tm, LANES), lambda n: (n, 0)),
                  pl.BlockSpec((3, CONV_K, HEAD_W), lambda n: (0, 0, 0)),
                  lane, lane, lane,
                  pl.BlockSpec(lvl.shape, lambda n: (0, 0, 0)),
                  pl.BlockSpec(tri.shape, lambda n: (0, 0)),
                  pl.BlockSpec(trit.shape, lambda n: (0, 0))],
        out_specs=pl.BlockSpec((tm, HEAD_W), lambda n: (n, 0)),
        out_shape=jax.ShapeDtypeStruct((S, HEAD_W), BF16),
        scratch_shapes=[pltpu.VMEM((3, SUBLANES, HEAD_W), F32),
                        pltpu.VMEM((H, GDN_DK, GDN_DK), F32)],
        compiler_params=_params("arbitrary"),
        name="gdn",
    )(z, z, z, z, z_ab, cw, alog, dtb, norm_g.reshape(1, GDN_DK), lvl, tri, trit)


ROUTE_LANES = LANES


def _merge_kernel(yr_ref, yl_ref, yg_ref, g0_ref, g1_ref, g2_ref, x_ref, gt_ref, bp_ref, wo_ref,
                  ng_ref, sc_ref, sh_ref, wr_ref, br_ref, xo_ref, h_ref, lg_ref):
    merged = g0_ref[...] * jnp.dot(yr_ref[...], bp_ref[0], preferred_element_type=F32)
    merged += g1_ref[...] * jnp.dot(yl_ref[...], bp_ref[1], preferred_element_type=F32)
    merged += g2_ref[...] * jnp.dot(yg_ref[...], bp_ref[2], preferred_element_type=F32)
    out = jnp.dot(merged.astype(BF16), wo_ref[...], preferred_element_type=F32)
    x = x_ref[...] + gt_ref[...] * out
    xo_ref[...] = x
    h = (_rms(x) * ng_ref[...]) * (1.0 + sc_ref[...]) + sh_ref[...]
    h_ref[...] = h
    lg_ref[...] = _dot3(h, wr_ref[...]) + br_ref[...]


def merge(y_ret, y_lru, y_gdn, gates, x, gt1, bp, w_out, norm2_g, sc2, sh2, w_route, b_route):
    S, D = x.shape
    tm = min(MERGE_ROWS, S)
    ycol = pl.BlockSpec((tm, HEAD_W), lambda i: (i, 0))
    gate = lambda b: pl.BlockSpec((tm, D), lambda i, b=b: (i, b))
    rows = pl.BlockSpec((tm, D), lambda i: (i, 0))
    vec = pl.BlockSpec((1, D), lambda i: (0, 0))
    once = dict(pipeline_mode=pl.Buffered(1))
    return pl.pallas_call(
        _merge_kernel,
        grid=(S // tm,),
        in_specs=[ycol, ycol, ycol, gate(0), gate(1), gate(2), rows, vec,
                  pl.BlockSpec((N_BRANCH, HEAD_W, D), lambda i: (0, 0, 0), **once),
                  pl.BlockSpec((D, D), lambda i: (0, 0), **once),
                  vec, vec, vec,
                  pl.BlockSpec((D, ROUTE_LANES), lambda i: (0, 0)),
                  pl.BlockSpec((1, ROUTE_LANES), lambda i: (0, 0))],
        out_specs=[rows, rows, pl.BlockSpec((tm, ROUTE_LANES), lambda i: (i, 0))],
        out_shape=[jax.ShapeDtypeStruct((S, D), F32), jax.ShapeDtypeStruct((S, D), F32),
                   jax.ShapeDtypeStruct((S, ROUTE_LANES), F32)],
        compiler_params=_params("parallel"),
        name="merge",
    )(y_ret, y_lru, y_gdn, gates, gates, gates, x, gt1, bp, w_out, norm2_g, sc2, sh2, w_route, b_route)


R_E0, R_E1, R_W0, R_W1, R_RANK0, R_RANK1 = range(6)


def _first_max(vals, lane):
    m = jnp.max(vals, axis=-1, keepdims=True)
    idx = jnp.min(jnp.where(vals == m, lane, ROUTE_LANES), axis=-1, keepdims=True)
    return m, idx


def _route_kernel(lg_ref, o_ref, cnt_ref, carry_ref):
    @pl.when(pl.program_id(0) == 0)
    def _():
        carry_ref[...] = jnp.zeros_like(carry_ref)

    lg = lg_ref[...]
    tm = lg.shape[0]
    lane = lax.broadcasted_iota(jnp.int32, lg.shape, 1)
    neg = jnp.float32(-jnp.inf)
    grp = jnp.where(lane < N_GROUPS, lg, neg)
    gmax, gsel = _first_max(grp, lane)
    pg_top = 1.0 / jnp.sum(jnp.exp(grp - gmax), axis=-1, keepdims=True)
    lo = N_GROUPS + gsel * EXPERTS_PER_GROUP
    inside = jnp.where((lane >= lo) & (lane < lo + EXPERTS_PER_GROUP), lg, neg)
    v0, i0 = _first_max(inside, lane)
    v1, i1 = _first_max(jnp.where(lane == i0, neg, inside), lane)
    e1w = jnp.exp(v1 - v0)
    w0 = pg_top / (1.0 + e1w)
    w1 = pg_top * e1w / (1.0 + e1w)
    hit0 = lane == i0
    hit1 = lane == i1
    hits = (hit0 | hit1).astype(BF16)
    row = lax.broadcasted_iota(jnp.int32, (tm, tm), 0)
    col = lax.broadcasted_iota(jnp.int32, (tm, tm), 1)
    before = jnp.dot((row > col).astype(BF16), hits, preferred_element_type=F32) + carry_ref[0:1, :]
    rank0 = jnp.sum(jnp.where(hit0, before, 0.0), axis=-1, keepdims=True)
    rank1 = jnp.sum(jnp.where(hit1, before, 0.0), axis=-1, keepdims=True)
    total = carry_ref[0:1, :] + jnp.sum(hits.astype(F32), axis=0, keepdims=True)
    carry_ref[0:1, :] = total
    cnt_ref[...] = jnp.broadcast_to(total, cnt_ref.shape)
    rec = jnp.zeros(lg.shape, F32)
    for slot, val in ((R_E0, (i0 - N_GROUPS).astype(F32)), (R_E1, (i1 - N_GROUPS).astype(F32)),
                      (R_W0, w0), (R_W1, w1), (R_RANK0, rank0), (R_RANK1, rank1)):
        rec = jnp.where(lane == slot, val, rec)
    o_ref[...] = rec


def route(logits):
    S = logits.shape[0]
    tm = min(ROUTE_ROWS, S)
    return pl.pallas_call(
        _route_kernel,
        grid=(S // tm,),
        in_specs=[pl.BlockSpec((tm, ROUTE_LANES), lambda i: (i, 0))],
        out_specs=[pl.BlockSpec((tm, ROUTE_LANES), lambda i: (i, 0)),
                   pl.BlockSpec((SUBLANES, ROUTE_LANES), lambda i: (0, 0))],
        out_shape=[jax.ShapeDtypeStruct((S, ROUTE_LANES), F32),
                   jax.ShapeDtypeStruct((SUBLANES, ROUTE_LANES), F32)],
        scratch_shapes=[pltpu.VMEM((SUBLANES, ROUTE_LANES), F32)],
        compiler_params=_params("arbitrary"),
        name="route",
    )(logits)


def _expert_kernel(be_ref, nused_ref, src_ref, src1_ref, src2_ref, h_hbm, w1_ref, w3_ref, w2_ref, o_ref,
                   xbuf, wb1, wb3, wb2, gsem):
    i = pl.program_id(0)
    nused = nused_ref[0]
    B = EXPERT_ROWS
    slot = i % GATHER_DEPTH

    def gather(idx_ref, s):
        for r in range(B):
            pltpu.make_async_copy(h_hbm.at[pl.ds(idx_ref[0, r], 1)], xbuf.at[s, pl.ds(r, 1)],
                                  gsem.at[s]).start()

    def gather_wait(s):
        pltpu.make_async_copy(h_hbm.at[pl.ds(0, B)], xbuf.at[s], gsem.at[s]).wait()

    @pl.when(i == 0)
    def _():
        gather(src_ref, 0)
        gather(src1_ref, 1)

    @pl.when(i < nused)
    def _():
        @pl.when((i == 0) | (be_ref[i] != be_ref[jnp.maximum(i - 1, 0)]))
        def _():
            wb1[...] = w1_ref[...].astype(BF16)
            wb3[...] = w3_ref[...].astype(BF16)
            wb2[...] = w2_ref[...].astype(BF16)

        gather_wait(slot)
        x = xbuf[slot].astype(BF16)
        gather(src2_ref, (i + 2) % GATHER_DEPTH)
        a1 = jnp.dot(x, wb1[...], preferred_element_type=F32)
        a3 = jnp.dot(x, wb3[...], preferred_element_type=F32)
        mid = (_silu(a1) * a3).astype(BF16)
        o_ref[...] = jnp.dot(mid, wb2[...], preferred_element_type=F32)

        @pl.when(i == nused - 1)
        def _():
            gather_wait((i + 1) % GATHER_DEPTH)
            gather_wait((i + 2) % GATHER_DEPTH)

    @pl.when(i >= nused)
    def _():
        o_ref[...] = jnp.zeros(o_ref.shape, o_ref.dtype)


def experts(h, block_e, nused, src_row, w1, w3, w2, layer):
    T, D = h.shape
    nb = block_e.shape[0]
    B = EXPERT_ROWS
    idx = lambda f: pl.BlockSpec((None, 1, B), f, memory_space=pltpu.SMEM)
    wspec = lambda shp: pl.BlockSpec((None, None) + shp, lambda i, be, nu: (layer, be[i], 0, 0))
    grid_spec = pltpu.PrefetchScalarGridSpec(
        num_scalar_prefetch=2,
        grid=(nb,),
        in_specs=[idx(lambda i, be, nu: (i, 0, 0)),
                  idx(lambda i, be, nu: (jnp.minimum(i + 1, nb - 1), 0, 0)),
                  idx(lambda i, be, nu: (jnp.minimum(i + 2, nb - 1), 0, 0)),
                  pl.BlockSpec(memory_space=pl.ANY),
                  wspec((D, D_EXPERT)), wspec((D, D_EXPERT)), wspec((D_EXPERT, D))],
        out_specs=pl.BlockSpec((B, D), lambda i, be, nu: (i, 0)),
        scratch_shapes=[pltpu.VMEM((GATHER_DEPTH, B, D), F32),
                        pltpu.VMEM((D, D_EXPERT), BF16), pltpu.VMEM((D, D_EXPERT), BF16),
                        pltpu.VMEM((D_EXPERT, D), BF16),
                        pltpu.SemaphoreType.DMA((GATHER_DEPTH,))])
    src3 = src_row.reshape(nb, 1, B)
    return pl.pallas_call(
        _expert_kernel,
        grid_spec=grid_spec,
        out_shape=jax.ShapeDtypeStruct((nb * B, D), F32),
        compiler_params=_params("arbitrary"),
        name="experts",
    )(block_e, nused, src3, src3, src3, h, w1, w3, w2)


def dispatch_plan(rec, counts, T):
    B = EXPERT_ROWS
    A = 2 * T
    nb = A // B + N_EXPERTS
    cnt = counts[0, N_GROUPS:N_GROUPS + N_EXPERTS].astype(jnp.int32)
    padded = ((cnt + B - 1) // B) * B
    pend = jnp.cumsum(padded)
    pstart = pend - padded
    e = rec[:, R_E0:R_E1 + 1].astype(jnp.int32)
    rank = rec[:, R_RANK0:R_RANK1 + 1].astype(jnp.int32)
    dest = pstart[e] + rank
    tok = jnp.broadcast_to(jnp.arange(T, dtype=jnp.int32)[:, None], (T, 2))
    src_row = jnp.zeros((nb * B,), jnp.int32).at[dest.reshape(-1)].set(tok.reshape(-1))
    first = jnp.arange(nb, dtype=jnp.int32) * B
    block_e = jnp.minimum(jnp.sum((first[:, None] >= pend[None, :]).astype(jnp.int32), axis=1),
                          N_EXPERTS - 1)
    nused = (pend[-1] // B).astype(jnp.int32).reshape(1)
    return block_e, nused, src_row, dest


def _combine_kernel(final, d0_ref, d1_ref, nd0_ref, nd1_ref, x_ref, rec_ref, gt_ref, fg_ref, y_hbm,
                    o_ref, ybuf, sem):
    i = pl.program_id(0)
    tm = x_ref.shape[0]
    slot = i % 2

    def fetch(r0_ref, r1_ref, s):
        for k, idx_ref in enumerate((r0_ref, r1_ref)):
            for r in range(tm):
                pltpu.make_async_copy(y_hbm.at[pl.ds(idx_ref[0, r], 1)], ybuf.at[s, k, pl.ds(r, 1)],
                                      sem.at[s]).start()

    @pl.when(i == 0)
    def _():
        fetch(d0_ref, d1_ref, 0)

    @pl.when(i + 1 < pl.num_programs(0))
    def _():
        fetch(nd0_ref, nd1_ref, 1 - slot)

    for k in range(2):
        pltpu.make_async_copy(y_hbm.at[pl.ds(0, tm)], ybuf.at[slot, k], sem.at[slot]).wait()
    rec = rec_ref[...]
    moe = rec[:, R_W0:R_W0 + 1] * ybuf[slot, 0] + rec[:, R_W1:R_W1 + 1] * ybuf[slot, 1]
    x = x_ref[...] + gt_ref[...] * moe
    o_ref[...] = _rms(x) * fg_ref[...] if final else x


def combine(x, y, dest, rec, gt2, final_g, final):
    S, D = x.shape
    tm = min(COMBINE_ROWS, S)
    nt = S // tm
    d0 = dest[:, 0].reshape(nt, 1, tm)
    d1 = dest[:, 1].reshape(nt, 1, tm)
    rows = pl.BlockSpec((tm, D), lambda i: (i, 0))
    vec = pl.BlockSpec((1, D), lambda i: (0, 0))
    cur = pl.BlockSpec((None, 1, tm), lambda i: (i, 0, 0), memory_space=pltpu.SMEM)
    nxt = pl.BlockSpec((None, 1, tm), lambda i: (jnp.minimum(i + 1, nt - 1), 0, 0), memory_space=pltpu.SMEM)
    return pl.pallas_call(
        functools.partial(_combine_kernel, final),
        grid=(nt,),
        in_specs=[cur, cur, nxt, nxt, rows, pl.BlockSpec((tm, ROUTE_LANES), lambda i: (i, 0)), vec, vec,
                  pl.BlockSpec(memory_space=pl.ANY)],
        out_specs=rows,
        out_shape=jax.ShapeDtypeStruct((S, D), F32),
        scratch_shapes=[pltpu.VMEM((2, 2, tm, D), F32), pltpu.SemaphoreType.DMA((2,))],
        compiler_params=_params("arbitrary"),
        name="combine",
    )(d0, d1, d0, d1, x, rec, gt2, final_g, y)


def _rope_tables(positions):
    half = RET_DK // 2
    inv_freq = ROPE_BASE ** (-jnp.arange(half, dtype=F32) * (2.0 / RET_DK))
    ang = positions.astype(F32)[:, None] * inv_freq
    cos = jnp.cos(ang)
    sin = jnp.sin(ang)
    return jnp.concatenate([cos, cos], axis=-1), jnp.concatenate([-sin, sin], axis=-1)


def _proj_weights(w_in):
    L, D, _ = w_in.shape
    w_all = w_in.astype(BF16)
    ab = w_all[:, :, MAIN_COLS:MAIN_COLS + N_AB]
    w_ab = jnp.concatenate([ab, jnp.zeros((L, D, LANES - N_AB), BF16)], axis=2)
    w_gates = w_all[:, :, MAIN_COLS + N_AB:]
    return w_all, w_ab, w_gates


def kernel(x, c, positions, ada_w, ada_b, norm1_g, norm2_g, w_in, lru_conv_w, lru_conv_b, lru_wa, lru_ba,
           lru_wx, lru_bx, lru_lambda, gdn_conv_w, gdn_a_log, gdn_dt_bias, gdn_norm_g, branch_proj, w_out,
           router_grp_w, router_grp_b, router_exp_w, router_exp_b, moe_w1, moe_w3, moe_w2, final_g):
    B, S, D = x.shape
    assert B == 1
    xs = x.reshape(S, D)
    mod = ada_mod(c, ada_w, ada_b)
    cos2, sin2 = _rope_tables(positions.reshape(S))
    row = lambda v: v.reshape(1, -1).astype(F32)
    w_all, w_ab, w_gates = _proj_weights(w_in)
    for l in range(DEPTH):
        sh1, sc1, gt1, sh2, sc2, gt2 = (row(m) for m in jnp.split(mod[l], 6))
        z, z_ab = norm_proj(xs, row(norm1_g[l]), sc1, sh1, w_all, w_ab, l)
        gates = gate_proj(xs, row(norm1_g[l]), sc1, sh1, w_gates, l)
        y_ret = retention(z, cos2, sin2)
        y_lru = rglru(z, lru_conv_w[l], lru_conv_b[l], lru_wa[l], lru_ba[l], lru_wx[l], lru_bx[l],
                      lru_lambda[l])
        y_gdn = gdn(z, z_ab, gdn_conv_w[l], gdn_a_log[l], gdn_dt_bias[l], gdn_norm_g[l])
        n_r = N_GROUPS + N_EXPERTS
        w_route = jnp.concatenate([router_grp_w[l], router_exp_w[l],
                                   jnp.zeros((D, ROUTE_LANES - n_r), F32)], axis=1)
        b_route = jnp.concatenate([router_grp_b[l], router_exp_b[l],
                                   jnp.zeros((ROUTE_LANES - n_r,), F32)]).reshape(1, ROUTE_LANES)
        xs, h2, logits = merge(y_ret, y_lru, y_gdn, gates, xs, gt1, branch_proj[l].astype(BF16),
                               w_out[l].astype(BF16), row(norm2_g[l]), sc2, sh2, w_route, b_route)
        rec, counts = route(logits)
        block_e, nused, src_row, dest = dispatch_plan(rec, counts, S)
        y = experts(h2, block_e, nused, src_row, moe_w1, moe_w3, moe_w2, l)
        xs = combine(xs, y, dest, rec, gt2, row(final_g), final=(l == DEPTH - 1))
    return xs.reshape(B, S, D)
```

```python
import functools
import math

import numpy as np
import jax
import jax.numpy as jnp
from jax import lax
from jax.experimental import pallas as pl
from jax.experimental.pallas import tpu as pltpu

F32 = jnp.float32
BF16 = jnp.bfloat16
HIGHEST = lax.Precision.HIGHEST

D_MODEL = 2048
DEPTH = 2
RET_HEADS = 8
RET_DK = 128
RET_CHUNK = 128
ROPE_BASE = 10000.0
LRU_W = 1024
LRU_BLOCKS = 8
LRU_BS = LRU_W // LRU_BLOCKS
LRU_C = 8.0
CONV_K = 4
GDN_HEADS = 8
GDN_DK = 128
GDN_CHUNK = 64
N_BRANCH = 3
N_GROUPS = 4
EXPERTS_PER_GROUP = 8
N_EXPERTS = N_GROUPS * EXPERTS_PER_GROUP
D_EXPERT = 512
EPS = 1e-6

HEAD_W = 1024
MAIN_COLS = 10 * HEAD_W
N_AB = 2 * GDN_HEADS
GATE_COLS = N_BRANCH * D_MODEL
MOD_TN = 1536
PROJ_ROWS = 1024
PROJ_TN = 1280
LRU_ROWS = 256
MERGE_ROWS = 256
ROUTE_ROWS = 512
GATE_TN = 1024
GATE_ROW_CHUNK = 256
LANES = 128
SUBLANES = 8
EXPERT_ROWS = 256
COMBINE_ROWS = 256
GDN_ROWS = 2 * GDN_CHUNK
RET_ROWS = 4 * RET_CHUNK
VMEM_LIMIT = 56 * 1024 * 1024

NT_DIMS = (((1,), (1,)), ((), ()))
TN_DIMS = (((0,), (0,)), ((), ()))


def _params(*sem):
    return pltpu.CompilerParams(dimension_semantics=sem, vmem_limit_bytes=VMEM_LIMIT)


def _dot32(a, b):
    return jnp.dot(a, b, preferred_element_type=F32, precision=HIGHEST)


def _dotg32(a, b, dims):
    return lax.dot_general(a, b, dims, preferred_element_type=F32, precision=HIGHEST)


def _dot(a, b):
    return jnp.dot(a.astype(BF16), b.astype(BF16), preferred_element_type=F32)


def _dotg(a, b, dims):
    return lax.dot_general(a.astype(BF16), b.astype(BF16), dims, preferred_element_type=F32)


def _dot3(a, b):
    ah = a.astype(BF16)
    bh = b.astype(BF16)
    al = (a - ah.astype(F32)).astype(BF16)
    bl = (b - bh.astype(F32)).astype(BF16)
    d = functools.partial(jnp.dot, preferred_element_type=F32)
    return d(ah, bh) + (d(ah, bl) + d(al, bh))


def _silu(x):
    return x * jax.nn.sigmoid(x)


def _softplus(x):
    return jnp.maximum(x, 0.0) + jnp.log1p(jnp.exp(-jnp.abs(x)))


def _rms(x):
    return x * lax.rsqrt(jnp.mean(x * x, axis=-1, keepdims=True) + EPS)


def _shift_rows(x, s, fill):
    n = x.shape[0]
    if s % SUBLANES == 0:
        return jnp.concatenate([jnp.full((s, x.shape[1]), fill, x.dtype), x[:n - s]], axis=0)
    row = lax.broadcasted_iota(jnp.int32, x.shape, 0)
    return jnp.where(row < s, fill, pltpu.roll(x, s, 0))


def _causal_conv(x, tail, w):
    n = x.shape[0]
    row = lax.broadcasted_iota(jnp.int32, (SUBLANES, x.shape[1]), 0)
    y = x * w[CONV_K - 1:CONV_K]
    for s in range(1, CONV_K):
        xs = pltpu.roll(x, s, 0)
        head = jnp.where(row < s, pltpu.roll(tail, s, 0), xs[:SUBLANES])
        xs = jnp.concatenate([head, xs[SUBLANES:]], axis=0) if n > SUBLANES else head
        y = y + xs * w[CONV_K - 1 - s:CONV_K - s]
    return y


def _mod_kernel(c_ref, w_ref, b_ref, o_ref):
    c = c_ref[...]
    o_ref[...] = _dot3(_silu(c), w_ref[...]) + b_ref[...]


def ada_mod(c, ada_w, ada_b):
    L, D, N = ada_w.shape
    tn = MOD_TN
    c8 = jnp.broadcast_to(c.astype(F32), (SUBLANES, D))
    out = pl.pallas_call(
        _mod_kernel,
        grid=(L, N // tn),
        in_specs=[pl.BlockSpec((SUBLANES, D), lambda l, j: (0, 0)),
                  pl.BlockSpec((None, D, tn), lambda l, j: (l, 0, j)),
                  pl.BlockSpec((None, 1, tn), lambda l, j: (l, 0, j))],
        out_specs=pl.BlockSpec((None, SUBLANES, tn), lambda l, j: (l, 0, j)),
        out_shape=jax.ShapeDtypeStruct((L, SUBLANES, N), F32),
        compiler_params=_params("parallel", "parallel"),
        name="ada_mod",
    )(c8, ada_w, ada_b.reshape(L, 1, N))
    return out[:, 0, :]


def _norm_proj_kernel(x_ref, g_ref, sc_ref, sh_ref, w_ref, wab_ref, o_ref, ab_ref, h_ref):
    @pl.when(pl.program_id(1) == 0)
    def _():
        h = (_rms(x_ref[...]) * g_ref[...]) * (1.0 + sc_ref[...]) + sh_ref[...]
        h_ref[...] = h.astype(BF16)
        ab_ref[...] = jnp.dot(h_ref[...], wab_ref[...], preferred_element_type=F32)

    o_ref[...] = jnp.dot(h_ref[...], w_ref[...], preferred_element_type=F32)


def norm_proj(x, g, sc, sh, w, w_ab, layer):
    S, D = x.shape
    tm = min(PROJ_ROWS, S)
    vec = pl.BlockSpec((1, D), lambda i, j: (0, 0))
    return pl.pallas_call(
        _norm_proj_kernel,
        grid=(S // tm, MAIN_COLS // PROJ_TN),
        in_specs=[pl.BlockSpec((tm, D), lambda i, j: (i, 0)), vec, vec, vec,
                  pl.BlockSpec((None, D, PROJ_TN), lambda i, j: (layer, 0, j)),
                  pl.BlockSpec((None, D, LANES), lambda i, j: (layer, 0, 0))],
        out_specs=[pl.BlockSpec((tm, PROJ_TN), lambda i, j: (i, j)),
                   pl.BlockSpec((tm, LANES), lambda i, j: (i, 0))],
        out_shape=[jax.ShapeDtypeStruct((S, MAIN_COLS), F32), jax.ShapeDtypeStruct((S, LANES), F32)],
        scratch_shapes=[pltpu.VMEM((tm, D), BF16)],
        compiler_params=_params("parallel", "arbitrary"),
        name="norm_proj",
    )(x, g, sc, sh, w, w_ab)


def _gate_proj_kernel(x_ref, g_ref, sc_ref, sh_ref, w_ref, o_ref, h_ref):
    @pl.when(pl.program_id(1) == 0)
    def _():
        h = (_rms(x_ref[...]) * g_ref[...]) * (1.0 + sc_ref[...]) + sh_ref[...]
        h_ref[...] = h.astype(BF16)

    rows = GATE_ROW_CHUNK
    for r in range(h_ref.shape[0] // rows):
        rs = slice(r * rows, (r + 1) * rows)
        o_ref[rs] = jax.nn.sigmoid(jnp.dot(h_ref[rs], w_ref[...], preferred_element_type=F32)).astype(BF16)


def gate_proj(x, g, sc, sh, w_gates, layer):
    S, D = x.shape
    tm = min(PROJ_ROWS, S)
    tn = GATE_TN
    vec = pl.BlockSpec((1, D), lambda i, j: (0, 0))
    return pl.pallas_call(
        _gate_proj_kernel,
        grid=(S // tm, GATE_COLS // tn),
        in_specs=[pl.BlockSpec((tm, D), lambda i, j: (i, 0)), vec, vec, vec,
                  pl.BlockSpec((None, D, tn), lambda i, j: (layer, 0, j))],
        out_specs=pl.BlockSpec((tm, tn), lambda i, j: (i, j)),
        out_shape=jax.ShapeDtypeStruct((S, GATE_COLS), BF16),
        scratch_shapes=[pltpu.VMEM((tm, D), BF16)],
        compiler_params=_params("parallel", "arbitrary"),
        name="gate_proj",
    )(x, g, sc, sh, w_gates)


def _ret_tables():
    H, C = RET_HEADS, RET_CHUNK
    log_g = np.log1p(-np.exp2(-5.0 - np.arange(H, dtype=np.float64)))
    pos = np.arange(C, dtype=np.float64)
    rel = pos[:, None] - pos[None, :]
    intra = np.where(rel >= 0, np.exp(log_g[:, None, None] * np.maximum(rel, 0.0)), 0.0)
    qd = np.exp(log_g[:, None] * (pos + 1.0))
    kd = np.exp(log_g[:, None] * (C - 1.0 - pos))
    ones = np.ones((1, 1, LANES))
    cd = [float(np.exp(lg * C)) for lg in log_g]
    return (jnp.asarray(intra, F32), jnp.asarray(qd[:, :, None] * ones, F32),
            jnp.asarray(kd[:, :, None] * ones, F32), cd)


def _ret_kernel(cd, q_ref, k_ref, v_ref, g_ref, cos_ref, sin_ref, intra_ref, qd_ref, kd_ref,
                o_ref, state_ref):
    @pl.when(pl.program_id(0) == 0)
    def _():
        state_ref[...] = jnp.zeros_like(state_ref)

    C = RET_CHUNK
    half = RET_DK // 2
    for c in range(q_ref.shape[0] // C):
        rs = slice(c * C, (c + 1) * C)
        cos2 = cos_ref[rs]
        sin2 = sin_ref[rs]
        for h in range(RET_HEADS):
            sl = slice(h * RET_DK, (h + 1) * RET_DK)
            q = q_ref[rs, sl]
            k = k_ref[rs, sl]
            v = v_ref[rs, sl]
            q = q * cos2 + pltpu.roll(q, half, 1) * sin2
            k = (k * cos2 + pltpu.roll(k, half, 1) * sin2) * (RET_DK ** -0.5)
            scores = _dotg(q, k, NT_DIMS) * intra_ref[h]
            state = state_ref[h]
            o = _dot(scores, v) + _dot(q * qd_ref[h], state)
            state_ref[h] = state * cd[h] + _dotg(k * kd_ref[h], v, TN_DIMS)
            o_ref[rs, sl] = (_rms(o) * _silu(g_ref[rs, sl])).astype(o_ref.dtype)


def retention(z, cos2, sin2):
    S = z.shape[0]
    C = min(RET_ROWS, S)
    intra, qd, kd, cd = _ret_tables()
    col = lambda b: pl.BlockSpec((C, HEAD_W), lambda n, b=b: (n, b))
    tab = lambda a: pl.BlockSpec(a.shape, lambda n: (0, 0, 0))
    rot = pl.BlockSpec((C, RET_DK), lambda n: (n, 0))
    return pl.pallas_call(
        functools.partial(_ret_kernel, cd),
        grid=(S // C,),
        in_specs=[col(0), col(1), col(2), col(3), rot, rot, tab(intra), tab(qd), tab(kd)],
        out_specs=pl.BlockSpec((C, HEAD_W), lambda n: (n, 0)),
        out_shape=jax.ShapeDtypeStruct((S, HEAD_W), BF16),
        scratch_shapes=[pltpu.VMEM((RET_HEADS, RET_DK, RET_DK), F32)],
        compiler_params=_params("arbitrary"),
        name="retention",
    )(z, z, z, z, cos2, sin2, intra, qd, kd)


def _lru_kernel(x_ref, gate_ref, cw_ref, cb_ref, wa_ref, ba_ref, wx_ref, bx_ref, lam_ref,
                o_ref, tail_ref, carry_ref):
    @pl.when(pl.program_id(0) == 0)
    def _():
        tail_ref[...] = jnp.zeros_like(tail_ref)
        carry_ref[...] = jnp.zeros_like(carry_ref)

    tm = x_ref.shape[0]
    for j in range(LRU_BLOCKS):
        sl = slice(j * LRU_BS, (j + 1) * LRU_BS)
        x = x_ref[:, sl]
        xc = _causal_conv(x, tail_ref[:, sl], cw_ref[:, sl]) + cb_ref[:, sl]
        tail_ref[:, sl] = x[tm - SUBLANES:]
        r = jax.nn.sigmoid(_dot(xc, wa_ref[j]) + ba_ref[:, sl])
        i = jax.nn.sigmoid(_dot(xc, wx_ref[j]) + bx_ref[:, sl])
        log_a = (-LRU_C * r) * _softplus(-lam_ref[:, sl])
        a = jnp.exp(log_a)
        t = jnp.tanh(log_a)
        b = jnp.sqrt(-2.0 * t / (1.0 - t)) * (i * xc)
        s = 1
        while s < tm:
            b = b + a * _shift_rows(b, s, 0.0)
            a = a * _shift_rows(a, s, 1.0)
            s *= 2
        h = b + a * carry_ref[0:1, sl]
        carry_ref[0:1, sl] = h[tm - 1:tm]
        g = gate_ref[:, sl]
        gelu = g * (0.5 * (1.0 + jnp.tanh(math.sqrt(2.0 / math.pi) * (g + 0.044715 * (g * g * g)))))
        o_ref[:, sl] = (h * gelu).astype(o_ref.dtype)


def rglru(z, conv_w, conv_b, wa, ba, wx, bx, lam):
    S = z.shape[0]
    tm = min(LRU_ROWS, S)
    row = lambda a: a.reshape(1, LRU_W)
    vec = pl.BlockSpec((1, LRU_W), lambda n: (0, 0))
    wspec = pl.BlockSpec((LRU_BLOCKS, LRU_BS, LRU_BS), lambda n: (0, 0, 0))
    return pl.pallas_call(
        _lru_kernel,
        grid=(S // tm,),
        in_specs=[pl.BlockSpec((tm, HEAD_W), lambda n: (n, 4)),
                  pl.BlockSpec((tm, HEAD_W), lambda n: (n, 5)),
                  pl.BlockSpec((CONV_K, LRU_W), lambda n: (0, 0)), vec, wspec, vec, wspec, vec, vec],
        out_specs=pl.BlockSpec((tm, HEAD_W), lambda n: (n, 0)),
        out_shape=jax.ShapeDtypeStruct((S, HEAD_W), BF16),
        scratch_shapes=[pltpu.VMEM((SUBLANES, LRU_W), F32), pltpu.VMEM((SUBLANES, LRU_W), F32)],
        compiler_params=_params("arbitrary"),
        name="rglru",
    )(z, z, conv_w, row(conv_b), wa, row(ba), wx, row(bx), row(lam))


def _gdn_tables():
    C = GDN_CHUNK
    i = np.arange(C)[:, None]
    j = np.arange(C)[None, :]
    levels = []
    s = 1
    while s < C:
        levels.append((i // (2 * s) == j // (2 * s)) & (i % (2 * s) >= s) & (j % (2 * s) < s))
        s *= 2
    return jnp.asarray(np.stack(levels), F32), jnp.asarray(i >= j, F32), jnp.asarray(i <= j, F32)


def _gdn_kernel(q_ref, k_ref, v_ref, z_ref, ab_ref, cw_ref, alog_ref, dtb_ref, ng_ref, lvl_ref, tri_ref,
                trit_ref, o_ref, tail_ref, state_ref):
    @pl.when(pl.program_id(0) == 0)
    def _():
        tail_ref[...] = jnp.zeros_like(tail_ref)
        state_ref[...] = jnp.zeros_like(state_ref)

    C = GDN_CHUNK
    H = GDN_HEADS
    DK = GDN_DK
    rows = q_ref.shape[0]
    incl = tri_ref[...]
    rowi = lax.broadcasted_iota(jnp.int32, (C, C), 0)
    coli = lax.broadcasted_iota(jnp.int32, (C, C), 1)
    eye = (rowi == coli).astype(F32)

    ab = ab_ref[...]
    g_all = -jnp.exp(alog_ref[...]) * _softplus(ab + dtb_ref[...])
    beta_all = jax.nn.sigmoid(ab)

    conv = []
    for p, ref in enumerate((q_ref, k_ref, v_ref)):
        x = ref[...]
        conv.append(_silu(_causal_conv(x, tail_ref[p], cw_ref[p])))
        tail_ref[p] = x[rows - SUBLANES:]
    qc, kc, vc = conv

    ones = jnp.ones((DK, DK), BF16)

    def unit_rows(x, scale):
        ssq = jnp.concatenate(
            [jnp.dot((x[:, h * DK:(h + 1) * DK] ** 2).astype(BF16), ones, preferred_element_type=F32)
             for h in range(H)], axis=1)
        return x * (lax.rsqrt(ssq + EPS) * scale)
    qc = unit_rows(qc, DK ** -0.5)
    kc = unit_rows(kc, 1.0)

    pairs = [(c, h) for c in range(rows // C) for h in range(H)]
    st = {}
    for c in range(rows // C):
        g_c = g_all[c * C:(c + 1) * C]
        st[c] = (_dot32(incl, g_c),
                 _dotg32(g_c, trit_ref[...], TN_DIMS))
    for c, h in pairs:
        rs = slice(c * C, (c + 1) * C)
        sl = slice(h * DK, (h + 1) * DK)
        q = qc[rs, sl]
        k = kc[rs, sl]
        beta = beta_all[rs, H + h:H + h + 1]
        gc = st[c][0][:, h:h + 1]
        gcr = st[c][1][h:h + 1, :]
        decay = incl * jnp.exp(jnp.where(incl > 0, gc - gcr, 0.0))
        kb = k * beta
        egc = jnp.exp(gc)
        gl = gc[C - 1:C]
        qk = _dotg(jnp.concatenate([q, kb], axis=0), k, NT_DIMS)
        st[c, h] = dict(
            attn=qk[:C] * decay,
            m=qk[C:] * (decay - eye),
            rhs=jnp.concatenate([vc[rs, sl] * beta, kb * egc], axis=1),
            qdec=q * egc,
            kdt=(k * jnp.exp(gl - gc)).T,
            egl=jnp.exp(gl))
    for p in pairs:
        st[p]['inv'] = eye - st[p]['m'] * lvl_ref[0]
    for lv in range(1, lvl_ref.shape[0]):
        for p in pairs:
            st[p]['t'] = _dot(st[p]['m'] * lvl_ref[lv], st[p]['inv'])
        for p in pairs:
            st[p]['inv'] = st[p]['inv'] - _dot(st[p]['inv'], st[p]['t'])
    for p in pairs:
        st[p]['uw'] = _dot(st[p]['inv'], st[p]['rhs'])
    for c, h in pairs:
        d = st[c, h]
        state = state_ref[h]
        ws = _dot(jnp.concatenate([d['uw'][:, DK:], d['qdec']], axis=0), state)
        v_new = d['uw'][:, :DK] - ws[:C]
        o = ws[C:] + _dot(d['attn'], v_new)
        state_ref[h] = state * d['egl'] + _dot(d['kdt'], v_new)
        rs = slice(c * C, (c + 1) * C)
        sl = slice(h * DK, (h + 1) * DK)
        o_ref[rs, sl] = (_rms(o) * ng_ref[...] * _silu(z_ref[rs, sl])).astype(o_ref.dtype)


def gdn(z, z_ab, conv_w, a_log, dt_bias, norm_g):
    S = z.shape[0]
    C = GDN_CHUNK
    H = GDN_HEADS
    lvl, tri, trit = _gdn_tables()
    alog = jnp.zeros((1, LANES), F32).at[0, :H].set(a_log)
    dtb = jnp.zeros((1, LANES), F32).at[0, :H].set(dt_bias)
    cw = conv_w.reshape(CONV_K, 3, HEAD_W).transpose(1, 0, 2)
    tm = min(GDN_ROWS, S)
    col = lambda b: pl.BlockSpec((tm, HEAD_W), lambda n, b=b: (n, b))
    lane = pl.BlockSpec((1, LANES), lambda n: (0, 0))
    return pl.pallas_call(
        _gdn_kernel,
        grid=(S // tm,),
        in_specs=[col(6), col(7), col(8), col(9),
                  pl.BlockSpec((tm, LANES), lambda n: (n, 0)),
                  pl.BlockSpec((3, CONV_K, HEAD_W), lambda n: (0, 0, 0)),
                  lane, lane, lane,
                  pl.BlockSpec(lvl.shape, lambda n: (0, 0, 0)),
                  pl.BlockSpec(tri.shape, lambda n: (0, 0)),
                  pl.BlockSpec(trit.shape, lambda n: (0, 0))],
        out_specs=pl.BlockSpec((tm, HEAD_W), lambda n: (n, 0)),
        out_shape=jax.ShapeDtypeStruct((S, HEAD_W), BF16),
        scratch_shapes=[pltpu.VMEM((3, SUBLANES, HEAD_W), F32),
                        pltpu.VMEM((H, GDN_DK, GDN_DK), F32)],
        compiler_params=_params("arbitrary"),
        name="gdn",
    )(z, z, z, z, z_ab, cw, alog, dtb, norm_g.reshape(1, GDN_DK), lvl, tri, trit)


ROUTE_LANES = LANES


def _merge_kernel(yr_ref, yl_ref, yg_ref, g0_ref, g1_ref, g2_ref, x_ref, gt_ref, bp_ref, wo_ref,
                  ng_ref, sc_ref, sh_ref, wr_ref, br_ref, xo_ref, h_ref, lg_ref):
    merged = g0_ref[...] * jnp.dot(yr_ref[...], bp_ref[0], preferred_element_type=F32)
    merged += g1_ref[...] * jnp.dot(yl_ref[...], bp_ref[1], preferred_element_type=F32)
    merged += g2_ref[...] * jnp.dot(yg_ref[...], bp_ref[2], preferred_element_type=F32)
    out = jnp.dot(merged.astype(BF16), wo_ref[...], preferred_element_type=F32)
    x = x_ref[...] + gt_ref[...] * out
    xo_ref[...] = x
    h = (_rms(x) * ng_ref[...]) * (1.0 + sc_ref[...]) + sh_ref[...]
    h_ref[...] = h
    lg_ref[...] = _dot3(h, wr_ref[...]) + br_ref[...]


def merge(y_ret, y_lru, y_gdn, gates, x, gt1, bp, w_out, norm2_g, sc2, sh2, w_route, b_route):
    S, D = x.shape
    tm = min(MERGE_ROWS, S)
    ycol = pl.BlockSpec((tm, HEAD_W), lambda i: (i, 0))
    gate = lambda b: pl.BlockSpec((tm, D), lambda i, b=b: (i, b))
    rows = pl.BlockSpec((tm, D), lambda i: (i, 0))
    vec = pl.BlockSpec((1, D), lambda i: (0, 0))
    once = dict(pipeline_mode=pl.Buffered(1))
    return pl.pallas_call(
        _merge_kernel,
        grid=(S // tm,),
        in_specs=[ycol, ycol, ycol, gate(0), gate(1), gate(2), rows, vec,
                  pl.BlockSpec((N_BRANCH, HEAD_W, D), lambda i: (0, 0, 0), **once),
                  pl.BlockSpec((D, D), lambda i: (0, 0), **once),
                  vec, vec, vec,
                  pl.BlockSpec((D, ROUTE_LANES), lambda i: (0, 0)),
                  pl.BlockSpec((1, ROUTE_LANES), lambda i: (0, 0))],
        out_specs=[rows, rows, pl.BlockSpec((tm, ROUTE_LANES), lambda i: (i, 0))],
        out_shape=[jax.ShapeDtypeStruct((S, D), F32), jax.ShapeDtypeStruct((S, D), F32),
                   jax.ShapeDtypeStruct((S, ROUTE_LANES), F32)],
        compiler_params=_params("parallel"),
        name="merge",
    )(y_ret, y_lru, y_gdn, gates, gates, gates, x, gt1, bp, w_out, norm2_g, sc2, sh2, w_route, b_route)


R_E0, R_E1, R_W0, R_W1, R_RANK0, R_RANK1 = range(6)


def _first_max(vals, lane):
    m = jnp.max(vals, axis=-1, keepdims=True)
    idx = jnp.min(jnp.where(vals == m, lane, ROUTE_LANES), axis=-1, keepdims=True)
    return m, idx


def _route_kernel(lg_ref, o_ref, cnt_ref, carry_ref):
    @pl.when(pl.program_id(0) == 0)
    def _():
        carry_ref[...] = jnp.zeros_like(carry_ref)

    lg = lg_ref[...]
    tm = lg.shape[0]
    lane = lax.broadcasted_iota(jnp.int32, lg.shape, 1)
    neg = jnp.float32(-jnp.inf)
    grp = jnp.where(lane < N_GROUPS, lg, neg)
    gmax, gsel = _first_max(grp, lane)
    pg_top = 1.0 / jnp.sum(jnp.exp(grp - gmax), axis=-1, keepdims=True)
    lo = N_GROUPS + gsel * EXPERTS_PER_GROUP
    inside = jnp.where((lane >= lo) & (lane < lo + EXPERTS_PER_GROUP), lg, neg)
    v0, i0 = _first_max(inside, lane)
    v1, i1 = _first_max(jnp.where(lane == i0, neg, inside), lane)
    e1w = jnp.exp(v1 - v0)
    w0 = pg_top / (1.0 + e1w)
    w1 = pg_top * e1w / (1.0 + e1w)
    hit0 = lane == i0
    hit1 = lane == i1
    hits = (hit0 | hit1).astype(BF16)
    row = lax.broadcasted_iota(jnp.int32, (tm, tm), 0)
    col = lax.broadcasted_iota(jnp.int32, (tm, tm), 1)
    before = jnp.dot((row > col).astype(BF16), hits, preferred_element_type=F32) + carry_ref[0:1, :]
    rank0 = jnp.sum(jnp.where(hit0, before, 0.0), axis=-1, keepdims=True)
    rank1 = jnp.sum(jnp.where(hit1, before, 0.0), axis=-1, keepdims=True)
    total = carry_ref[0:1, :] + jnp.sum(hits.astype(F32), axis=0, keepdims=True)
    carry_ref[0:1, :] = total
    cnt_ref[...] = jnp.broadcast_to(total, cnt_ref.shape)
    rec = jnp.zeros(lg.shape, F32)
    for slot, val in ((R_E0, (i0 - N_GROUPS).astype(F32)), (R_E1, (i1 - N_GROUPS).astype(F32)),
                      (R_W0, w0), (R_W1, w1), (R_RANK0, rank0), (R_RANK1, rank1)):
        rec = jnp.where(lane == slot, val, rec)
    o_ref[...] = rec


def route(logits):
    S = logits.shape[0]
    tm = min(ROUTE_ROWS, S)
    return pl.pallas_call(
        _route_kernel,
        grid=(S // tm,),
        in_specs=[pl.BlockSpec((tm, ROUTE_LANES), lambda i: (i, 0))],
        out_specs=[pl.BlockSpec((tm, ROUTE_LANES), lambda i: (i, 0)),
                   pl.BlockSpec((SUBLANES, ROUTE_LANES), lambda i: (0, 0))],
        out_shape=[jax.ShapeDtypeStruct((S, ROUTE_LANES), F32),
                   jax.ShapeDtypeStruct((SUBLANES, ROUTE_LANES), F32)],
        scratch_shapes=[pltpu.VMEM((SUBLANES, ROUTE_LANES), F32)],
        compiler_params=_params("arbitrary"),
        name="route",
    )(logits)


def _dispatch_kernel(nt, pad0_ref, padn_ref, padtot_ref, nused_ref, d0_ref, d1_ref, h_ref, xs_hbm,
                     buf, zbuf, sem, fsem):
    i = pl.program_id(0)
    tm = h_ref.shape[0]
    B = EXPERT_ROWS
    nb = xs_hbm.shape[0] // B
    slot = i % 2

    def sent_wait(s):
        for _ in range(2):
            pltpu.make_async_copy(buf.at[s], xs_hbm.at[pl.ds(0, tm)], sem.at[s]).wait()

    @pl.when(i >= 2)
    def _():
        sent_wait(slot)

    buf[slot] = h_ref[...]
    for idx_ref in (d0_ref, d1_ref):
        for r in range(tm):
            pltpu.make_async_copy(buf.at[slot, pl.ds(r, 1)], xs_hbm.at[pl.ds(idx_ref[0, r], 1)],
                                  sem.at[slot]).start()

    @pl.when(i == 0)
    def _():
        zbuf[...] = jnp.zeros(zbuf.shape, zbuf.dtype)

        def fill_block(b, c):
            pltpu.make_async_copy(zbuf, xs_hbm.at[pl.ds(pl.multiple_of(b * B, B), B)], fsem).start()
            return c
        lax.fori_loop(nused_ref[0], nb, fill_block, 0)
        for e in range(N_EXPERTS):
            def fill_row(r, c, e=e):
                pltpu.make_async_copy(zbuf.at[pl.ds(0, 1)], xs_hbm.at[pl.ds(pad0_ref[e] + r, 1)], fsem).start()
                return c
            lax.fori_loop(0, padn_ref[e], fill_row, 0)

        def wait_block(b, c):
            pltpu.make_async_copy(zbuf, xs_hbm.at[pl.ds(0, B)], fsem).wait()
            return c
        lax.fori_loop(nused_ref[0], nb, wait_block, 0)

        def wait_row(r, c):
            pltpu.make_async_copy(zbuf.at[pl.ds(0, 1)], xs_hbm.at[pl.ds(0, 1)], fsem).wait()
            return c
        lax.fori_loop(0, padtot_ref[0], wait_row, 0)

    @pl.when(i == nt - 1)
    def _():
        sent_wait(slot)
        if nt >= 2:
            sent_wait(1 - slot)


def dispatch(h, dest, pad0, padn, padtot, nused, nb):
    T, D = h.shape
    tm = min(COMBINE_ROWS, T)
    nt = T // tm
    tbl = pl.BlockSpec((None, 1, tm), lambda i, *_: (i, 0, 0), memory_space=pltpu.SMEM)
    grid_spec = pltpu.PrefetchScalarGridSpec(
        num_scalar_prefetch=4,
        grid=(nt,),
        in_specs=[tbl, tbl, pl.BlockSpec((tm, D), lambda i, *_: (i, 0))],
        out_specs=pl.BlockSpec(memory_space=pl.ANY),
        scratch_shapes=[pltpu.VMEM((2, tm, D), F32), pltpu.VMEM((EXPERT_ROWS, D), F32),
                        pltpu.SemaphoreType.DMA((2,)), pltpu.SemaphoreType.DMA(())])
    return pl.pallas_call(
        functools.partial(_dispatch_kernel, nt),
        grid_spec=grid_spec,
        out_shape=jax.ShapeDtypeStruct((nb * EXPERT_ROWS, D), F32),
        compiler_params=_params("arbitrary"),
        name="dispatch",
    )(pad0, padn, padtot, nused, dest[:, 0].reshape(nt, 1, tm), dest[:, 1].reshape(nt, 1, tm), h)


def _expert_kernel(be_ref, nused_ref, xs_ref, w1_ref, w3_ref, w2_ref, o_ref, wb1, wb3, wb2):
    i = pl.program_id(0)
    nused = nused_ref[0]

    @pl.when(i < nused)
    def _():
        @pl.when((i == 0) | (be_ref[i] != be_ref[jnp.maximum(i - 1, 0)]))
        def _():
            wb1[...] = w1_ref[...].astype(BF16)
            wb3[...] = w3_ref[...].astype(BF16)
            wb2[...] = w2_ref[...].astype(BF16)

        x = xs_ref[...].astype(BF16)
        a1 = jnp.dot(x, wb1[...], preferred_element_type=F32)
        a3 = jnp.dot(x, wb3[...], preferred_element_type=F32)
        mid = (_silu(a1) * a3).astype(BF16)
        o_ref[...] = jnp.dot(mid, wb2[...], preferred_element_type=F32)

    @pl.when(i >= nused)
    def _():
        o_ref[...] = jnp.zeros(o_ref.shape, o_ref.dtype)


def experts(xs, block_e, nused, w1, w3, w2, layer):
    P, D = xs.shape
    B = EXPERT_ROWS
    nb = P // B
    wspec = lambda shp: pl.BlockSpec((None, None) + shp, lambda i, be, nu: (layer, be[i], 0, 0))
    grid_spec = pltpu.PrefetchScalarGridSpec(
        num_scalar_prefetch=2,
        grid=(nb,),
        in_specs=[pl.BlockSpec((B, D), lambda i, be, nu: (jnp.minimum(i, nu[0] - 1), 0)),
                  wspec((D, D_EXPERT)), wspec((D, D_EXPERT)), wspec((D_EXPERT, D))],
        out_specs=pl.BlockSpec((B, D), lambda i, be, nu: (i, 0)),
        scratch_shapes=[pltpu.VMEM((D, D_EXPERT), BF16), pltpu.VMEM((D, D_EXPERT), BF16),
                        pltpu.VMEM((D_EXPERT, D), BF16)])
    return pl.pallas_call(
        _expert_kernel,
        grid_spec=grid_spec,
        out_shape=jax.ShapeDtypeStruct((P, D), F32),
        compiler_params=_params("arbitrary"),
        name="experts",
    )(block_e, nused, xs, w1, w3, w2)


def dispatch_plan(rec, counts, T):
    B = EXPERT_ROWS
    nb = 2 * T // B + N_EXPERTS
    cnt = counts[0, N_GROUPS:N_GROUPS + N_EXPERTS].astype(jnp.int32)
    padded = ((cnt + B - 1) // B) * B
    pend = jnp.cumsum(padded)
    pstart = pend - padded
    e = rec[:, R_E0:R_E1 + 1].astype(jnp.int32)
    rank = rec[:, R_RANK0:R_RANK1 + 1].astype(jnp.int32)
    dest = pstart[e] + rank
    first = jnp.arange(nb, dtype=jnp.int32) * B
    block_e = jnp.minimum(jnp.sum((first[:, None] >= pend[None, :]).astype(jnp.int32), axis=1),
                          N_EXPERTS - 1)
    nused = (pend[-1] // B).astype(jnp.int32).reshape(1)
    pad0 = pstart + cnt
    padn = padded - cnt
    return block_e, nused, dest, pad0, padn, jnp.sum(padn).reshape(1), nb


def _combine_kernel(final, d0_ref, d1_ref, nd0_ref, nd1_ref, x_ref, rec_ref, gt_ref, fg_ref, y_hbm,
                    o_ref, ybuf, sem):
    i = pl.program_id(0)
    tm = x_ref.shape[0]
    slot = i % 2

    def fetch(r0_ref, r1_ref, s):
        for k, idx_ref in enumerate((r0_ref, r1_ref)):
            for r in range(tm):
                pltpu.make_async_copy(y_hbm.at[pl.ds(idx_ref[0, r], 1)], ybuf.at[s, k, pl.ds(r, 1)],
                                      sem.at[s]).start()

    @pl.when(i == 0)
    def _():
        fetch(d0_ref, d1_ref, 0)

    @pl.when(i + 1 < pl.num_programs(0))
    def _():
        fetch(nd0_ref, nd1_ref, 1 - slot)

    for k in range(2):
        pltpu.make_async_copy(y_hbm.at[pl.ds(0, tm)], ybuf.at[slot, k], sem.at[slot]).wait()
    rec = rec_ref[...]
    moe = rec[:, R_W0:R_W0 + 1] * ybuf[slot, 0] + rec[:, R_W1:R_W1 + 1] * ybuf[slot, 1]
    x = x_ref[...] + gt_ref[...] * moe
    o_ref[...] = _rms(x) * fg_ref[...] if final else x


def combine(x, y, dest, rec, gt2, final_g, final):
    S, D = x.shape
    tm = min(COMBINE_ROWS, S)
    nt = S // tm
    d0 = dest[:, 0].reshape(nt, 1, tm)
    d1 = dest[:, 1].reshape(nt, 1, tm)
    rows = pl.BlockSpec((tm, D), lambda i: (i, 0))
    vec = pl.BlockSpec((1, D), lambda i: (0, 0))
    cur = pl.BlockSpec((None, 1, tm), lambda i: (i, 0, 0), memory_space=pltpu.SMEM)
    nxt = pl.BlockSpec((None, 1, tm), lambda i: (jnp.minimum(i + 1, nt - 1), 0, 0), memory_space=pltpu.SMEM)
    return pl.pallas_call(
        functools.partial(_combine_kernel, final),
        grid=(nt,),
        in_specs=[cur, cur, nxt, nxt, rows, pl.BlockSpec((tm, ROUTE_LANES), lambda i: (i, 0)), vec, vec,
                  pl.BlockSpec(memory_space=pl.ANY)],
        out_specs=rows,
        out_shape=jax.ShapeDtypeStruct((S, D), F32),
        scratch_shapes=[pltpu.VMEM((2, 2, tm, D), F32), pltpu.SemaphoreType.DMA((2,))],
        compiler_params=_params("arbitrary"),
        name="combine",
    )(d0, d1, d0, d1, x, rec, gt2, final_g, y)


def _rope_tables(positions):
    half = RET_DK // 2
    inv_freq = ROPE_BASE ** (-jnp.arange(half, dtype=F32) * (2.0 / RET_DK))
    ang = positions.astype(F32)[:, None] * inv_freq
    cos = jnp.cos(ang)
    sin = jnp.sin(ang)
    return jnp.concatenate([cos, cos], axis=-1), jnp.concatenate([-sin, sin], axis=-1)


def _proj_weights(w_in):
    L, D, _ = w_in.shape
    w_all = w_in.astype(BF16)
    ab = w_all[:, :, MAIN_COLS:MAIN_COLS + N_AB]
    w_ab = jnp.concatenate([ab, jnp.zeros((L, D, LANES - N_AB), BF16)], axis=2)
    w_gates = w_all[:, :, MAIN_COLS + N_AB:]
    return w_all, w_ab, w_gates


def kernel(x, c, positions, ada_w, ada_b, norm1_g, norm2_g, w_in, lru_conv_w, lru_conv_b, lru_wa, lru_ba,
           lru_wx, lru_bx, lru_lambda, gdn_conv_w, gdn_a_log, gdn_dt_bias, gdn_norm_g, branch_proj, w_out,
           router_grp_w, router_grp_b, router_exp_w, router_exp_b, moe_w1, moe_w3, moe_w2, final_g):
    B, S, D = x.shape
    assert B == 1
    xs = x.reshape(S, D)
    mod = ada_mod(c, ada_w, ada_b)
    cos2, sin2 = _rope_tables(positions.reshape(S))
    row = lambda v: v.reshape(1, -1).astype(F32)
    w_all, w_ab, w_gates = _proj_weights(w_in)
    for l in range(DEPTH):
        sh1, sc1, gt1, sh2, sc2, gt2 = (row(m) for m in jnp.split(mod[l], 6))
        z, z_ab = norm_proj(xs, row(norm1_g[l]), sc1, sh1, w_all, w_ab, l)
        gates = gate_proj(xs, row(norm1_g[l]), sc1, sh1, w_gates, l)
        y_ret = retention(z, cos2, sin2)
        y_lru = rglru(z, lru_conv_w[l], lru_conv_b[l], lru_wa[l], lru_ba[l], lru_wx[l], lru_bx[l],
                      lru_lambda[l])
        y_gdn = gdn(z, z_ab, gdn_conv_w[l], gdn_a_log[l], gdn_dt_bias[l], gdn_norm_g[l])
        n_r = N_GROUPS + N_EXPERTS
        w_route = jnp.concatenate([router_grp_w[l], router_exp_w[l],
                                   jnp.zeros((D, ROUTE_LANES - n_r), F32)], axis=1)
        b_route = jnp.concatenate([router_grp_b[l], router_exp_b[l],
                                   jnp.zeros((ROUTE_LANES - n_r,), F32)]).reshape(1, ROUTE_LANES)
        xs, h2, logits = merge(y_ret, y_lru, y_gdn, gates, xs, gt1, branch_proj[l].astype(BF16),
                               w_out[l].astype(BF16), row(norm2_g[l]), sc2, sh2, w_route, b_route)
        rec, counts = route(logits)
        block_e, nused, dest, pad0, padn, padtot, nb = dispatch_plan(rec, counts, S)
        xb = dispatch(h2, dest, pad0, padn, padtot, nused, nb)
        y = experts(xb, block_e, nused, moe_w1, moe_w3, moe_w2, l)
        xs = combine(xs, y, dest, rec, gt2, row(final_g), final=(l == DEPTH - 1))
    return xs.reshape(B, S, D)
```

```python
import functools
import math

import numpy as np
import jax
import jax.numpy as jnp
from jax import lax
from jax.experimental import pallas as pl
from jax.experimental.pallas import tpu as pltpu

F32 = jnp.float32
BF16 = jnp.bfloat16
HIGHEST = lax.Precision.HIGHEST

D_MODEL = 2048
DEPTH = 2
RET_HEADS = 8
RET_DK = 128
RET_CHUNK = 128
ROPE_BASE = 10000.0
LRU_W = 1024
LRU_BLOCKS = 8
LRU_BS = LRU_W // LRU_BLOCKS
LRU_C = 8.0
CONV_K = 4
GDN_HEADS = 8
GDN_DK = 128
GDN_CHUNK = 64
N_BRANCH = 3
N_GROUPS = 4
EXPERTS_PER_GROUP = 8
N_EXPERTS = N_GROUPS * EXPERTS_PER_GROUP
D_EXPERT = 512
EPS = 1e-6

HEAD_W = 1024
MAIN_COLS = 10 * HEAD_W
N_AB = 2 * GDN_HEADS
GATE_COLS = N_BRANCH * D_MODEL
MOD_TN = 1536
PROJ_ROWS = 1024
PROJ_TN = 1280
LRU_ROWS = 256
MERGE_ROWS = 256
ROUTE_ROWS = 512
GATE_TN = 1024
GATE_ROW_CHUNK = 256
LANES = 128
SUBLANES = 8
EXPERT_ROWS = 256
COMBINE_ROWS = 512
GDN_ROWS = 2 * GDN_CHUNK
RET_ROWS = 4 * RET_CHUNK
VMEM_LIMIT = 56 * 1024 * 1024

NT_DIMS = (((1,), (1,)), ((), ()))
TN_DIMS = (((0,), (0,)), ((), ()))


def _params(*sem):
    return pltpu.CompilerParams(dimension_semantics=sem, vmem_limit_bytes=VMEM_LIMIT)


def _dot32(a, b):
    return jnp.dot(a, b, preferred_element_type=F32, precision=HIGHEST)


def _dotg32(a, b, dims):
    return lax.dot_general(a, b, dims, preferred_element_type=F32, precision=HIGHEST)


def _dot(a, b):
    return jnp.dot(a.astype(BF16), b.astype(BF16), preferred_element_type=F32)


def _dotg(a, b, dims):
    return lax.dot_general(a.astype(BF16), b.astype(BF16), dims, preferred_element_type=F32)


def _dot3(a, b):
    ah = a.astype(BF16)
    bh = b.astype(BF16)
    al = (a - ah.astype(F32)).astype(BF16)
    bl = (b - bh.astype(F32)).astype(BF16)
    d = functools.partial(jnp.dot, preferred_element_type=F32)
    return d(ah, bh) + (d(ah, bl) + d(al, bh))


def _silu(x):
    return x * jax.nn.sigmoid(x)


def _softplus(x):
    return jnp.maximum(x, 0.0) + jnp.log1p(jnp.exp(-jnp.abs(x)))


def _rms(x):
    return x * lax.rsqrt(jnp.mean(x * x, axis=-1, keepdims=True) + EPS)


def _shift_rows(x, s, fill):
    n = x.shape[0]
    if s % SUBLANES == 0:
        return jnp.concatenate([jnp.full((s, x.shape[1]), fill, x.dtype), x[:n - s]], axis=0)
    row = lax.broadcasted_iota(jnp.int32, x.shape, 0)
    return jnp.where(row < s, fill, pltpu.roll(x, s, 0))


def _causal_conv(x, tail, w):
    n = x.shape[0]
    row = lax.broadcasted_iota(jnp.int32, (SUBLANES, x.shape[1]), 0)
    y = x * w[CONV_K - 1:CONV_K]
    for s in range(1, CONV_K):
        xs = pltpu.roll(x, s, 0)
        head = jnp.where(row < s, pltpu.roll(tail, s, 0), xs[:SUBLANES])
        xs = jnp.concatenate([head, xs[SUBLANES:]], axis=0) if n > SUBLANES else head
        y = y + xs * w[CONV_K - 1 - s:CONV_K - s]
    return y


def _mod_kernel(c_ref, w_ref, b_ref, o_ref):
    c = c_ref[...]
    o_ref[...] = _dot3(_silu(c), w_ref[...]) + b_ref[...]


def ada_mod(c, ada_w, ada_b):
    L, D, N = ada_w.shape
    tn = MOD_TN
    c8 = jnp.broadcast_to(c.astype(F32), (SUBLANES, D))
    out = pl.pallas_call(
        _mod_kernel,
        grid=(L, N // tn),
        in_specs=[pl.BlockSpec((SUBLANES, D), lambda l, j: (0, 0)),
                  pl.BlockSpec((None, D, tn), lambda l, j: (l, 0, j)),
                  pl.BlockSpec((None, 1, tn), lambda l, j: (l, 0, j))],
        out_specs=pl.BlockSpec((None, SUBLANES, tn), lambda l, j: (l, 0, j)),
        out_shape=jax.ShapeDtypeStruct((L, SUBLANES, N), F32),
        compiler_params=_params("parallel", "parallel"),
        name="ada_mod",
    )(c8, ada_w, ada_b.reshape(L, 1, N))
    return out[:, 0, :]


def _norm_proj_kernel(x_ref, g_ref, sc_ref, sh_ref, w_ref, wab_ref, o_ref, ab_ref, h_ref):
    @pl.when(pl.program_id(1) == 0)
    def _():
        h = (_rms(x_ref[...]) * g_ref[...]) * (1.0 + sc_ref[...]) + sh_ref[...]
        h_ref[...] = h.astype(BF16)
        ab_ref[...] = jnp.dot(h_ref[...], wab_ref[...], preferred_element_type=F32)

    o_ref[...] = jnp.dot(h_ref[...], w_ref[...], preferred_element_type=F32)


def norm_proj(x, g, sc, sh, w, w_ab, layer):
    S, D = x.shape
    tm = min(PROJ_ROWS, S)
    vec = pl.BlockSpec((1, D), lambda i, j: (0, 0))
    return pl.pallas_call(
        _norm_proj_kernel,
        grid=(S // tm, MAIN_COLS // PROJ_TN),
        in_specs=[pl.BlockSpec((tm, D), lambda i, j: (i, 0)), vec, vec, vec,
                  pl.BlockSpec((None, D, PROJ_TN), lambda i, j: (layer, 0, j)),
                  pl.BlockSpec((None, D, LANES), lambda i, j: (layer, 0, 0))],
        out_specs=[pl.BlockSpec((tm, PROJ_TN), lambda i, j: (i, j)),
                   pl.BlockSpec((tm, LANES), lambda i, j: (i, 0))],
        out_shape=[jax.ShapeDtypeStruct((S, MAIN_COLS), F32), jax.ShapeDtypeStruct((S, LANES), F32)],
        scratch_shapes=[pltpu.VMEM((tm, D), BF16)],
        compiler_params=_params("parallel", "arbitrary"),
        name="norm_proj",
    )(x, g, sc, sh, w, w_ab)


def _gate_proj_kernel(x_ref, g_ref, sc_ref, sh_ref, w_ref, o_ref, h_ref):
    @pl.when(pl.program_id(1) == 0)
    def _():
        h = (_rms(x_ref[...]) * g_ref[...]) * (1.0 + sc_ref[...]) + sh_ref[...]
        h_ref[...] = h.astype(BF16)

    rows = GATE_ROW_CHUNK
    for r in range(h_ref.shape[0] // rows):
        rs = slice(r * rows, (r + 1) * rows)
        o_ref[rs] = jax.nn.sigmoid(jnp.dot(h_ref[rs], w_ref[...], preferred_element_type=F32)).astype(BF16)


def gate_proj(x, g, sc, sh, w_gates, layer):
    S, D = x.shape
    tm = min(PROJ_ROWS, S)
    tn = GATE_TN
    vec = pl.BlockSpec((1, D), lambda i, j: (0, 0))
    return pl.pallas_call(
        _gate_proj_kernel,
        grid=(S // tm, GATE_COLS // tn),
        in_specs=[pl.BlockSpec((tm, D), lambda i, j: (i, 0)), vec, vec, vec,
                  pl.BlockSpec((None, D, tn), lambda i, j: (layer, 0, j))],
        out_specs=pl.BlockSpec((tm, tn), lambda i, j: (i, j)),
        out_shape=jax.ShapeDtypeStruct((S, GATE_COLS), BF16),
        scratch_shapes=[pltpu.VMEM((tm, D), BF16)],
        compiler_params=_params("parallel", "arbitrary"),
        name="gate_proj",
    )(x, g, sc, sh, w_gates)


def _ret_tables():
    H, C = RET_HEADS, RET_CHUNK
    log_g = np.log1p(-np.exp2(-5.0 - np.arange(H, dtype=np.float64)))
    pos = np.arange(C, dtype=np.float64)
    rel = pos[:, None] - pos[None, :]
    intra = np.where(rel >= 0, np.exp(log_g[:, None, None] * np.maximum(rel, 0.0)), 0.0)
    qd = np.exp(log_g[:, None] * (pos + 1.0))
    kd = np.exp(log_g[:, None] * (C - 1.0 - pos))
    ones = np.ones((1, 1, LANES))
    cd = [float(np.exp(lg * C)) for lg in log_g]
    return (jnp.asarray(intra, F32), jnp.asarray(qd[:, :, None] * ones, F32),
            jnp.asarray(kd[:, :, None] * ones, F32), cd)


def _ret_kernel(cd, q_ref, k_ref, v_ref, g_ref, cos_ref, sin_ref, intra_ref, qd_ref, kd_ref,
                o_ref, state_ref):
    @pl.when(pl.program_id(0) == 0)
    def _():
        state_ref[...] = jnp.zeros_like(state_ref)

    C = RET_CHUNK
    half = RET_DK // 2
    for c in range(q_ref.shape[0] // C):
        rs = slice(c * C, (c + 1) * C)
        cos2 = cos_ref[rs]
        sin2 = sin_ref[rs]
        for h in range(RET_HEADS):
            sl = slice(h * RET_DK, (h + 1) * RET_DK)
            q = q_ref[rs, sl]
            k = k_ref[rs, sl]
            v = v_ref[rs, sl]
            q = q * cos2 + pltpu.roll(q, half, 1) * sin2
            k = (k * cos2 + pltpu.roll(k, half, 1) * sin2) * (RET_DK ** -0.5)
            scores = _dotg(q, k, NT_DIMS) * intra_ref[h]
            state = state_ref[h]
            o = _dot(scores, v) + _dot(q * qd_ref[h], state)
            state_ref[h] = state * cd[h] + _dotg(k * kd_ref[h], v, TN_DIMS)
            o_ref[rs, sl] = (_rms(o) * _silu(g_ref[rs, sl])).astype(o_ref.dtype)


def retention(z, cos2, sin2):
    S = z.shape[0]
    C = min(RET_ROWS, S)
    intra, qd, kd, cd = _ret_tables()
    col = lambda b: pl.BlockSpec((C, HEAD_W), lambda n, b=b: (n, b))
    tab = lambda a: pl.BlockSpec(a.shape, lambda n: (0, 0, 0))
    rot = pl.BlockSpec((C, RET_DK), lambda n: (n, 0))
    return pl.pallas_call(
        functools.partial(_ret_kernel, cd),
        grid=(S // C,),
        in_specs=[col(0), col(1), col(2), col(3), rot, rot, tab(intra), tab(qd), tab(kd)],
        out_specs=pl.BlockSpec((C, HEAD_W), lambda n: (n, 0)),
        out_shape=jax.ShapeDtypeStruct((S, HEAD_W), BF16),
        scratch_shapes=[pltpu.VMEM((RET_HEADS, RET_DK, RET_DK), F32)],
        compiler_params=_params("arbitrary"),
        name="retention",
    )(z, z, z, z, cos2, sin2, intra, qd, kd)


def _lru_kernel(x_ref, gate_ref, cw_ref, cb_ref, wa_ref, ba_ref, wx_ref, bx_ref, lam_ref,
                o_ref, tail_ref, carry_ref):
    @pl.when(pl.program_id(0) == 0)
    def _():
        tail_ref[...] = jnp.zeros_like(tail_ref)
        carry_ref[...] = jnp.zeros_like(carry_ref)

    tm = x_ref.shape[0]
    for j in range(LRU_BLOCKS):
        sl = slice(j * LRU_BS, (j + 1) * LRU_BS)
        x = x_ref[:, sl]
        xc = _causal_conv(x, tail_ref[:, sl], cw_ref[:, sl]) + cb_ref[:, sl]
        tail_ref[:, sl] = x[tm - SUBLANES:]
        r = jax.nn.sigmoid(_dot(xc, wa_ref[j]) + ba_ref[:, sl])
        i = jax.nn.sigmoid(_dot(xc, wx_ref[j]) + bx_ref[:, sl])
        log_a = (-LRU_C * r) * _softplus(-lam_ref[:, sl])
        a = jnp.exp(log_a)
        t = jnp.tanh(log_a)
        b = jnp.sqrt(-2.0 * t / (1.0 - t)) * (i * xc)
        s = 1
        while s < tm:
            b = b + a * _shift_rows(b, s, 0.0)
            a = a * _shift_rows(a, s, 1.0)
            s *= 2
        h = b + a * carry_ref[0:1, sl]
        carry_ref[0:1, sl] = h[tm - 1:tm]
        g = gate_ref[:, sl]
        gelu = g * (0.5 * (1.0 + jnp.tanh(math.sqrt(2.0 / math.pi) * (g + 0.044715 * (g * g * g)))))
        o_ref[:, sl] = (h * gelu).astype(o_ref.dtype)


def rglru(z, conv_w, conv_b, wa, ba, wx, bx, lam):
    S = z.shape[0]
    tm = min(LRU_ROWS, S)
    row = lambda a: a.reshape(1, LRU_W)
    vec = pl.BlockSpec((1, LRU_W), lambda n: (0, 0))
    wspec = pl.BlockSpec((LRU_BLOCKS, LRU_BS, LRU_BS), lambda n: (0, 0, 0))
    return pl.pallas_call(
        _lru_kernel,
        grid=(S // tm,),
        in_specs=[pl.BlockSpec((tm, HEAD_W), lambda n: (n, 4)),
                  pl.BlockSpec((tm, HEAD_W), lambda n: (n, 5)),
                  pl.BlockSpec((CONV_K, LRU_W), lambda n: (0, 0)), vec, wspec, vec, wspec, vec, vec],
        out_specs=pl.BlockSpec((tm, HEAD_W), lambda n: (n, 0)),
        out_shape=jax.ShapeDtypeStruct((S, HEAD_W), BF16),
        scratch_shapes=[pltpu.VMEM((SUBLANES, LRU_W), F32), pltpu.VMEM((SUBLANES, LRU_W), F32)],
        compiler_params=_params("arbitrary"),
        name="rglru",
    )(z, z, conv_w, row(conv_b), wa, row(ba), wx, row(bx), row(lam))


def _gdn_tables():
    C = GDN_CHUNK
    i = np.arange(C)[:, None]
    j = np.arange(C)[None, :]
    levels = []
    s = 1
    while s < C:
        levels.append((i // (2 * s) == j // (2 * s)) & (i % (2 * s) >= s) & (j % (2 * s) < s))
        s *= 2
    return jnp.asarray(np.stack(levels), F32), jnp.asarray(i >= j, F32), jnp.asarray(i <= j, F32)


def _gdn_kernel(q_ref, k_ref, v_ref, z_ref, ab_ref, cw_ref, alog_ref, dtb_ref, ng_ref, lvl_ref, tri_ref,
                trit_ref, o_ref, tail_ref, state_ref):
    @pl.when(pl.program_id(0) == 0)
    def _():
        tail_ref[...] = jnp.zeros_like(tail_ref)
        state_ref[...] = jnp.zeros_like(state_ref)

    C = GDN_CHUNK
    H = GDN_HEADS
    DK = GDN_DK
    rows = q_ref.shape[0]
    incl = tri_ref[...]
    rowi = lax.broadcasted_iota(jnp.int32, (C, C), 0)
    coli = lax.broadcasted_iota(jnp.int32, (C, C), 1)
    eye = (rowi == coli).astype(F32)

    ab = ab_ref[...]
    g_all = -jnp.exp(alog_ref[...]) * _softplus(ab + dtb_ref[...])
    beta_all = jax.nn.sigmoid(ab)

    conv = []
    for p, ref in enumerate((q_ref, k_ref, v_ref)):
        x = ref[...]
        conv.append(_silu(_causal_conv(x, tail_ref[p], cw_ref[p])))
        tail_ref[p] = x[rows - SUBLANES:]
    qc, kc, vc = conv

    ones = jnp.ones((DK, DK), BF16)

    def unit_rows(x, scale):
        ssq = jnp.concatenate(
            [jnp.dot((x[:, h * DK:(h + 1) * DK] ** 2).astype(BF16), ones, preferred_element_type=F32)
             for h in range(H)], axis=1)
        return x * (lax.rsqrt(ssq + EPS) * scale)
    qc = unit_rows(qc, DK ** -0.5)
    kc = unit_rows(kc, 1.0)

    pairs = [(c, h) for c in range(rows // C) for h in range(H)]
    st = {}
    for c in range(rows // C):
        g_c = g_all[c * C:(c + 1) * C]
        st[c] = (_dot32(incl, g_c),
                 _dotg32(g_c, trit_ref[...], TN_DIMS))
    for c, h in pairs:
        rs = slice(c * C, (c + 1) * C)
        sl = slice(h * DK, (h + 1) * DK)
        q = qc[rs, sl]
        k = kc[rs, sl]
        beta = beta_all[rs, H + h:H + h + 1]
        gc = st[c][0][:, h:h + 1]
        gcr = st[c][1][h:h + 1, :]
        decay = incl * jnp.exp(jnp.where(incl > 0, gc - gcr, 0.0))
        kb = k * beta
        egc = jnp.exp(gc)
        gl = gc[C - 1:C]
        qk = _dotg(jnp.concatenate([q, kb], axis=0), k, NT_DIMS)
        st[c, h] = dict(
            attn=qk[:C] * decay,
            m=qk[C:] * (decay - eye),
            rhs=jnp.concatenate([vc[rs, sl] * beta, kb * egc], axis=1),
            qdec=q * egc,
            kdt=(k * jnp.exp(gl - gc)).T,
            egl=jnp.exp(gl))
    for p in pairs:
        st[p]['inv'] = eye - st[p]['m'] * lvl_ref[0]
    for lv in range(1, lvl_ref.shape[0]):
        for p in pairs:
            st[p]['t'] = _dot(st[p]['m'] * lvl_ref[lv], st[p]['inv'])
        for p in pairs:
            st[p]['inv'] = st[p]['inv'] - _dot(st[p]['inv'], st[p]['t'])
    for p in pairs:
        st[p]['uw'] = _dot(st[p]['inv'], st[p]['rhs'])
    for c, h in pairs:
        d = st[c, h]
        state = state_ref[h]
        ws = _dot(jnp.concatenate([d['uw'][:, DK:], d['qdec']], axis=0), state)
        v_new = d['uw'][:, :DK] - ws[:C]
        o = ws[C:] + _dot(d['attn'], v_new)
        state_ref[h] = state * d['egl'] + _dot(d['kdt'], v_new)
        rs = slice(c * C, (c + 1) * C)
        sl = slice(h * DK, (h + 1) * DK)
        o_ref[rs, sl] = (_rms(o) * ng_ref[...] * _silu(z_ref[rs, sl])).astype(o_ref.dtype)


def gdn(z, z_ab, conv_w, a_log, dt_bias, norm_g):
    S = z.shape[0]
    C = GDN_CHUNK
    H = GDN_HEADS
    lvl, tri, trit = _gdn_tables()
    alog = jnp.zeros((1, LANES), F32).at[0, :H].set(a_log)
    dtb = jnp.zeros((1, LANES), F32).at[0, :H].set(dt_bias)
    cw = conv_w.reshape(CONV_K, 3, HEAD_W).transpose(1, 0, 2)
    tm = min(GDN_ROWS, S)
    col = lambda b: pl.BlockSpec((tm, HEAD_W), lambda n, b=b: (n, b))
    lane = pl.BlockSpec((1, LANES), lambda n: (0, 0))
    return pl.pallas_call(
        _gdn_kernel,
        grid=(S // tm,),
        in_specs=[col(6), col(7), col(8), col(9),
                  pl.BlockSpec((tm, LANES), lambda n: (n, 0)),
                  pl.BlockSpec((3, CONV_K, HEAD_W), lambda n: (0, 0, 0)),
                  lane, lane, lane,
                  pl.BlockSpec(lvl.shape, lambda n: (0, 0, 0)),
                  pl.BlockSpec(tri.shape, lambda n: (0, 0)),
                  pl.BlockSpec(trit.shape, lambda n: (0, 0))],
        out_specs=pl.BlockSpec((tm, HEAD_W), lambda n: (n, 0)),
        out_shape=jax.ShapeDtypeStruct((S, HEAD_W), BF16),
        scratch_shapes=[pltpu.VMEM((3, SUBLANES, HEAD_W), F32),
                        pltpu.VMEM((H, GDN_DK, GDN_DK), F32)],
        compiler_params=_params("arbitrary"),
        name="gdn",
    )(z, z, z, z, z_ab, cw, alog, dtb, norm_g.reshape(1, GDN_DK), lvl, tri, trit)


ROUTE_LANES = LANES


def _merge_kernel(yr_ref, yl_ref, yg_ref, g0_ref, g1_ref, g2_ref, x_ref, gt_ref, bp_ref, wo_ref,
                  ng_ref, sc_ref, sh_ref, wr_ref, br_ref, xo_ref, h_ref, lg_ref):
    merged = g0_ref[...] * jnp.dot(yr_ref[...], bp_ref[0], preferred_element_type=F32)
    merged += g1_ref[...] * jnp.dot(yl_ref[...], bp_ref[1], preferred_element_type=F32)
    merged += g2_ref[...] * jnp.dot(yg_ref[...], bp_ref[2], preferred_element_type=F32)
    out = jnp.dot(merged.astype(BF16), wo_ref[...], preferred_element_type=F32)
    x = x_ref[...] + gt_ref[...] * out
    xo_ref[...] = x
    h = (_rms(x) * ng_ref[...]) * (1.0 + sc_ref[...]) + sh_ref[...]
    h_ref[...] = h
    lg_ref[...] = _dot3(h, wr_ref[...]) + br_ref[...]


def merge(y_ret, y_lru, y_gdn, gates, x, gt1, bp, w_out, norm2_g, sc2, sh2, w_route, b_route):
    S, D = x.shape
    tm = min(MERGE_ROWS, S)
    ycol = pl.BlockSpec((tm, HEAD_W), lambda i: (i, 0))
    gate = lambda b: pl.BlockSpec((tm, D), lambda i, b=b: (i, b))
    rows = pl.BlockSpec((tm, D), lambda i: (i, 0))
    vec = pl.BlockSpec((1, D), lambda i: (0, 0))
    once = dict(pipeline_mode=pl.Buffered(1))
    return pl.pallas_call(
        _merge_kernel,
        grid=(S // tm,),
        in_specs=[ycol, ycol, ycol, gate(0), gate(1), gate(2), rows, vec,
                  pl.BlockSpec((N_BRANCH, HEAD_W, D), lambda i: (0, 0, 0), **once),
                  pl.BlockSpec((D, D), lambda i: (0, 0), **once),
                  vec, vec, vec,
                  pl.BlockSpec((D, ROUTE_LANES), lambda i: (0, 0)),
                  pl.BlockSpec((1, ROUTE_LANES), lambda i: (0, 0))],
        out_specs=[rows, rows, pl.BlockSpec((tm, ROUTE_LANES), lambda i: (i, 0))],
        out_shape=[jax.ShapeDtypeStruct((S, D), F32), jax.ShapeDtypeStruct((S, D), F32),
                   jax.ShapeDtypeStruct((S, ROUTE_LANES), F32)],
        compiler_params=_params("parallel"),
        name="merge",
    )(y_ret, y_lru, y_gdn, gates, gates, gates, x, gt1, bp, w_out, norm2_g, sc2, sh2, w_route, b_route)


R_E0, R_E1, R_W0, R_W1, R_RANK0, R_RANK1 = range(6)


def _first_max(vals, lane):
    m = jnp.max(vals, axis=-1, keepdims=True)
    idx = jnp.min(jnp.where(vals == m, lane, ROUTE_LANES), axis=-1, keepdims=True)
    return m, idx


def _route_kernel(lg_ref, o_ref, cnt_ref, carry_ref):
    @pl.when(pl.program_id(0) == 0)
    def _():
        carry_ref[...] = jnp.zeros_like(carry_ref)

    lg = lg_ref[...]
    tm = lg.shape[0]
    lane = lax.broadcasted_iota(jnp.int32, lg.shape, 1)
    neg = jnp.float32(-jnp.inf)
    grp = jnp.where(lane < N_GROUPS, lg, neg)
    gmax, gsel = _first_max(grp, lane)
    pg_top = 1.0 / jnp.sum(jnp.exp(grp - gmax), axis=-1, keepdims=True)
    lo = N_GROUPS + gsel * EXPERTS_PER_GROUP
    inside = jnp.where((lane >= lo) & (lane < lo + EXPERTS_PER_GROUP), lg, neg)
    v0, i0 = _first_max(inside, lane)
    v1, i1 = _first_max(jnp.where(lane == i0, neg, inside), lane)
    e1w = jnp.exp(v1 - v0)
    w0 = pg_top / (1.0 + e1w)
    w1 = pg_top * e1w / (1.0 + e1w)
    hit0 = lane == i0
    hit1 = lane == i1
    hits = (hit0 | hit1).astype(BF16)
    row = lax.broadcasted_iota(jnp.int32, (tm, tm), 0)
    col = lax.broadcasted_iota(jnp.int32, (tm, tm), 1)
    before = jnp.dot((row > col).astype(BF16), hits, preferred_element_type=F32) + carry_ref[0:1, :]
    rank0 = jnp.sum(jnp.where(hit0, before, 0.0), axis=-1, keepdims=True)
    rank1 = jnp.sum(jnp.where(hit1, before, 0.0), axis=-1, keepdims=True)
    total = carry_ref[0:1, :] + jnp.sum(hits.astype(F32), axis=0, keepdims=True)
    carry_ref[0:1, :] = total
    cnt_ref[...] = jnp.broadcast_to(total, cnt_ref.shape)
    rec = jnp.zeros(lg.shape, F32)
    for slot, val in ((R_E0, (i0 - N_GROUPS).astype(F32)), (R_E1, (i1 - N_GROUPS).astype(F32)),
                      (R_W0, w0), (R_W1, w1), (R_RANK0, rank0), (R_RANK1, rank1)):
        rec = jnp.where(lane == slot, val, rec)
    o_ref[...] = rec


def route(logits):
    S = logits.shape[0]
    tm = min(ROUTE_ROWS, S)
    return pl.pallas_call(
        _route_kernel,
        grid=(S // tm,),
        in_specs=[pl.BlockSpec((tm, ROUTE_LANES), lambda i: (i, 0))],
        out_specs=[pl.BlockSpec((tm, ROUTE_LANES), lambda i: (i, 0)),
                   pl.BlockSpec((SUBLANES, ROUTE_LANES), lambda i: (0, 0))],
        out_shape=[jax.ShapeDtypeStruct((S, ROUTE_LANES), F32),
                   jax.ShapeDtypeStruct((SUBLANES, ROUTE_LANES), F32)],
        scratch_shapes=[pltpu.VMEM((SUBLANES, ROUTE_LANES), F32)],
        compiler_params=_params("arbitrary"),
        name="route",
    )(logits)


def _dispatch_kernel(nt, pad0_ref, padn_ref, padtot_ref, nused_ref, d0_ref, d1_ref, h_ref, xs_hbm,
                     buf, zbuf, sem, fsem):
    i = pl.program_id(0)
    tm = h_ref.shape[0]
    B = EXPERT_ROWS
    nb = xs_hbm.shape[0] // B
    slot = i % 2

    def sent_wait(s):
        for _ in range(2):
            pltpu.make_async_copy(buf.at[s], xs_hbm.at[pl.ds(0, tm)], sem.at[s]).wait()

    @pl.when(i >= 2)
    def _():
        sent_wait(slot)

    buf[slot] = h_ref[...]
    for idx_ref in (d0_ref, d1_ref):
        for r in range(tm):
            pltpu.make_async_copy(buf.at[slot, pl.ds(r, 1)], xs_hbm.at[pl.ds(idx_ref[0, r], 1)],
                                  sem.at[slot]).start()

    @pl.when(i == 0)
    def _():
        zbuf[...] = jnp.zeros(zbuf.shape, zbuf.dtype)

        def fill_block(b, c):
            pltpu.make_async_copy(zbuf, xs_hbm.at[pl.ds(pl.multiple_of(b * B, B), B)], fsem).start()
            return c
        lax.fori_loop(nused_ref[0], nb, fill_block, 0)
        for e in range(N_EXPERTS):
            def fill_row(r, c, e=e):
                pltpu.make_async_copy(zbuf.at[pl.ds(0, 1)], xs_hbm.at[pl.ds(pad0_ref[e] + r, 1)], fsem).start()
                return c
            lax.fori_loop(0, padn_ref[e], fill_row, 0)

        def wait_block(b, c):
            pltpu.make_async_copy(zbuf, xs_hbm.at[pl.ds(0, B)], fsem).wait()
            return c
        lax.fori_loop(nused_ref[0], nb, wait_block, 0)

        rows8 = pl.multiple_of((padtot_ref[0] // SUBLANES) * SUBLANES, SUBLANES)

        @pl.when(rows8 > 0)
        def _():
            pltpu.make_async_copy(xs_hbm.at[pl.ds(0, rows8)], xs_hbm.at[pl.ds(0, rows8)], fsem).wait()

        def wait_row(r, c):
            pltpu.make_async_copy(zbuf.at[pl.ds(0, 1)], xs_hbm.at[pl.ds(0, 1)], fsem).wait()
            return c
        lax.fori_loop(0, padtot_ref[0] - rows8, wait_row, 0)

    @pl.when(i == nt - 1)
    def _():
        sent_wait(slot)
        if nt >= 2:
            sent_wait(1 - slot)


def dispatch(h, dest, pad0, padn, padtot, nused, nb):
    T, D = h.shape
    tm = min(COMBINE_ROWS, T)
    nt = T // tm
    tbl = pl.BlockSpec((None, 1, tm), lambda i, *_: (i, 0, 0), memory_space=pltpu.SMEM)
    grid_spec = pltpu.PrefetchScalarGridSpec(
        num_scalar_prefetch=4,
        grid=(nt,),
        in_specs=[tbl, tbl, pl.BlockSpec((tm, D), lambda i, *_: (i, 0))],
        out_specs=pl.BlockSpec(memory_space=pl.ANY),
        scratch_shapes=[pltpu.VMEM((2, tm, D), F32), pltpu.VMEM((EXPERT_ROWS, D), F32),
                        pltpu.SemaphoreType.DMA((2,)), pltpu.SemaphoreType.DMA(())])
    return pl.pallas_call(
        functools.partial(_dispatch_kernel, nt),
        grid_spec=grid_spec,
        out_shape=jax.ShapeDtypeStruct((nb * EXPERT_ROWS, D), F32),
        compiler_params=_params("arbitrary"),
        name="dispatch",
    )(pad0, padn, padtot, nused, dest[:, 0].reshape(nt, 1, tm), dest[:, 1].reshape(nt, 1, tm), h)


def _expert_kernel(be_ref, nused_ref, xs_ref, w1_ref, w3_ref, w2_ref, o_ref, wb1, wb3, wb2):
    i = pl.program_id(0)
    nused = nused_ref[0]

    @pl.when(i < nused)
    def _():
        @pl.when((i == 0) | (be_ref[i] != be_ref[jnp.maximum(i - 1, 0)]))
        def _():
            wb1[...] = w1_ref[...].astype(BF16)
            wb3[...] = w3_ref[...].astype(BF16)
            wb2[...] = w2_ref[...].astype(BF16)

        x = xs_ref[...].astype(BF16)
        a1 = jnp.dot(x, wb1[...], preferred_element_type=F32)
        a3 = jnp.dot(x, wb3[...], preferred_element_type=F32)
        mid = (_silu(a1) * a3).astype(BF16)
        o_ref[...] = jnp.dot(mid, wb2[...], preferred_element_type=F32)

    @pl.when(i >= nused)
    def _():
        o_ref[...] = jnp.zeros(o_ref.shape, o_ref.dtype)


def experts(xs, block_e, nused, w1, w3, w2, layer):
    P, D = xs.shape
    B = EXPERT_ROWS
    nb = P // B
    wspec = lambda shp: pl.BlockSpec((None, None) + shp, lambda i, be, nu: (layer, be[i], 0, 0))
    grid_spec = pltpu.PrefetchScalarGridSpec(
        num_scalar_prefetch=2,
        grid=(nb,),
        in_specs=[pl.BlockSpec((B, D), lambda i, be, nu: (jnp.minimum(i, nu[0] - 1), 0)),
                  wspec((D, D_EXPERT)), wspec((D, D_EXPERT)), wspec((D_EXPERT, D))],
        out_specs=pl.BlockSpec((B, D), lambda i, be, nu: (i, 0)),
        scratch_shapes=[pltpu.VMEM((D, D_EXPERT), BF16), pltpu.VMEM((D, D_EXPERT), BF16),
                        pltpu.VMEM((D_EXPERT, D), BF16)])
    return pl.pallas_call(
        _expert_kernel,
        grid_spec=grid_spec,
        out_shape=jax.ShapeDtypeStruct((P, D), F32),
        compiler_params=_params("arbitrary"),
        name="experts",
    )(block_e, nused, xs, w1, w3, w2)


def dispatch_plan(rec, counts, T):
    B = EXPERT_ROWS
    nb = 2 * T // B + N_EXPERTS
    cnt = counts[0, N_GROUPS:N_GROUPS + N_EXPERTS].astype(jnp.int32)
    padded = ((cnt + B - 1) // B) * B
    pend = jnp.cumsum(padded)
    pstart = pend - padded
    e = rec[:, R_E0:R_E1 + 1].astype(jnp.int32)
    rank = rec[:, R_RANK0:R_RANK1 + 1].astype(jnp.int32)
    dest = pstart[e] + rank
    first = jnp.arange(nb, dtype=jnp.int32) * B
    block_e = jnp.minimum(jnp.sum((first[:, None] >= pend[None, :]).astype(jnp.int32), axis=1),
                          N_EXPERTS - 1)
    nused = (pend[-1] // B).astype(jnp.int32).reshape(1)
    pad0 = pstart + cnt
    padn = padded - cnt
    return block_e, nused, dest, pad0, padn, jnp.sum(padn).reshape(1), nb


def _combine_kernel(final, d0_ref, d1_ref, nd0_ref, nd1_ref, x_ref, rec_ref, gt_ref, fg_ref, y_hbm,
                    o_ref, ybuf, sem):
    i = pl.program_id(0)
    tm = x_ref.shape[0]
    slot = i % 2

    def fetch(r0_ref, r1_ref, s):
        for k, idx_ref in enumerate((r0_ref, r1_ref)):
            for r in range(tm):
                pltpu.make_async_copy(y_hbm.at[pl.ds(idx_ref[0, r], 1)], ybuf.at[s, k, pl.ds(r, 1)],
                                      sem.at[s]).start()

    @pl.when(i == 0)
    def _():
        fetch(d0_ref, d1_ref, 0)

    @pl.when(i + 1 < pl.num_programs(0))
    def _():
        fetch(nd0_ref, nd1_ref, 1 - slot)

    for k in range(2):
        pltpu.make_async_copy(y_hbm.at[pl.ds(0, tm)], ybuf.at[slot, k], sem.at[slot]).wait()
    rec = rec_ref[...]
    moe = rec[:, R_W0:R_W0 + 1] * ybuf[slot, 0] + rec[:, R_W1:R_W1 + 1] * ybuf[slot, 1]
    x = x_ref[...] + gt_ref[...] * moe
    o_ref[...] = _rms(x) * fg_ref[...] if final else x


def combine(x, y, dest, rec, gt2, final_g, final):
    S, D = x.shape
    tm = min(COMBINE_ROWS, S)
    nt = S // tm
    d0 = dest[:, 0].reshape(nt, 1, tm)
    d1 = dest[:, 1].reshape(nt, 1, tm)
    rows = pl.BlockSpec((tm, D), lambda i: (i, 0))
    vec = pl.BlockSpec((1, D), lambda i: (0, 0))
    cur = pl.BlockSpec((None, 1, tm), lambda i: (i, 0, 0), memory_space=pltpu.SMEM)
    nxt = pl.BlockSpec((None, 1, tm), lambda i: (jnp.minimum(i + 1, nt - 1), 0, 0), memory_space=pltpu.SMEM)
    return pl.pallas_call(
        functools.partial(_combine_kernel, final),
        grid=(nt,),
        in_specs=[cur, cur, nxt, nxt, rows, pl.BlockSpec((tm, ROUTE_LANES), lambda i: (i, 0)), vec, vec,
                  pl.BlockSpec(memory_space=pl.ANY)],
        out_specs=rows,
        out_shape=jax.ShapeDtypeStruct((S, D), F32),
        scratch_shapes=[pltpu.VMEM((2, 2, tm, D), F32), pltpu.SemaphoreType.DMA((2,))],
        compiler_params=_params("arbitrary"),
        name="combine",
    )(d0, d1, d0, d1, x, rec, gt2, final_g, y)


def _rope_tables(positions):
    half = RET_DK // 2
    inv_freq = ROPE_BASE ** (-jnp.arange(half, dtype=F32) * (2.0 / RET_DK))
    ang = positions.astype(F32)[:, None] * inv_freq
    cos = jnp.cos(ang)
    sin = jnp.sin(ang)
    return jnp.concatenate([cos, cos], axis=-1), jnp.concatenate([-sin, sin], axis=-1)


def _proj_weights(w_in):
    L, D, _ = w_in.shape
    w_all = w_in.astype(BF16)
    ab = w_all[:, :, MAIN_COLS:MAIN_COLS + N_AB]
    w_ab = jnp.concatenate([ab, jnp.zeros((L, D, LANES - N_AB), BF16)], axis=2)
    w_gates = w_all[:, :, MAIN_COLS + N_AB:]
    return w_all, w_ab, w_gates


def kernel(x, c, positions, ada_w, ada_b, norm1_g, norm2_g, w_in, lru_conv_w, lru_conv_b, lru_wa, lru_ba,
           lru_wx, lru_bx, lru_lambda, gdn_conv_w, gdn_a_log, gdn_dt_bias, gdn_norm_g, branch_proj, w_out,
           router_grp_w, router_grp_b, router_exp_w, router_exp_b, moe_w1, moe_w3, moe_w2, final_g):
    B, S, D = x.shape
    assert B == 1
    xs = x.reshape(S, D)
    mod = ada_mod(c, ada_w, ada_b)
    cos2, sin2 = _rope_tables(positions.reshape(S))
    row = lambda v: v.reshape(1, -1).astype(F32)
    w_all, w_ab, w_gates = _proj_weights(w_in)
    for l in range(DEPTH):
        sh1, sc1, gt1, sh2, sc2, gt2 = (row(m) for m in jnp.split(mod[l], 6))
        z, z_ab = norm_proj(xs, row(norm1_g[l]), sc1, sh1, w_all, w_ab, l)
        gates = gate_proj(xs, row(norm1_g[l]), sc1, sh1, w_gates, l)
        y_ret = retention(z, cos2, sin2)
        y_lru = rglru(z, lru_conv_w[l], lru_conv_b[l], lru_wa[l], lru_ba[l], lru_wx[l], lru_bx[l],
                      lru_lambda[l])
        y_gdn = gdn(z, z_ab, gdn_conv_w[l], gdn_a_log[l], gdn_dt_bias[l], gdn_norm_g[l])
        n_r = N_GROUPS + N_EXPERTS
        w_route = jnp.concatenate([router_grp_w[l], router_exp_w[l],
                                   jnp.zeros((D, ROUTE_LANES - n_r), F32)], axis=1)
        b_route = jnp.concatenate([router_grp_b[l], router_exp_b[l],
                                   jnp.zeros((ROUTE_LANES - n_r,), F32)]).reshape(1, ROUTE_LANES)
        xs, h2, logits = merge(y_ret, y_lru, y_gdn, gates, xs, gt1, branch_proj[l].astype(BF16),
                               w_out[l].astype(BF16), row(norm2_g[l]), sc2, sh2, w_route, b_route)
        rec, counts = route(logits)
        block_e, nused, dest, pad0, padn, padtot, nb = dispatch_plan(rec, counts, S)
        xb = dispatch(h2, dest, pad0, padn, padtot, nused, nb)
        y = experts(xb, block_e, nused, moe_w1, moe_w3, moe_w2, l)
        xs = combine(xs, y, dest, rec, gt2, row(final_g), final=(l == DEPTH - 1))
    return xs.reshape(B, S, D)
```

```python
import functools
import math

import numpy as np
import jax
import jax.numpy as jnp
from jax import lax
from jax.experimental import pallas as pl
from jax.experimental.pallas import tpu as pltpu

F32 = jnp.float32
BF16 = jnp.bfloat16
HIGHEST = lax.Precision.HIGHEST

D_MODEL = 2048
DEPTH = 2
RET_HEADS = 8
RET_DK = 128
RET_CHUNK = 128
ROPE_BASE = 10000.0
LRU_W = 1024
LRU_BLOCKS = 8
LRU_BS = LRU_W // LRU_BLOCKS
LRU_C = 8.0
CONV_K = 4
GDN_HEADS = 8
GDN_DK = 128
GDN_CHUNK = 64
N_BRANCH = 3
N_GROUPS = 4
EXPERTS_PER_GROUP = 8
N_EXPERTS = N_GROUPS * EXPERTS_PER_GROUP
D_EXPERT = 512
EPS = 1e-6

HEAD_W = 1024
MAIN_COLS = 10 * HEAD_W
N_AB = 2 * GDN_HEADS
GATE_COLS = N_BRANCH * D_MODEL
MOD_TN = 1536
PROJ_ROWS = 1024
PROJ_TN = 1280
LRU_ROWS = 256
MERGE_ROWS = 256
ROUTE_ROWS = 512
GATE_TN = 1024
GATE_ROW_CHUNK = 256
LANES = 128
SUBLANES = 8
EXPERT_ROWS = 256
DISPATCH_ROWS = 512
COMBINE_ROWS = 256
GDN_ROWS = 2 * GDN_CHUNK
RET_ROWS = 4 * RET_CHUNK
VMEM_LIMIT = 56 * 1024 * 1024

NT_DIMS = (((1,), (1,)), ((), ()))
TN_DIMS = (((0,), (0,)), ((), ()))


def _params(*sem):
    return pltpu.CompilerParams(dimension_semantics=sem, vmem_limit_bytes=VMEM_LIMIT)


def _dot32(a, b):
    return jnp.dot(a, b, preferred_element_type=F32, precision=HIGHEST)


def _dotg32(a, b, dims):
    return lax.dot_general(a, b, dims, preferred_element_type=F32, precision=HIGHEST)


def _dot(a, b):
    return jnp.dot(a.astype(BF16), b.astype(BF16), preferred_element_type=F32)


def _dotg(a, b, dims):
    return lax.dot_general(a.astype(BF16), b.astype(BF16), dims, preferred_element_type=F32)


def _dot3(a, b):
    ah = a.astype(BF16)
    bh = b.astype(BF16)
    al = (a - ah.astype(F32)).astype(BF16)
    bl = (b - bh.astype(F32)).astype(BF16)
    d = functools.partial(jnp.dot, preferred_element_type=F32)
    return d(ah, bh) + (d(ah, bl) + d(al, bh))


def _silu(x):
    return x * jax.nn.sigmoid(x)


def _softplus(x):
    return jnp.maximum(x, 0.0) + jnp.log1p(jnp.exp(-jnp.abs(x)))


def _rms(x):
    return x * lax.rsqrt(jnp.mean(x * x, axis=-1, keepdims=True) + EPS)


def _shift_rows(x, s, fill):
    n = x.shape[0]
    if s % SUBLANES == 0:
        return jnp.concatenate([jnp.full((s, x.shape[1]), fill, x.dtype), x[:n - s]], axis=0)
    row = lax.broadcasted_iota(jnp.int32, x.shape, 0)
    return jnp.where(row < s, fill, pltpu.roll(x, s, 0))


def _causal_conv(x, tail, w):
    n = x.shape[0]
    row = lax.broadcasted_iota(jnp.int32, (SUBLANES, x.shape[1]), 0)
    y = x * w[CONV_K - 1:CONV_K]
    for s in range(1, CONV_K):
        xs = pltpu.roll(x, s, 0)
        head = jnp.where(row < s, pltpu.roll(tail, s, 0), xs[:SUBLANES])
        xs = jnp.concatenate([head, xs[SUBLANES:]], axis=0) if n > SUBLANES else head
        y = y + xs * w[CONV_K - 1 - s:CONV_K - s]
    return y


def _mod_kernel(c_ref, w_ref, b_ref, o_ref):
    c = c_ref[...]
    o_ref[...] = _dot3(_silu(c), w_ref[...]) + b_ref[...]


def ada_mod(c, ada_w, ada_b):
    L, D, N = ada_w.shape
    tn = MOD_TN
    c8 = jnp.broadcast_to(c.astype(F32), (SUBLANES, D))
    out = pl.pallas_call(
        _mod_kernel,
        grid=(L, N // tn),
        in_specs=[pl.BlockSpec((SUBLANES, D), lambda l, j: (0, 0)),
                  pl.BlockSpec((None, D, tn), lambda l, j: (l, 0, j)),
                  pl.BlockSpec((None, 1, tn), lambda l, j: (l, 0, j))],
        out_specs=pl.BlockSpec((None, SUBLANES, tn), lambda l, j: (l, 0, j)),
        out_shape=jax.ShapeDtypeStruct((L, SUBLANES, N), F32),
        compiler_params=_params("parallel", "parallel"),
        name="ada_mod",
    )(c8, ada_w, ada_b.reshape(L, 1, N))
    return out[:, 0, :]


def _norm_proj_kernel(x_ref, g_ref, sc_ref, sh_ref, w_ref, wab_ref, o_ref, ab_ref, h_ref):
    @pl.when(pl.program_id(1) == 0)
    def _():
        h = (_rms(x_ref[...]) * g_ref[...]) * (1.0 + sc_ref[...]) + sh_ref[...]
        h_ref[...] = h.astype(BF16)
        ab_ref[...] = jnp.dot(h_ref[...], wab_ref[...], preferred_element_type=F32)

    o_ref[...] = jnp.dot(h_ref[...], w_ref[...], preferred_element_type=F32)


def norm_proj(x, g, sc, sh, w, w_ab, layer):
    S, D = x.shape
    tm = min(PROJ_ROWS, S)
    vec = pl.BlockSpec((1, D), lambda i, j: (0, 0))
    return pl.pallas_call(
        _norm_proj_kernel,
        grid=(S // tm, MAIN_COLS // PROJ_TN),
        in_specs=[pl.BlockSpec((tm, D), lambda i, j: (i, 0)), vec, vec, vec,
                  pl.BlockSpec((None, D, PROJ_TN), lambda i, j: (layer, 0, j)),
                  pl.BlockSpec((None, D, LANES), lambda i, j: (layer, 0, 0))],
        out_specs=[pl.BlockSpec((tm, PROJ_TN), lambda i, j: (i, j)),
                   pl.BlockSpec((tm, LANES), lambda i, j: (i, 0))],
        out_shape=[jax.ShapeDtypeStruct((S, MAIN_COLS), F32), jax.ShapeDtypeStruct((S, LANES), F32)],
        scratch_shapes=[pltpu.VMEM((tm, D), BF16)],
        compiler_params=_params("parallel", "arbitrary"),
        name="norm_proj",
    )(x, g, sc, sh, w, w_ab)


def _gate_proj_kernel(x_ref, g_ref, sc_ref, sh_ref, w_ref, o_ref, h_ref):
    @pl.when(pl.program_id(1) == 0)
    def _():
        h = (_rms(x_ref[...]) * g_ref[...]) * (1.0 + sc_ref[...]) + sh_ref[...]
        h_ref[...] = h.astype(BF16)

    rows = GATE_ROW_CHUNK
    for r in range(h_ref.shape[0] // rows):
        rs = slice(r * rows, (r + 1) * rows)
        o_ref[rs] = jax.nn.sigmoid(jnp.dot(h_ref[rs], w_ref[...], preferred_element_type=F32)).astype(BF16)


def gate_proj(x, g, sc, sh, w_gates, layer):
    S, D = x.shape
    tm = min(PROJ_ROWS, S)
    tn = GATE_TN
    vec = pl.BlockSpec((1, D), lambda i, j: (0, 0))
    return pl.pallas_call(
        _gate_proj_kernel,
        grid=(S // tm, GATE_COLS // tn),
        in_specs=[pl.BlockSpec((tm, D), lambda i, j: (i, 0)), vec, vec, vec,
                  pl.BlockSpec((None, D, tn), lambda i, j: (layer, 0, j))],
        out_specs=pl.BlockSpec((tm, tn), lambda i, j: (i, j)),
        out_shape=jax.ShapeDtypeStruct((S, GATE_COLS), BF16),
        scratch_shapes=[pltpu.VMEM((tm, D), BF16)],
        compiler_params=_params("parallel", "arbitrary"),
        name="gate_proj",
    )(x, g, sc, sh, w_gates)


def _ret_tables():
    H, C = RET_HEADS, RET_CHUNK
    log_g = np.log1p(-np.exp2(-5.0 - np.arange(H, dtype=np.float64)))
    pos = np.arange(C, dtype=np.float64)
    rel = pos[:, None] - pos[None, :]
    intra = np.where(rel >= 0, np.exp(log_g[:, None, None] * np.maximum(rel, 0.0)), 0.0)
    qd = np.exp(log_g[:, None] * (pos + 1.0))
    kd = np.exp(log_g[:, None] * (C - 1.0 - pos))
    ones = np.ones((1, 1, LANES))
    cd = [float(np.exp(lg * C)) for lg in log_g]
    return (jnp.asarray(intra, F32), jnp.asarray(qd[:, :, None] * ones, F32),
            jnp.asarray(kd[:, :, None] * ones, F32), cd)


def _ret_kernel(cd, q_ref, k_ref, v_ref, g_ref, cos_ref, sin_ref, intra_ref, qd_ref, kd_ref,
                o_ref, state_ref):
    @pl.when(pl.program_id(0) == 0)
    def _():
        state_ref[...] = jnp.zeros_like(state_ref)

    C = RET_CHUNK
    half = RET_DK // 2
    for c in range(q_ref.shape[0] // C):
        rs = slice(c * C, (c + 1) * C)
        cos2 = cos_ref[rs]
        sin2 = sin_ref[rs]
        for h in range(RET_HEADS):
            sl = slice(h * RET_DK, (h + 1) * RET_DK)
            q = q_ref[rs, sl]
            k = k_ref[rs, sl]
            v = v_ref[rs, sl]
            q = q * cos2 + pltpu.roll(q, half, 1) * sin2
            k = (k * cos2 + pltpu.roll(k, half, 1) * sin2) * (RET_DK ** -0.5)
            scores = _dotg(q, k, NT_DIMS) * intra_ref[h]
            state = state_ref[h]
            o = _dot(scores, v) + _dot(q * qd_ref[h], state)
            state_ref[h] = state * cd[h] + _dotg(k * kd_ref[h], v, TN_DIMS)
            o_ref[rs, sl] = (_rms(o) * _silu(g_ref[rs, sl])).astype(o_ref.dtype)


def retention(z, cos2, sin2):
    S = z.shape[0]
    C = min(RET_ROWS, S)
    intra, qd, kd, cd = _ret_tables()
    col = lambda b: pl.BlockSpec((C, HEAD_W), lambda n, b=b: (n, b))
    tab = lambda a: pl.BlockSpec(a.shape, lambda n: (0, 0, 0))
    rot = pl.BlockSpec((C, RET_DK), lambda n: (n, 0))
    return pl.pallas_call(
        functools.partial(_ret_kernel, cd),
        grid=(S // C,),
        in_specs=[col(0), col(1), col(2), col(3), rot, rot, tab(intra), tab(qd), tab(kd)],
        out_specs=pl.BlockSpec((C, HEAD_W), lambda n: (n, 0)),
        out_shape=jax.ShapeDtypeStruct((S, HEAD_W), BF16),
        scratch_shapes=[pltpu.VMEM((RET_HEADS, RET_DK, RET_DK), F32)],
        compiler_params=_params("arbitrary"),
        name="retention",
    )(z, z, z, z, cos2, sin2, intra, qd, kd)


def _lru_kernel(x_ref, gate_ref, cw_ref, cb_ref, wa_ref, ba_ref, wx_ref, bx_ref, lam_ref,
                o_ref, tail_ref, carry_ref):
    @pl.when(pl.program_id(0) == 0)
    def _():
        tail_ref[...] = jnp.zeros_like(tail_ref)
        carry_ref[...] = jnp.zeros_like(carry_ref)

    tm = x_ref.shape[0]
    for j in range(LRU_BLOCKS):
        sl = slice(j * LRU_BS, (j + 1) * LRU_BS)
        x = x_ref[:, sl]
        xc = _causal_conv(x, tail_ref[:, sl], cw_ref[:, sl]) + cb_ref[:, sl]
        tail_ref[:, sl] = x[tm - SUBLANES:]
        r = jax.nn.sigmoid(_dot(xc, wa_ref[j]) + ba_ref[:, sl])
        i = jax.nn.sigmoid(_dot(xc, wx_ref[j]) + bx_ref[:, sl])
        log_a = (-LRU_C * r) * _softplus(-lam_ref[:, sl])
        a = jnp.exp(log_a)
        t = jnp.tanh(log_a)
        b = jnp.sqrt(-2.0 * t / (1.0 - t)) * (i * xc)
        s = 1
        while s < tm:
            b = b + a * _shift_rows(b, s, 0.0)
            a = a * _shift_rows(a, s, 1.0)
            s *= 2
        h = b + a * carry_ref[0:1, sl]
        carry_ref[0:1, sl] = h[tm - 1:tm]
        g = gate_ref[:, sl]
        gelu = g * (0.5 * (1.0 + jnp.tanh(math.sqrt(2.0 / math.pi) * (g + 0.044715 * (g * g * g)))))
        o_ref[:, sl] = (h * gelu).astype(o_ref.dtype)


def rglru(z, conv_w, conv_b, wa, ba, wx, bx, lam):
    S = z.shape[0]
    tm = min(LRU_ROWS, S)
    row = lambda a: a.reshape(1, LRU_W)
    vec = pl.BlockSpec((1, LRU_W), lambda n: (0, 0))
    wspec = pl.BlockSpec((LRU_BLOCKS, LRU_BS, LRU_BS), lambda n: (0, 0, 0))
    return pl.pallas_call(
        _lru_kernel,
        grid=(S // tm,),
        in_specs=[pl.BlockSpec((tm, HEAD_W), lambda n: (n, 4)),
                  pl.BlockSpec((tm, HEAD_W), lambda n: (n, 5)),
                  pl.BlockSpec((CONV_K, LRU_W), lambda n: (0, 0)), vec, wspec, vec, wspec, vec, vec],
        out_specs=pl.BlockSpec((tm, HEAD_W), lambda n: (n, 0)),
        out_shape=jax.ShapeDtypeStruct((S, HEAD_W), BF16),
        scratch_shapes=[pltpu.VMEM((SUBLANES, LRU_W), F32), pltpu.VMEM((SUBLANES, LRU_W), F32)],
        compiler_params=_params("arbitrary"),
        name="rglru",
    )(z, z, conv_w, row(conv_b), wa, row(ba), wx, row(bx), row(lam))


def _gdn_tables():
    C = GDN_CHUNK
    i = np.arange(C)[:, None]
    j = np.arange(C)[None, :]
    levels = []
    s = 1
    while s < C:
        levels.append((i // (2 * s) == j // (2 * s)) & (i % (2 * s) >= s) & (j % (2 * s) < s))
        s *= 2
    return jnp.asarray(np.stack(levels), F32), jnp.asarray(i >= j, F32), jnp.asarray(i <= j, F32)


def _gdn_kernel(q_ref, k_ref, v_ref, z_ref, ab_ref, cw_ref, alog_ref, dtb_ref, ng_ref, lvl_ref, tri_ref,
                trit_ref, o_ref, tail_ref, state_ref):
    @pl.when(pl.program_id(0) == 0)
    def _():
        tail_ref[...] = jnp.zeros_like(tail_ref)
        state_ref[...] = jnp.zeros_like(state_ref)

    C = GDN_CHUNK
    H = GDN_HEADS
    DK = GDN_DK
    rows = q_ref.shape[0]
    incl = tri_ref[...]
    rowi = lax.broadcasted_iota(jnp.int32, (C, C), 0)
    coli = lax.broadcasted_iota(jnp.int32, (C, C), 1)
    eye = (rowi == coli).astype(F32)

    ab = ab_ref[...]
    g_all = -jnp.exp(alog_ref[...]) * _softplus(ab + dtb_ref[...])
    beta_all = jax.nn.sigmoid(ab)

    conv = []
    for p, ref in enumerate((q_ref, k_ref, v_ref)):
        x = ref[...]
        conv.append(_silu(_causal_conv(x, tail_ref[p], cw_ref[p])))
        tail_ref[p] = x[rows - SUBLANES:]
    qc, kc, vc = conv

    ones = jnp.ones((DK, DK), BF16)

    def unit_rows(x, scale):
        ssq = jnp.concatenate(
            [jnp.dot((x[:, h * DK:(h + 1) * DK] ** 2).astype(BF16), ones, preferred_element_type=F32)
             for h in range(H)], axis=1)
        return x * (lax.rsqrt(ssq + EPS) * scale)
    qc = unit_rows(qc, DK ** -0.5)
    kc = unit_rows(kc, 1.0)

    pairs = [(c, h) for c in range(rows // C) for h in range(H)]
    st = {}
    for c in range(rows // C):
        g_c = g_all[c * C:(c + 1) * C]
        st[c] = (_dot32(incl, g_c),
                 _dotg32(g_c, trit_ref[...], TN_DIMS))
    for c, h in pairs:
        rs = slice(c * C, (c + 1) * C)
        sl = slice(h * DK, (h + 1) * DK)
        q = qc[rs, sl]
        k = kc[rs, sl]
        beta = beta_all[rs, H + h:H + h + 1]
        gc = st[c][0][:, h:h + 1]
        gcr = st[c][1][h:h + 1, :]
        decay = incl * jnp.exp(jnp.where(incl > 0, gc - gcr, 0.0))
        kb = k * beta
        egc = jnp.exp(gc)
        gl = gc[C - 1:C]
        qk = _dotg(jnp.concatenate([q, kb], axis=0), k, NT_DIMS)
        st[c, h] = dict(
            attn=qk[:C] * decay,
            m=qk[C:] * (decay - eye),
            rhs=jnp.concatenate([vc[rs, sl] * beta, kb * egc], axis=1),
            qdec=q * egc,
            kdt=(k * jnp.exp(gl - gc)).T,
            egl=jnp.exp(gl))
    for p in pairs:
        st[p]['inv'] = eye - st[p]['m'] * lvl_ref[0]
    for lv in range(1, lvl_ref.shape[0]):
        for p in pairs:
            st[p]['t'] = _dot(st[p]['m'] * lvl_ref[lv], st[p]['inv'])
        for p in pairs:
            st[p]['inv'] = st[p]['inv'] - _dot(st[p]['inv'], st[p]['t'])
    for p in pairs:
        st[p]['uw'] = _dot(st[p]['inv'], st[p]['rhs'])
    for c, h in pairs:
        d = st[c, h]
        state = state_ref[h]
        ws = _dot(jnp.concatenate([d['uw'][:, DK:], d['qdec']], axis=0), state)
        v_new = d['uw'][:, :DK] - ws[:C]
        o = ws[C:] + _dot(d['attn'], v_new)
        state_ref[h] = state * d['egl'] + _dot(d['kdt'], v_new)
        rs = slice(c * C, (c + 1) * C)
        sl = slice(h * DK, (h + 1) * DK)
        o_ref[rs, sl] = (_rms(o) * ng_ref[...] * _silu(z_ref[rs, sl])).astype(o_ref.dtype)


def gdn(z, z_ab, conv_w, a_log, dt_bias, norm_g):
    S = z.shape[0]
    C = GDN_CHUNK
    H = GDN_HEADS
    lvl, tri, trit = _gdn_tables()
    alog = jnp.zeros((1, LANES), F32).at[0, :H].set(a_log)
    dtb = jnp.zeros((1, LANES), F32).at[0, :H].set(dt_bias)
    cw = conv_w.reshape(CONV_K, 3, HEAD_W).transpose(1, 0, 2)
    tm = min(GDN_ROWS, S)
    col = lambda b: pl.BlockSpec((tm, HEAD_W), lambda n, b=b: (n, b))
    lane = pl.BlockSpec((1, LANES), lambda n: (0, 0))
    return pl.pallas_call(
        _gdn_kernel,
        grid=(S // tm,),
        in_specs=[col(6), col(7), col(8), col(9),
                  pl.BlockSpec((tm, LANES), lambda n: (n, 0)),
                  pl.BlockSpec((3, CONV_K, HEAD_W), lambda n: (0, 0, 0)),
                  lane, lane, lane,
                  pl.BlockSpec(lvl.shape, lambda n: (0, 0, 0)),
                  pl.BlockSpec(tri.shape, lambda n: (0, 0)),
                  pl.BlockSpec(trit.shape, lambda n: (0, 0))],
        out_specs=pl.BlockSpec((tm, HEAD_W), lambda n: (n, 0)),
        out_shape=jax.ShapeDtypeStruct((S, HEAD_W), BF16),
        scratch_shapes=[pltpu.VMEM((3, SUBLANES, HEAD_W), F32),
                        pltpu.VMEM((H, GDN_DK, GDN_DK), F32)],
        compiler_params=_params("arbitrary"),
        name="gdn",
    )(z, z, z, z, z_ab, cw, alog, dtb, norm_g.reshape(1, GDN_DK), lvl, tri, trit)


ROUTE_LANES = LANES


def _merge_kernel(yr_ref, yl_ref, yg_ref, g0_ref, g1_ref, g2_ref, x_ref, gt_ref, bp_ref, wo_ref,
                  ng_ref, sc_ref, sh_ref, wr_ref, br_ref, xo_ref, h_ref, lg_ref):
    merged = g0_ref[...] * jnp.dot(yr_ref[...], bp_ref[0], preferred_element_type=F32)
    merged += g1_ref[...] * jnp.dot(yl_ref[...], bp_ref[1], preferred_element_type=F32)
    merged += g2_ref[...] * jnp.dot(yg_ref[...], bp_ref[2], preferred_element_type=F32)
    out = jnp.dot(merged.astype(BF16), wo_ref[...], preferred_element_type=F32)
    x = x_ref[...] + gt_ref[...] * out
    xo_ref[...] = x
    h = (_rms(x) * ng_ref[...]) * (1.0 + sc_ref[...]) + sh_ref[...]
    h_ref[...] = h
    lg_ref[...] = _dot3(h, wr_ref[...]) + br_ref[...]


def merge(y_ret, y_lru, y_gdn, gates, x, gt1, bp, w_out, norm2_g, sc2, sh2, w_route, b_route):
    S, D = x.shape
    tm = min(MERGE_ROWS, S)
    ycol = pl.BlockSpec((tm, HEAD_W), lambda i: (i, 0))
    gate = lambda b: pl.BlockSpec((tm, D), lambda i, b=b: (i, b))
    rows = pl.BlockSpec((tm, D), lambda i: (i, 0))
    vec = pl.BlockSpec((1, D), lambda i: (0, 0))
    once = dict(pipeline_mode=pl.Buffered(1))
    return pl.pallas_call(
        _merge_kernel,
        grid=(S // tm,),
        in_specs=[ycol, ycol, ycol, gate(0), gate(1), gate(2), rows, vec,
                  pl.BlockSpec((N_BRANCH, HEAD_W, D), lambda i: (0, 0, 0), **once),
                  pl.BlockSpec((D, D), lambda i: (0, 0), **once),
                  vec, vec, vec,
                  pl.BlockSpec((D, ROUTE_LANES), lambda i: (0, 0)),
                  pl.BlockSpec((1, ROUTE_LANES), lambda i: (0, 0))],
        out_specs=[rows, rows, pl.BlockSpec((tm, ROUTE_LANES), lambda i: (i, 0))],
        out_shape=[jax.ShapeDtypeStruct((S, D), F32), jax.ShapeDtypeStruct((S, D), F32),
                   jax.ShapeDtypeStruct((S, ROUTE_LANES), F32)],
        compiler_params=_params("parallel"),
        name="merge",
    )(y_ret, y_lru, y_gdn, gates, gates, gates, x, gt1, bp, w_out, norm2_g, sc2, sh2, w_route, b_route)


R_E0, R_E1, R_W0, R_W1, R_RANK0, R_RANK1 = range(6)


def _first_max(vals, lane):
    m = jnp.max(vals, axis=-1, keepdims=True)
    idx = jnp.min(jnp.where(vals == m, lane, ROUTE_LANES), axis=-1, keepdims=True)
    return m, idx


def _route_kernel(lg_ref, o_ref, cnt_ref, carry_ref):
    @pl.when(pl.program_id(0) == 0)
    def _():
        carry_ref[...] = jnp.zeros_like(carry_ref)

    lg = lg_ref[...]
    tm = lg.shape[0]
    lane = lax.broadcasted_iota(jnp.int32, lg.shape, 1)
    neg = jnp.float32(-jnp.inf)
    grp = jnp.where(lane < N_GROUPS, lg, neg)
    gmax, gsel = _first_max(grp, lane)
    pg_top = 1.0 / jnp.sum(jnp.exp(grp - gmax), axis=-1, keepdims=True)
    lo = N_GROUPS + gsel * EXPERTS_PER_GROUP
    inside = jnp.where((lane >= lo) & (lane < lo + EXPERTS_PER_GROUP), lg, neg)
    v0, i0 = _first_max(inside, lane)
    v1, i1 = _first_max(jnp.where(lane == i0, neg, inside), lane)
    e1w = jnp.exp(v1 - v0)
    w0 = pg_top / (1.0 + e1w)
    w1 = pg_top * e1w / (1.0 + e1w)
    hit0 = lane == i0
    hit1 = lane == i1
    hits = (hit0 | hit1).astype(BF16)
    row = lax.broadcasted_iota(jnp.int32, (tm, tm), 0)
    col = lax.broadcasted_iota(jnp.int32, (tm, tm), 1)
    before = jnp.dot((row > col).astype(BF16), hits, preferred_element_type=F32) + carry_ref[0:1, :]
    rank0 = jnp.sum(jnp.where(hit0, before, 0.0), axis=-1, keepdims=True)
    rank1 = jnp.sum(jnp.where(hit1, before, 0.0), axis=-1, keepdims=True)
    total = carry_ref[0:1, :] + jnp.sum(hits.astype(F32), axis=0, keepdims=True)
    carry_ref[0:1, :] = total
    cnt_ref[...] = jnp.broadcast_to(total, cnt_ref.shape)
    rec = jnp.zeros(lg.shape, F32)
    for slot, val in ((R_E0, (i0 - N_GROUPS).astype(F32)), (R_E1, (i1 - N_GROUPS).astype(F32)),
                      (R_W0, w0), (R_W1, w1), (R_RANK0, rank0), (R_RANK1, rank1)):
        rec = jnp.where(lane == slot, val, rec)
    o_ref[...] = rec


def route(logits):
    S = logits.shape[0]
    tm = min(ROUTE_ROWS, S)
    return pl.pallas_call(
        _route_kernel,
        grid=(S // tm,),
        in_specs=[pl.BlockSpec((tm, ROUTE_LANES), lambda i: (i, 0))],
        out_specs=[pl.BlockSpec((tm, ROUTE_LANES), lambda i: (i, 0)),
                   pl.BlockSpec((SUBLANES, ROUTE_LANES), lambda i: (0, 0))],
        out_shape=[jax.ShapeDtypeStruct((S, ROUTE_LANES), F32),
                   jax.ShapeDtypeStruct((SUBLANES, ROUTE_LANES), F32)],
        scratch_shapes=[pltpu.VMEM((SUBLANES, ROUTE_LANES), F32)],
        compiler_params=_params("arbitrary"),
        name="route",
    )(logits)


def _dispatch_kernel(nt, pad0_ref, padn_ref, padtot_ref, nused_ref, d0_ref, d1_ref, h_ref, xs_hbm,
                     buf, zbuf, sem, fsem):
    i = pl.program_id(0)
    tm = h_ref.shape[0]
    B = EXPERT_ROWS
    nb = xs_hbm.shape[0] // B
    slot = i % 2

    def sent_wait(s):
        for _ in range(2):
            pltpu.make_async_copy(buf.at[s], xs_hbm.at[pl.ds(0, tm)], sem.at[s]).wait()

    @pl.when(i >= 2)
    def _():
        sent_wait(slot)

    buf[slot] = h_ref[...]
    for idx_ref in (d0_ref, d1_ref):
        for r in range(tm):
            pltpu.make_async_copy(buf.at[slot, pl.ds(r, 1)], xs_hbm.at[pl.ds(idx_ref[0, r], 1)],
                                  sem.at[slot]).start()

    @pl.when(i == 0)
    def _():
        zbuf[...] = jnp.zeros(zbuf.shape, zbuf.dtype)

        def fill_block(b, c):
            pltpu.make_async_copy(zbuf, xs_hbm.at[pl.ds(pl.multiple_of(b * B, B), B)], fsem).start()
            return c
        lax.fori_loop(nused_ref[0], nb, fill_block, 0)
        for e in range(N_EXPERTS):
            def fill_row(r, c, e=e):
                pltpu.make_async_copy(zbuf.at[pl.ds(0, 1)], xs_hbm.at[pl.ds(pad0_ref[e] + r, 1)], fsem).start()
                return c
            lax.fori_loop(0, padn_ref[e], fill_row, 0)

        def wait_block(b, c):
            pltpu.make_async_copy(zbuf, xs_hbm.at[pl.ds(0, B)], fsem).wait()
            return c
        lax.fori_loop(nused_ref[0], nb, wait_block, 0)

        rows8 = pl.multiple_of((padtot_ref[0] // SUBLANES) * SUBLANES, SUBLANES)

        @pl.when(rows8 > 0)
        def _():
            pltpu.make_async_copy(xs_hbm.at[pl.ds(0, rows8)], xs_hbm.at[pl.ds(0, rows8)], fsem).wait()

        def wait_row(r, c):
            pltpu.make_async_copy(zbuf.at[pl.ds(0, 1)], xs_hbm.at[pl.ds(0, 1)], fsem).wait()
            return c
        lax.fori_loop(0, padtot_ref[0] - rows8, wait_row, 0)

    @pl.when(i == nt - 1)
    def _():
        sent_wait(slot)
        if nt >= 2:
            sent_wait(1 - slot)


def dispatch(h, dest, pad0, padn, padtot, nused, nb):
    T, D = h.shape
    tm = min(DISPATCH_ROWS, T)
    nt = T // tm
    tbl = pl.BlockSpec((None, 1, tm), lambda i, *_: (i, 0, 0), memory_space=pltpu.SMEM)
    grid_spec = pltpu.PrefetchScalarGridSpec(
        num_scalar_prefetch=4,
        grid=(nt,),
        in_specs=[tbl, tbl, pl.BlockSpec((tm, D), lambda i, *_: (i, 0))],
        out_specs=pl.BlockSpec(memory_space=pl.ANY),
        scratch_shapes=[pltpu.VMEM((2, tm, D), F32), pltpu.VMEM((EXPERT_ROWS, D), F32),
                        pltpu.SemaphoreType.DMA((2,)), pltpu.SemaphoreType.DMA(())])
    return pl.pallas_call(
        functools.partial(_dispatch_kernel, nt),
        grid_spec=grid_spec,
        out_shape=jax.ShapeDtypeStruct((nb * EXPERT_ROWS, D), F32),
        compiler_params=_params("arbitrary"),
        name="dispatch",
    )(pad0, padn, padtot, nused, dest[:, 0].reshape(nt, 1, tm), dest[:, 1].reshape(nt, 1, tm), h)


def _expert_kernel(be_ref, nused_ref, xs_ref, w1_ref, w3_ref, w2_ref, o_ref, wb1, wb3, wb2):
    i = pl.program_id(0)
    nused = nused_ref[0]

    @pl.when(i < nused)
    def _():
        @pl.when((i == 0) | (be_ref[i] != be_ref[jnp.maximum(i - 1, 0)]))
        def _():
            wb1[...] = w1_ref[...].astype(BF16)
            wb3[...] = w3_ref[...].astype(BF16)
            wb2[...] = w2_ref[...].astype(BF16)

        x = xs_ref[...].astype(BF16)
        a1 = jnp.dot(x, wb1[...], preferred_element_type=F32)
        a3 = jnp.dot(x, wb3[...], preferred_element_type=F32)
        mid = (_silu(a1) * a3).astype(BF16)
        o_ref[...] = jnp.dot(mid, wb2[...], preferred_element_type=F32)

    @pl.when(i >= nused)
    def _():
        o_ref[...] = jnp.zeros(o_ref.shape, o_ref.dtype)


def experts(xs, block_e, nused, w1, w3, w2, layer):
    P, D = xs.shape
    B = EXPERT_ROWS
    nb = P // B
    wspec = lambda shp: pl.BlockSpec((None, None) + shp, lambda i, be, nu: (layer, be[i], 0, 0))
    grid_spec = pltpu.PrefetchScalarGridSpec(
        num_scalar_prefetch=2,
        grid=(nb,),
        in_specs=[pl.BlockSpec((B, D), lambda i, be, nu: (jnp.minimum(i, nu[0] - 1), 0)),
                  wspec((D, D_EXPERT)), wspec((D, D_EXPERT)), wspec((D_EXPERT, D))],
        out_specs=pl.BlockSpec((B, D), lambda i, be, nu: (i, 0)),
        scratch_shapes=[pltpu.VMEM((D, D_EXPERT), BF16), pltpu.VMEM((D, D_EXPERT), BF16),
                        pltpu.VMEM((D_EXPERT, D), BF16)])
    return pl.pallas_call(
        _expert_kernel,
        grid_spec=grid_spec,
        out_shape=jax.ShapeDtypeStruct((P, D), F32),
        compiler_params=_params("arbitrary"),
        name="experts",
    )(block_e, nused, xs, w1, w3, w2)


def dispatch_plan(rec, counts, T):
    B = EXPERT_ROWS
    nb = 2 * T // B + N_EXPERTS
    cnt = counts[0, N_GROUPS:N_GROUPS + N_EXPERTS].astype(jnp.int32)
    padded = ((cnt + B - 1) // B) * B
    pend = jnp.cumsum(padded)
    pstart = pend - padded
    e = rec[:, R_E0:R_E1 + 1].astype(jnp.int32)
    rank = rec[:, R_RANK0:R_RANK1 + 1].astype(jnp.int32)
    dest = pstart[e] + rank
    first = jnp.arange(nb, dtype=jnp.int32) * B
    block_e = jnp.minimum(jnp.sum((first[:, None] >= pend[None, :]).astype(jnp.int32), axis=1),
                          N_EXPERTS - 1)
    nused = (pend[-1] // B).astype(jnp.int32).reshape(1)
    pad0 = pstart + cnt
    padn = padded - cnt
    return block_e, nused, dest, pad0, padn, jnp.sum(padn).reshape(1), nb


def _combine_kernel(final, d0_ref, d1_ref, nd0_ref, nd1_ref, x_ref, rec_ref, gt_ref, fg_ref, y_hbm,
                    o_ref, ybuf, sem):
    i = pl.program_id(0)
    tm = x_ref.shape[0]
    slot = i % 2

    def fetch(r0_ref, r1_ref, s):
        for k, idx_ref in enumerate((r0_ref, r1_ref)):
            for r in range(tm):
                pltpu.make_async_copy(y_hbm.at[pl.ds(idx_ref[0, r], 1)], ybuf.at[s, k, pl.ds(r, 1)],
                                      sem.at[s]).start()

    @pl.when(i == 0)
    def _():
        fetch(d0_ref, d1_ref, 0)

    @pl.when(i + 1 < pl.num_programs(0))
    def _():
        fetch(nd0_ref, nd1_ref, 1 - slot)

    for k in range(2):
        pltpu.make_async_copy(y_hbm.at[pl.ds(0, tm)], ybuf.at[slot, k], sem.at[slot]).wait()
    rec = rec_ref[...]
    moe = rec[:, R_W0:R_W0 + 1] * ybuf[slot, 0] + rec[:, R_W1:R_W1 + 1] * ybuf[slot, 1]
    x = x_ref[...] + gt_ref[...] * moe
    o_ref[...] = _rms(x) * fg_ref[...] if final else x


def combine(x, y, dest, rec, gt2, final_g, final):
    S, D = x.shape
    tm = min(COMBINE_ROWS, S)
    nt = S // tm
    d0 = dest[:, 0].reshape(nt, 1, tm)
    d1 = dest[:, 1].reshape(nt, 1, tm)
    rows = pl.BlockSpec((tm, D), lambda i: (i, 0))
    vec = pl.BlockSpec((1, D), lambda i: (0, 0))
    cur = pl.BlockSpec((None, 1, tm), lambda i: (i, 0, 0), memory_space=pltpu.SMEM)
    nxt = pl.BlockSpec((None, 1, tm), lambda i: (jnp.minimum(i + 1, nt - 1), 0, 0), memory_space=pltpu.SMEM)
    return pl.pallas_call(
        functools.partial(_combine_kernel, final),
        grid=(nt,),
        in_specs=[cur, cur, nxt, nxt, rows, pl.BlockSpec((tm, ROUTE_LANES), lambda i: (i, 0)), vec, vec,
                  pl.BlockSpec(memory_space=pl.ANY)],
        out_specs=rows,
        out_shape=jax.ShapeDtypeStruct((S, D), F32),
        scratch_shapes=[pltpu.VMEM((2, 2, tm, D), F32), pltpu.SemaphoreType.DMA((2,))],
        compiler_params=_params("arbitrary"),
        name="combine",
    )(d0, d1, d0, d1, x, rec, gt2, final_g, y)


def _rope_tables(positions):
    half = RET_DK // 2
    inv_freq = ROPE_BASE ** (-jnp.arange(half, dtype=F32) * (2.0 / RET_DK))
    ang = positions.astype(F32)[:, None] * inv_freq
    cos = jnp.cos(ang)
    sin = jnp.sin(ang)
    return jnp.concatenate([cos, cos], axis=-1), jnp.concatenate([-sin, sin], axis=-1)


def _proj_weights(w_in):
    L, D, _ = w_in.shape
    w_all = w_in.astype(BF16)
    ab = w_all[:, :, MAIN_COLS:MAIN_COLS + N_AB]
    w_ab = jnp.concatenate([ab, jnp.zeros((L, D, LANES - N_AB), BF16)], axis=2)
    w_gates = w_all[:, :, MAIN_COLS + N_AB:]
    return w_all, w_ab, w_gates


def kernel(x, c, positions, ada_w, ada_b, norm1_g, norm2_g, w_in, lru_conv_w, lru_conv_b, lru_wa, lru_ba,
           lru_wx, lru_bx, lru_lambda, gdn_conv_w, gdn_a_log, gdn_dt_bias, gdn_norm_g, branch_proj, w_out,
           router_grp_w, router_grp_b, router_exp_w, router_exp_b, moe_w1, moe_w3, moe_w2, final_g):
    B, S, D = x.shape
    assert B == 1
    xs = x.reshape(S, D)
    mod = ada_mod(c, ada_w, ada_b)
    cos2, sin2 = _rope_tables(positions.reshape(S))
    row = lambda v: v.reshape(1, -1).astype(F32)
    w_all, w_ab, w_gates = _proj_weights(w_in)
    for l in range(DEPTH):
        sh1, sc1, gt1, sh2, sc2, gt2 = (row(m) for m in jnp.split(mod[l], 6))
        z, z_ab = norm_proj(xs, row(norm1_g[l]), sc1, sh1, w_all, w_ab, l)
        gates = gate_proj(xs, row(norm1_g[l]), sc1, sh1, w_gates, l)
        y_ret = retention(z, cos2, sin2)
        y_lru = rglru(z, lru_conv_w[l], lru_conv_b[l], lru_wa[l], lru_ba[l], lru_wx[l], lru_bx[l],
                      lru_lambda[l])
        y_gdn = gdn(z, z_ab, gdn_conv_w[l], gdn_a_log[l], gdn_dt_bias[l], gdn_norm_g[l])
        n_r = N_GROUPS + N_EXPERTS
        w_route = jnp.concatenate([router_grp_w[l], router_exp_w[l],
                                   jnp.zeros((D, ROUTE_LANES - n_r), F32)], axis=1)
        b_route = jnp.concatenate([router_grp_b[l], router_exp_b[l],
                                   jnp.zeros((ROUTE_LANES - n_r,), F32)]).reshape(1, ROUTE_LANES)
        xs, h2, logits = merge(y_ret, y_lru, y_gdn, gates, xs, gt1, branch_proj[l].astype(BF16),
                               w_out[l].astype(BF16), row(norm2_g[l]), sc2, sh2, w_route, b_route)
        rec, counts = route(logits)
        block_e, nused, dest, pad0, padn, padtot, nb = dispatch_plan(rec, counts, S)
        xb = dispatch(h2, dest, pad0, padn, padtot, nused, nb)
        y = experts(xb, block_e, nused, moe_w1, moe_w3, moe_w2, l)
        xs = combine(xs, y, dest, rec, gt2, row(final_g), final=(l == DEPTH - 1))
    return xs.reshape(B, S, D)
```
